```python
import jax, jax.numpy as jnp
from jax import lax
import numpy as np

D_MODEL = 1024
BATCH = 16
SEQ = 2048
DEPTH = 4

HEAD_DIM = 64
N_HEADS_MIX = 8
W_MIX = N_HEADS_MIX * HEAD_DIM
N_BRANCH = 3
DILATED_PAIRS = ((128, 1), (512, 4), (2048, 16))
QBLK = 128
ROT_DIM = HEAD_DIM // 4
ROPE_THETA = 500000.0
GRID_W = 64
NA_WIN_ROWS = 8
NA_WIN_COLS = 16
DECAY_LORA = 64
ICLR_LORA = 64
GATE_LORA = 128
DECAY_SCALE = 0.6065306597126334
D_FF = ((8 * D_MODEL // 3 + 255) // 256) * 256
N_MOD = 9
RMS_EPS = 1e-6
GN_EPS = 64e-5
NEG = -1e30
IN_SPLITS = (W_MIX, W_MIX, W_MIX,
             W_MIX, W_MIX, W_MIX,
             W_MIX, W_MIX, W_MIX,
             DECAY_LORA, DECAY_LORA,
             ICLR_LORA, ICLR_LORA,
             GATE_LORA,
             D_MODEL, D_MODEL, D_MODEL)
D_IN = sum(IN_SPLITS)

kernel_name = "hybrid_dilated_natten_rwkv7_encoder"


def rmsnorm(x, g):
    xf = x.astype(jnp.float32)
    y = xf * lax.rsqrt(jnp.mean(xf * xf, axis=-1, keepdims=True) + RMS_EPS)
    return (y * g.astype(jnp.float32)).astype(x.dtype)


def modulate(h, shift, scale):
    return h * (1 + scale[:, None, :]) + shift[:, None, :]


def swiglu(h, wi, wo):
    gate, up = jnp.split(h @ wi, 2, axis=-1)
    return (jax.nn.silu(gate) * up) @ wo


def partial_rotary(t, positions):
    half = ROT_DIM // 2
    inv_freq = ROPE_THETA ** (-jnp.arange(half, dtype=jnp.float32) * 2.0 / ROT_DIM)
    ang = positions.astype(jnp.float32)[..., None] * inv_freq
    cos, sin = jnp.cos(ang)[:, :, None, :], jnp.sin(ang)[:, :, None, :]
    tr = t[..., :ROT_DIM].astype(jnp.float32)
    t1, t2 = tr[..., :half], tr[..., half:]
    rot = jnp.concatenate([t1 * cos - t2 * sin, t2 * cos + t1 * sin], axis=-1)
    return jnp.concatenate([rot.astype(t.dtype), t[..., ROT_DIM:]], axis=-1)


def banded_attention(q, k, v, radius):
    lead = q.shape[:-2]
    L, hd = q.shape[-2], q.shape[-1]
    nb = -(-L // QBLK)
    lp = nb * QBLK
    kw = QBLK + 2 * radius
    pad = [(0, 0)] * len(lead)
    qb = jnp.pad(q, pad + [(0, lp - L), (0, 0)]).reshape(*lead, nb, QBLK, hd)
    kp = jnp.pad(k, pad + [(radius, radius + lp - L), (0, 0)])
    vp = jnp.pad(v, pad + [(radius, radius + lp - L), (0, 0)])
    idx = np.arange(nb)[:, None] * QBLK + np.arange(kw)[None, :]
    kb = kp[..., idx, :]
    vb = vp[..., idx, :]
    s = jnp.einsum("...nqd,...nkd->...nqk", qb, kb).astype(jnp.float32) * (hd ** -0.5)
    key_pos = idx - radius
    q_pos = np.arange(nb)[:, None] * QBLK + np.arange(QBLK)[None, :]
    off = key_pos[:, None, :] - q_pos[:, :, None]
    valid = (np.abs(off) <= radius) & (key_pos[:, None, :] >= 0) & (key_pos[:, None, :] < L)
    s = jnp.where(valid, s, NEG)
    m = jnp.max(s, axis=-1, keepdims=True)
    p = jnp.exp(s - m)
    den = jnp.sum(p, axis=-1, keepdims=True)
    o = jnp.einsum("...nqk,...nkd->...nqd", (p / den).astype(v.dtype), vb)
    lse = (m + jnp.log(den))[..., 0]
    return o.reshape(*lead, lp, hd)[..., :L, :], lse.reshape(*lead, lp)[..., :L]


def dilated_attention(q, k, v):
    B, S, H, hd = q.shape
    q, k, v = (t.transpose(0, 2, 1, 3) for t in (q, k, v))
    outs, lses = [], []
    for window, dil in DILATED_PAIRS:
        radius = window // (2 * dil)
        to_cls = lambda t: t.reshape(B, H, S // dil, dil, hd).swapaxes(2, 3)
        o, lse = banded_attention(to_cls(q), to_cls(k), to_cls(v), radius)
        outs.append(o.swapaxes(2, 3).reshape(B, H, S, hd))
        lses.append(lse.swapaxes(2, 3).reshape(B, H, S))
    alpha = jax.nn.softmax(jnp.stack(lses), axis=0)
    o = jnp.einsum("gbhs,gbhsd->bhsd", alpha, jnp.stack(outs).astype(jnp.float32))
    return o.astype(v.dtype).transpose(0, 2, 1, 3).reshape(B, S, H * hd)


def neighborhood_attention(q, k, v, rpb):
    B, S, H, hd = q.shape
    rows = S // GRID_W
    wr = min(NA_WIN_ROWS, rows)
    wc = NA_WIN_COLS
    grid = lambda t: t.reshape(B, rows, GRID_W, H, hd).transpose(0, 3, 1, 2, 4)
    qg, kg, vg = grid(q), grid(k), grid(v)
    r_start = np.clip(np.arange(rows) - wr // 2, 0, rows - wr)
    row_idx = r_start[:, None] + np.arange(wr)[None, :]
    kr = kg[:, :, row_idx]
    vr = vg[:, :, row_idx]
    s = jnp.einsum("bhrqd,bhrikd->bhrqik", qg, kr).astype(jnp.float32) * (hd ** -0.5)
    cols = np.arange(GRID_W)
    c_start = np.clip(cols - wc // 2, 0, GRID_W - wc)
    col_in = (cols[None, :] >= c_start[:, None]) & (cols[None, :] < c_start[:, None] + wc)
    roff = row_idx - np.arange(rows)[:, None] + NA_WIN_ROWS - 1
    coff = np.clip(cols[None, :] - cols[:, None], -(wc - 1), wc - 1) + wc - 1
    bias = rpb[:, roff[:, None, :, None], coff[None, :, None, :]]
    s = jnp.where(col_in[:, None, :], s + bias[None].astype(jnp.float32), NEG)
    shp = s.shape
    p = jax.nn.softmax(s.reshape(*shp[:-2], shp[-2] * shp[-1]), axis=-1).reshape(shp)
    o = jnp.einsum("bhrqik,bhrikd->bhrqd", p.astype(v.dtype), vr)
    return o.transpose(0, 2, 3, 1, 4).reshape(B, S, H * hd)


def _neighbour_pair(tf, tb):
    prev = jnp.pad(tf, ((0, 0), (1, 0), (0, 0)))[:, :-1]
    nxt = jnp.pad(tb, ((0, 0), (0, 1), (0, 0)))[:, 1:]
    return jnp.stack([prev, nxt])


def _token_shift(tf, tb, mu):
    base = jnp.stack([tf, tb])
    return base + (_neighbour_pair(tf, tb) - base) * mu[:, None, None, :]


def _flip_bwd(t):
    return jnp.stack([t[0], jnp.flip(t[1], axis=1)])


def _rwkv_step(state, inp):
    r, w, k, v, a_vec, b_vec = inp
    sa = jnp.einsum("dbhvk,dbhk->dbhv", state, a_vec)
    state = state * w[..., None, :] + sa[..., None] * b_vec[..., None, :] + v[..., None] * k[..., None, :]
    y = jnp.einsum("dbhvk,dbhk->dbhv", state, r)
    return state, y


def rwkv7_bidirectional(pr, pk, pv, pwf, pwb, paf, pab, pg, mu_rkv, mu_w, mu_a,
                        w0, w_up, a0, a_up, g_up, k_k, k_a, r_k, gn_w, gn_b):
    B, S, C = pr.shape
    H, N = N_HEADS_MIX, HEAD_DIM
    f32 = jnp.float32
    r = _token_shift(pr, pr, mu_rkv[:, 0])
    k = _token_shift(pk, pk, mu_rkv[:, 1])
    v = _token_shift(pv, pv, mu_rkv[:, 2])
    xw = _token_shift(pwf, pwb, mu_w)
    xa = _token_shift(paf, pab, mu_a)
    wz = w0[:, None, None, :] + jnp.einsum("dbsr,drc->dbsc", jnp.tanh(xw), w_up)
    decay = jnp.exp(-DECAY_SCALE * jax.nn.sigmoid(wz.astype(f32)))
    a = jax.nn.sigmoid((a0[:, None, None, :] + jnp.einsum("dbsr,drc->dbsc", xa, a_up)).astype(f32))
    hs = lambda t: t.reshape(2, B, S, H, N).astype(f32)
    r, k, v, decay, a = hs(r), hs(k), hs(v), hs(decay), hs(a)
    kk = k * k_k.reshape(H, N).astype(f32)
    kk = kk / jnp.maximum(jnp.sqrt(jnp.sum(kk * kk, axis=-1, keepdims=True)), 1e-12)
    k = k * (1 + (a - 1) * k_a.reshape(H, N).astype(f32))
    tm = lambda t: jnp.moveaxis(_flip_bwd(t), 2, 0)
    init = jnp.zeros((2, B, H, N, N), f32)
    _, ys = lax.scan(_rwkv_step, init, (tm(r), tm(decay), tm(k), tm(v), tm(-kk), tm(kk * a)))
    y = _flip_bwd(jnp.moveaxis(ys, 0, 2))
    mean = jnp.mean(y, axis=-1, keepdims=True)
    var = jnp.mean(jnp.square(y - mean), axis=-1, keepdims=True)
    yn = ((y - mean) * lax.rsqrt(var + GN_EPS)).reshape(2, B, S, C) * gn_w.astype(f32) + gn_b.astype(f32)
    bonus = (jnp.sum(r * k * r_k.astype(f32), axis=-1, keepdims=True) * v).reshape(2, B, S, C)
    g = jax.nn.sigmoid(pg) @ g_up
    return (jnp.sum(yn + bonus, axis=0) * g).astype(pr.dtype)


def hybrid_mixer(h, positions, w_in, rpb, mu_rkv, mu_w, mu_a, w0, w_up, a0, a_up, g_up,
                 k_k, k_a, r_k, gn_w, gn_b, w_branch, w_out):
    B, S, _ = h.shape
    proj = h @ w_in
    (aq, ak, av, nq, nk, nv, pr, pk, pv, pwf, pwb, paf, pab, pg, gz0, gz1, gz2) = jnp.split(
        proj, np.cumsum(IN_SPLITS)[:-1].tolist(), axis=-1)
    heads = lambda t: t.reshape(B, S, N_HEADS_MIX, HEAD_DIM)
    o_a = dilated_attention(partial_rotary(heads(aq), positions), partial_rotary(heads(ak), positions), heads(av))
    o_b = neighborhood_attention(heads(nq), heads(nk), heads(nv), rpb)
    o_c = rwkv7_bidirectional(pr, pk, pv, pwf, pwb, paf, pab, pg, mu_rkv, mu_w, mu_a,
                              w0, w_up, a0, a_up, g_up, k_k, k_a, r_k, gn_w, gn_b)
    merged = (jax.nn.sigmoid(gz0) * (o_a @ w_branch[0])
              + jax.nn.sigmoid(gz1) * (o_b @ w_branch[1])
              + jax.nn.sigmoid(gz2) * (o_c @ w_branch[2]))
    return merged @ w_out


def setup_inputs(seed: int = 0) -> dict:
    key = jax.random.key(seed)
    ks = jax.random.split(key, 28)
    L, D = DEPTH, D_MODEL
    f32 = jnp.float32
    nrm = lambda k, shape, scale: jax.random.normal(k, shape, f32) * scale
    x = nrm(ks[0], (BATCH, SEQ, D), 1.0)
    c = nrm(ks[1], (BATCH, D), 1.0)
    positions = (jnp.arange(SEQ, dtype=jnp.int32)[None, :]
                 + jax.random.randint(ks[2], (BATCH, 1), 0, 4096, dtype=jnp.int32))
    ada_w = nrm(ks[3], (L, D, N_MOD * D), 0.5 * D ** -0.5)
    ada_b = nrm(ks[4], (L, N_MOD * D), 0.02)
    norm_gains = 1.0 + nrm(ks[5], (L, 3, D), 0.05)
    ffn_wi = nrm(ks[6], (L, 2, D, 2 * D_FF), D ** -0.5)
    ffn_wo = nrm(ks[7], (L, 2, D_FF, D), D_FF ** -0.5)
    w_in = nrm(ks[8], (L, D, D_IN), D ** -0.5)
    rpb = nrm(ks[9], (L, N_HEADS_MIX, 2 * NA_WIN_ROWS - 1, 2 * NA_WIN_COLS - 1), 0.2)
    mu_rkv = jax.random.uniform(ks[10], (L, 2, 3, W_MIX), f32)
    mu_w = jax.random.uniform(ks[11], (L, 2, DECAY_LORA), f32)
    mu_a = jax.random.uniform(ks[12], (L, 2, ICLR_LORA), f32)
    w0 = nrm(ks[13], (L, 2, W_MIX), 1.5)
    w_up = nrm(ks[14], (L, 2, DECAY_LORA, W_MIX), DECAY_LORA ** -0.5)
    a0 = nrm(ks[15], (L, 2, W_MIX), 0.5)
    a_up = nrm(ks[16], (L, 2, ICLR_LORA, W_MIX), ICLR_LORA ** -0.5)
    g_up = nrm(ks[17], (L, GATE_LORA, W_MIX), GATE_LORA ** -0.5)
    k_k = 0.85 + nrm(ks[18], (L, W_MIX), 0.05)
    k_a = 1.0 + nrm(ks[19], (L, W_MIX), 0.05)
    r_k = nrm(ks[20], (L, N_HEADS_MIX, HEAD_DIM), 0.1)
    gn_w = 1.0 + nrm(ks[21], (L, W_MIX), 0.05)
    gn_b = nrm(ks[22], (L, W_MIX), 0.02)
    w_branch = nrm(ks[23], (L, N_BRANCH, W_MIX, D), W_MIX ** -0.5)
    w_out = nrm(ks[24], (L, D, D), D ** -0.5)
    final_norm = 1.0 + nrm(ks[25], (D,), 0.05)
    return {"x": x, "c": c, "positions": positions, "ada_w": ada_w, "ada_b": ada_b,
            "norm_gains": norm_gains, "ffn_wi": ffn_wi, "ffn_wo": ffn_wo, "w_in": w_in,
            "rpb": rpb, "mu_rkv": mu_rkv, "mu_w": mu_w, "mu_a": mu_a, "w0": w0, "w_up": w_up,
            "a0": a0, "a_up": a_up, "g_up": g_up, "k_k": k_k, "k_a": k_a, "r_k": r_k,
            "gn_w": gn_w, "gn_b": gn_b, "w_branch": w_branch, "w_out": w_out,
            "final_norm": final_norm}


def reference(x, c, positions, ada_w, ada_b, norm_gains, ffn_wi, ffn_wo, w_in, rpb,
              mu_rkv, mu_w, mu_a, w0, w_up, a0, a_up, g_up, k_k, k_a, r_k, gn_w, gn_b,
              w_branch, w_out, final_norm):
    cs = jax.nn.silu(c)
    for l in range(DEPTH):
        mod = cs @ ada_w[l] + ada_b[l]
        sh1, sc1, gt1, sh2, sc2, gt2, sh3, sc3, gt3 = jnp.split(mod, N_MOD, axis=-1)
        h = modulate(rmsnorm(x, norm_gains[l, 0]), sh1, sc1)
        x = x + 0.5 * gt1[:, None, :] * swiglu(h, ffn_wi[l, 0], ffn_wo[l, 0])
        h = modulate(rmsnorm(x, norm_gains[l, 1]), sh2, sc2)
        x = x + gt2[:, None, :] * hybrid_mixer(
            h, positions, w_in[l], rpb[l], mu_rkv[l], mu_w[l], mu_a[l], w0[l], w_up[l],
            a0[l], a_up[l], g_up[l], k_k[l], k_a[l], r_k[l], gn_w[l], gn_b[l],
            w_branch[l], w_out[l])
        h = modulate(rmsnorm(x, norm_gains[l, 2]), sh3, sc3)
        x = x + 0.5 * gt3[:, None, :] * swiglu(h, ffn_wi[l, 1], ffn_wo[l, 1])
    return rmsnorm(x, final_norm)
```

```python
import functools

import numpy as np
import jax
import jax.numpy as jnp
from jax import lax
from jax.experimental import pallas as pl
from jax.experimental.pallas import tpu as pltpu

F32 = jnp.float32
BF16 = jnp.bfloat16

HEAD_DIM = 64
N_HEADS = 8
W_MIX = N_HEADS * HEAD_DIM
DILATED_PAIRS = ((128, 1), (512, 4), (2048, 16))
QBLK = 128
ROT_DIM = HEAD_DIM // 4
ROPE_THETA = 500000.0
GRID_W = 64
NA_WIN_ROWS = 8
NA_WIN_COLS = 16
DECAY_LORA = 64
ICLR_LORA = 64
GATE_LORA = 128
DECAY_SCALE = 0.6065306597126334
N_MOD = 9
RMS_EPS = 1e-6
GN_EPS = 64e-5
NEG = -1e30

LANES = 128
SUBLANES = 8
VMEM_LIMIT = 48 * 1024 * 1024

COLBLK = 512
CB_AQ, CB_AK, CB_AV = 0, 1, 2
CB_NQ, CB_NK, CB_NV = 3, 4, 5
CB_PR, CB_PK, CB_PV = 6, 7, 8
CB_LORA = 9
CB_GZ = 10
D_IN_PAD = 16 * COLBLK


def _cparams(sem):
    return pltpu.CompilerParams(dimension_semantics=sem, vmem_limit_bytes=VMEM_LIMIT)


def _mod_kernel(c_ref, w_ref, b_ref, o_ref):
    c = c_ref[...]
    cs = (c * jax.nn.sigmoid(c)).astype(BF16)
    o_ref[0] = jnp.dot(cs, w_ref[0].astype(BF16), preferred_element_type=F32) + b_ref[0]


def _modulation(c, ada_w, ada_b):
    depth, d, nd = ada_w.shape
    b = c.shape[0]
    return pl.pallas_call(
        _mod_kernel,
        grid=(depth, nd // d),
        in_specs=[pl.BlockSpec((b, d), lambda l, j: (0, 0)),
                  pl.BlockSpec((1, d, d), lambda l, j: (l, 0, j)),
                  pl.BlockSpec((1, 1, d), lambda l, j: (l, 0, j))],
        out_specs=pl.BlockSpec((1, b, d), lambda l, j: (l, 0, j)),
        out_shape=jax.ShapeDtypeStruct((depth, b, nd), F32),
        compiler_params=_cparams(("parallel", "parallel")),
        name="adaln_mod",
    )(c, ada_w, ada_b.reshape(depth, 1, nd))


def _norm_mod(x, gain, shift, scale):
    ms = jnp.mean(x * x, axis=-1, keepdims=True)
    y = x * lax.rsqrt(ms + RMS_EPS) * gain
    return y * (1.0 + scale) + shift


def _ffn_kernel(*refs, final):
    if final:
        (x_ref, g_ref, sh_ref, sc_ref, gt_ref, wg_ref, wu_ref, wo_ref, fin_ref,
         o_ref, h_scr, acc_scr) = refs
    else:
        (x_ref, g_ref, sh_ref, sc_ref, gt_ref, wg_ref, wu_ref, wo_ref,
         o_ref, h_scr, acc_scr) = refs
    j = pl.program_id(2)

    @pl.when(j == 0)
    def _():
        h = _norm_mod(x_ref[0], g_ref[...], sh_ref[0], sc_ref[0])
        h_scr[...] = h.astype(BF16)
        acc_scr[...] = jnp.zeros_like(acc_scr)

    h = h_scr[...]
    gate = jnp.dot(h, wg_ref[...], preferred_element_type=F32)
    up = jnp.dot(h, wu_ref[...], preferred_element_type=F32)
    act = (gate * jax.nn.sigmoid(gate)) * up
    acc_scr[...] += jnp.dot(act.astype(BF16), wo_ref[...], preferred_element_type=F32)

    @pl.when(j == pl.num_programs(2) - 1)
    def _():
        y = x_ref[0] + 0.5 * gt_ref[0] * acc_scr[...]
        if final:
            ms = jnp.mean(y * y, axis=-1, keepdims=True)
            y = y * lax.rsqrt(ms + RMS_EPS) * fin_ref[...]
        o_ref[0] = y


def _ffn(x, gain, shift, scale, gate, wi, wo, final_gain=None, *, tm=1024, tf=256):
    b, s, d = x.shape
    dff = wo.shape[0]
    nj = dff // tf
    final = final_gain is not None
    vec = pl.BlockSpec((1, 1, d), lambda bi, i, j: (bi, 0, 0))
    in_specs = [pl.BlockSpec((1, tm, d), lambda bi, i, j: (bi, i, 0)),
                pl.BlockSpec((1, d), lambda bi, i, j: (0, 0)),
                vec, vec, vec,
                pl.BlockSpec((d, tf), lambda bi, i, j: (0, j)),
                pl.BlockSpec((d, tf), lambda bi, i, j: (0, j + nj)),
                pl.BlockSpec((tf, d), lambda bi, i, j: (j, 0))]
    args = [x, gain.reshape(1, d), shift, scale, gate, wi, wi, wo]
    if final:
        in_specs.append(pl.BlockSpec((1, d), lambda bi, i, j: (0, 0)))
        args.append(final_gain.reshape(1, d))
    return pl.pallas_call(
        functools.partial(_ffn_kernel, final=final),
        grid=(b, s // tm, nj),
        in_specs=in_specs,
        out_specs=pl.BlockSpec((1, tm, d), lambda bi, i, j: (bi, i, 0)),
        out_shape=jax.ShapeDtypeStruct((b, s, d), F32),
        scratch_shapes=[pltpu.VMEM((tm, d), BF16), pltpu.VMEM((tm, d), F32)],
        compiler_params=_cparams(("parallel", "parallel", "arbitrary")),
        name="ffn_final" if final else "ffn",
    )(*args)


def _win_kernel(x_ref, g_ref, sh_ref, sc_ref, w_ref, o_ref, h_scr):
    @pl.when(pl.program_id(2) == 0)
    def _():
        h = _norm_mod(x_ref[0], g_ref[...], sh_ref[0], sc_ref[0])
        h_scr[...] = h.astype(BF16)

    o_ref[0] = jnp.dot(h_scr[...], w_ref[...], preferred_element_type=F32)


def _in_proj(x, gain, shift, scale, w, *, tm=1024, tn=COLBLK):
    b, s, d = x.shape
    n = w.shape[1]
    vec = pl.BlockSpec((1, 1, d), lambda bi, i, j: (bi, 0, 0))
    return pl.pallas_call(
        _win_kernel,
        grid=(b, s // tm, n // tn),
        in_specs=[pl.BlockSpec((1, tm, d), lambda bi, i, j: (bi, i, 0)),
                  pl.BlockSpec((1, d), lambda bi, i, j: (0, 0)),
                  vec, vec,
                  pl.BlockSpec((d, tn), lambda bi, i, j: (0, j))],
        out_specs=pl.BlockSpec((1, tm, tn), lambda bi, i, j: (bi, i, j)),
        out_shape=jax.ShapeDtypeStruct((b, s, n), F32),
        scratch_shapes=[pltpu.VMEM((tm, d), BF16)],
        compiler_params=_cparams(("parallel", "parallel", "arbitrary")),
        name="in_proj",
    )(x, gain.reshape(1, d), shift, scale, w)


def _dilated_kernel(q_ref, k_ref, v_ref, cos_ref, s1_ref, s2_ref, o_ref,
                    q_scr, k_scr, v_scr, og_scr, lse_scr, *, seq, pad):
    cos, s1, s2 = cos_ref[0], s1_ref[0], s2_ref[0]

    def rotary(t):
        return (t * cos + pltpu.roll(t, LANES - ROT_DIM // 2, axis=1) * s1
                + pltpu.roll(t, ROT_DIM // 2, axis=1) * s2)

    q_scr[...] = rotary(q_ref[0])
    zeros = jnp.zeros((pad, LANES), F32)
    k_scr[pl.ds(0, pad), :] = zeros
    k_scr[pl.ds(pad + seq, pad), :] = zeros
    v_scr[pl.ds(0, pad), :] = zeros
    v_scr[pl.ds(pad + seq, pad), :] = zeros
    k_scr[pl.ds(pad, seq), :] = rotary(k_ref[0])
    v_scr[pl.ds(pad, seq), :] = v_ref[0]

    lane = lax.broadcasted_iota(jnp.int32, (QBLK, LANES), 1)
    head0 = lane < HEAD_DIM
    qi = lax.broadcasted_iota(jnp.int32, (QBLK, 2 * QBLK), 0)
    kj = lax.broadcasted_iota(jnp.int32, (QBLK, 2 * QBLK), 1)
    scale = HEAD_DIM ** -0.5

    for g, (window, dil) in enumerate(DILATED_PAIRS):
        radius = window // (2 * dil)
        cls_len = seq // dil
        nblk = cls_len // QBLK
        band = jnp.abs(kj - radius - qi) <= radius

        def block(idx, carry, g=g, dil=dil, radius=radius, cls_len=cls_len, nblk=nblk, band=band):
            cls = idx // nblk
            n = idx % nblk
            q_start = cls + dil * QBLK * n
            k_start = pad + cls + dil * (QBLK * n - radius)
            if dil == 1:
                q_rows = pl.ds(q_start, QBLK)
                k_rows = pl.ds(k_start, 2 * QBLK)
            else:
                q_rows = pl.ds(q_start, QBLK, stride=dil)
                k_rows = pl.ds(k_start, 2 * QBLK, stride=dil)
            q = q_scr[q_rows, :]
            kw = k_scr[k_rows, :].astype(BF16)
            vw = v_scr[k_rows, :].astype(BF16)
            key_pos = QBLK * n + kj - radius
            valid = band & (key_pos >= 0) & (key_pos < cls_len)
            outs, lses = [], []
            for hmask in (head0, jnp.logical_not(head0)):
                qh = jnp.where(hmask, q, 0.0).astype(BF16)
                s = lax.dot_general(qh, kw, (((1,), (1,)), ((), ())),
                                    preferred_element_type=F32) * scale
                s = jnp.where(valid, s, NEG)
                m = jnp.max(s, axis=-1, keepdims=True)
                p = jnp.exp(s - m)
                den = jnp.sum(p, axis=-1, keepdims=True)
                outs.append(jnp.dot((p / den).astype(BF16), vw, preferred_element_type=F32))
                lses.append(m + jnp.log(den))
            og_scr[g, q_rows, :] = jnp.where(head0, outs[0], outs[1])
            lse_scr[g, q_rows, :] = jnp.where(head0, lses[0], lses[1])
            return carry

        lax.fori_loop(0, dil * nblk, block, 0)

    l0, l1, l2 = lse_scr[0], lse_scr[1], lse_scr[2]
    m = jnp.maximum(jnp.maximum(l0, l1), l2)
    e0, e1, e2 = jnp.exp(l0 - m), jnp.exp(l1 - m), jnp.exp(l2 - m)
    tot = e0 + e1 + e2
    o_ref[0] = (e0 / tot) * og_scr[0] + (e1 / tot) * og_scr[1] + (e2 / tot) * og_scr[2]


def _dilated_attention(proj, cos, s1, s2):
    b, s, _ = proj.shape
    pad = max((w // (2 * d)) * d for w, d in DILATED_PAIRS)
    hp = W_MIX // LANES
    per_cb = COLBLK // LANES
    col = lambda cb: pl.BlockSpec((1, s, LANES), lambda bi, h, cb=cb: (bi, 0, cb * per_cb + h))
    tab = pl.BlockSpec((1, s, LANES), lambda bi, h: (bi, 0, 0))
    return pl.pallas_call(
        functools.partial(_dilated_kernel, seq=s, pad=pad),
        grid=(b, hp),
        in_specs=[col(CB_AQ), col(CB_AK), col(CB_AV), tab, tab, tab],
        out_specs=pl.BlockSpec((1, s, LANES), lambda bi, h: (bi, 0, h)),
        out_shape=jax.ShapeDtypeStruct((b, s, W_MIX), F32),
        scratch_shapes=[pltpu.VMEM((s, LANES), F32),
                        pltpu.VMEM((s + 2 * pad, LANES), F32),
                        pltpu.VMEM((s + 2 * pad, LANES), F32),
                        pltpu.VMEM((len(DILATED_PAIRS), s, LANES), F32),
                        pltpu.VMEM((len(DILATED_PAIRS), s, LANES), F32)],
        compiler_params=_cparams(("parallel", "parallel")),
        name="dilated_attn",
    )(proj, proj, proj, cos, s1, s2)


def _rope_tables(positions):
    half = ROT_DIM // 2
    inv_freq = ROPE_THETA ** (-jnp.arange(half, dtype=F32) * 2.0 / ROT_DIM)
    ang = positions.astype(F32)[..., None] * inv_freq
    cos, sin = jnp.cos(ang), jnp.sin(ang)
    lead = ang.shape[:-1]
    ones = jnp.ones(lead + (HEAD_DIM - ROT_DIM,), F32)
    z = lambda n: jnp.zeros(lead + (n,), F32)
    c64 = jnp.concatenate([cos, cos, ones], axis=-1)
    s1 = jnp.concatenate([-sin, z(HEAD_DIM - half)], axis=-1)
    s2 = jnp.concatenate([z(half), sin, z(HEAD_DIM - ROT_DIM)], axis=-1)
    rep = lambda t: jnp.concatenate([t] * (LANES // HEAD_DIM), axis=-1)
    return rep(c64), rep(s1), rep(s2)


def _natten_kernel(q_ref, k_ref, v_ref, bias_ref, o_ref, *, rows, win_rows):
    lane = lax.broadcasted_iota(jnp.int32, (GRID_W, LANES), 1)
    head0 = lane < HEAD_DIM
    scale = HEAD_DIM ** -0.5
    nkeys = win_rows * GRID_W

    def row(r, carry):
        r_start = jnp.clip(r - win_rows // 2, 0, rows - win_rows)
        delta = r - r_start
        q_rows = pl.ds(pl.multiple_of(r * GRID_W, GRID_W), GRID_W)
        k_rows = pl.ds(pl.multiple_of(r_start * GRID_W, GRID_W), nkeys)
        q = q_ref[0, q_rows, :]
        kw = k_ref[0, k_rows, :].astype(BF16)
        vw = v_ref[0, k_rows, :].astype(BF16)
        outs = []
        for hh, hmask in enumerate((head0, jnp.logical_not(head0))):
            qh = jnp.where(hmask, q, 0.0).astype(BF16)
            s = lax.dot_general(qh, kw, (((1,), (1,)), ((), ())),
                                preferred_element_type=F32) * scale
            s = s + bias_ref[hh, delta]
            m = jnp.max(s, axis=-1, keepdims=True)
            p = jnp.exp(s - m)
            den = jnp.sum(p, axis=-1, keepdims=True)
            outs.append(jnp.dot((p / den).astype(BF16), vw, preferred_element_type=F32))
        o_ref[0, q_rows, :] = jnp.where(head0, outs[0], outs[1])
        return carry

    lax.fori_loop(0, rows, row, 0)


def _natten_bias_table(rpb, rows):
    wr = min(NA_WIN_ROWS, rows)
    wc = NA_WIN_COLS
    cols = np.arange(GRID_W)
    c_start = np.clip(cols - wc // 2, 0, GRID_W - wc)
    col_in = (cols[None, :] >= c_start[:, None]) & (cols[None, :] < c_start[:, None] + wc)
    coff = np.clip(cols[None, :] - cols[:, None], -(wc - 1), wc - 1) + wc - 1
    roff = np.arange(wr)[None, :] - np.arange(wr)[:, None] + NA_WIN_ROWS - 1
    bias = rpb[:, roff[:, None, :, None], coff[None, :, None, :]]
    bias = jnp.where(col_in[None, None, :, None, :], bias.astype(F32), NEG)
    return bias.reshape(rpb.shape[0], wr, GRID_W, wr * GRID_W)


def _neighborhood_attention(proj, bias_tab):
    b, s, _ = proj.shape
    rows = s // GRID_W
    wr = min(NA_WIN_ROWS, rows)
    hp = W_MIX // LANES
    per_cb = COLBLK // LANES
    col = lambda cb: pl.BlockSpec((1, s, LANES), lambda bi, h, cb=cb: (bi, 0, cb * per_cb + h))
    return pl.pallas_call(
        functools.partial(_natten_kernel, rows=rows, win_rows=wr),
        grid=(b, hp),
        in_specs=[col(CB_NQ), col(CB_NK), col(CB_NV),
                  pl.BlockSpec((LANES // HEAD_DIM, wr, GRID_W, wr * GRID_W),
                               lambda bi, h: (h, 0, 0, 0))],
        out_specs=pl.BlockSpec((1, s, LANES), lambda bi, h: (bi, 0, h)),
        out_shape=jax.ShapeDtypeStruct((b, s, W_MIX), F32),
        compiler_params=_cparams(("parallel", "parallel")),
        name="natten",
    )(proj, proj, proj, bias_tab)


def _seg_sum(x, ones_bd):
    hi = x.astype(BF16)
    lo = (x - hi.astype(F32)).astype(BF16)
    return (jnp.dot(hi, ones_bd, preferred_element_type=F32)
            + jnp.dot(lo, ones_bd, preferred_element_type=F32))


def _rwkv_pre_kernel(r_ref, k_ref, v_ref, l_ref,
                     rp_ref, kp_ref, vp_ref, lp_ref,
                     rn_ref, kn_ref, vn_ref, ln_ref,
                     mu_ref, mul_ref, wc_ref, w0_ref, a0_ref, kk_ref, ka_ref, rk_ref, ones_ref,
                     r_o, w_o, k_o, v_o, a_o, b_o, bonus_o, *, tm):
    i = pl.program_id(1)
    first = i == 0
    last = i == pl.num_programs(1) - 1
    row = lax.broadcasted_iota(jnp.int32, (tm, COLBLK), 0)
    lane = lax.broadcasted_iota(jnp.int32, (tm, COLBLK), 1)
    tanh_cols = lane < 2 * DECAY_LORA
    ones_bd = ones_ref[...]

    def neighbours(x_ref, p_ref, n_ref):
        x = x_ref[0]
        prev_row = jnp.where(first, 0.0, p_ref[0, SUBLANES - 1:SUBLANES, :])
        next_row = jnp.where(last, 0.0, n_ref[0, 0:1, :])
        prev = jnp.where(row == 0, prev_row, pltpu.roll(x, 1, axis=0))
        nxt = jnp.where(row == tm - 1, next_row, pltpu.roll(x, tm - 1, axis=0))
        return x, (prev, nxt)

    pr, nb_r = neighbours(r_ref, rp_ref, rn_ref)
    pk, nb_k = neighbours(k_ref, kp_ref, kn_ref)
    pv, nb_v = neighbours(v_ref, vp_ref, vn_ref)
    pl_, nb_l = neighbours(l_ref, lp_ref, ln_ref)

    bonus = jnp.zeros((tm, COLBLK), F32)
    for d in range(2):
        r = pr + (nb_r[d] - pr) * mu_ref[d, 0:1, :]
        k = pk + (nb_k[d] - pk) * mu_ref[d, 1:2, :]
        v = pv + (nb_v[d] - pv) * mu_ref[d, 2:3, :]
        xl = pl_ + (nb_l[d] - pl_) * mul_ref[d:d + 1, :]
        tz = jnp.where(tanh_cols, jnp.tanh(xl), xl).astype(BF16)
        z = jnp.dot(tz, wc_ref[d], preferred_element_type=F32)
        wz = w0_ref[d:d + 1, :] + z[:, :COLBLK]
        az = a0_ref[d:d + 1, :] + z[:, COLBLK:]
        decay = jnp.exp(-DECAY_SCALE * jax.nn.sigmoid(wz))
        a = jax.nn.sigmoid(az)
        kk = k * kk_ref[...]
        norm = jnp.sqrt(_seg_sum(kk * kk, ones_bd))
        kk = kk / jnp.maximum(norm, 1e-12)
        k2 = k * (1.0 + (a - 1.0) * ka_ref[...])
        bonus = bonus + _seg_sum(r * k2 * rk_ref[...], ones_bd) * v
        r_o[d, 0] = r
        w_o[d, 0] = decay
        k_o[d, 0] = k2
        v_o[d, 0] = v
        a_o[d, 0] = -kk
        b_o[d, 0] = kk * a
    bonus_o[0] = bonus


def _rwkv_pre(proj, mu_rkv, mu_lora, w_comb, w0, a0, k_k, k_a, r_k, ones_bd, *, tm=256):
    b, s, _ = proj.shape
    nsub = tm // SUBLANES
    last_sub = s // SUBLANES - 1
    main = lambda cb: pl.BlockSpec((1, tm, COLBLK), lambda bi, i, cb=cb: (bi, i, cb))
    prev = lambda cb: pl.BlockSpec(
        (1, SUBLANES, COLBLK), lambda bi, i, cb=cb: (bi, jnp.maximum(i * nsub - 1, 0), cb))
    nxt = lambda cb: pl.BlockSpec(
        (1, SUBLANES, COLBLK), lambda bi, i, cb=cb: (bi, jnp.minimum((i + 1) * nsub, last_sub), cb))
    cbs = (CB_PR, CB_PK, CB_PV, CB_LORA)
    const = lambda shape: pl.BlockSpec(shape, lambda bi, i: (0,) * len(shape))
    dir_out = pl.BlockSpec((2, 1, tm, COLBLK), lambda bi, i: (0, bi, i, 0))
    dir_shape = jax.ShapeDtypeStruct((2, b, s, COLBLK), F32)
    return pl.pallas_call(
        functools.partial(_rwkv_pre_kernel, tm=tm),
        grid=(b, s // tm),
        in_specs=([main(cb) for cb in cbs] + [prev(cb) for cb in cbs] + [nxt(cb) for cb in cbs]
                  + [const((2, 3, COLBLK)), const((2, COLBLK)), const((2, COLBLK, 2 * COLBLK)),
                     const((2, COLBLK)), const((2, COLBLK)), const((1, COLBLK)), const((1, COLBLK)),
                     const((1, COLBLK)), const((COLBLK, COLBLK))]),
        out_specs=[dir_out] * 6 + [pl.BlockSpec((1, tm, COLBLK), lambda bi, i: (bi, i, 0))],
        out_shape=[dir_shape] * 6 + [jax.ShapeDtypeStruct((b, s, COLBLK), F32)],
        compiler_params=_cparams(("parallel", "parallel")),
        name="rwkv_pre",
    )(*([proj] * 12), mu_rkv, mu_lora, w_comb, w0, a0, k_k, k_a, r_k, ones_bd)


def _rwkv_scan_kernel(r_ref, w_ref, k_ref, v_ref, a_ref, b_ref, y_ref, st_scr, *, tt):
    d = pl.program_id(0)
    n = HEAD_DIM

    @pl.when(pl.program_id(1) == 0)
    def _():
        st_scr[...] = jnp.zeros_like(st_scr)

    def step(i, carry):
        t = i + d * (tt - 1 - 2 * i)
        a, w, b = a_ref[0, t], w_ref[0, t], b_ref[0, t]
        kv, r, v = k_ref[0, t], r_ref[0, t], v_ref[0, t]
        sa = [jnp.zeros_like(v), jnp.zeros_like(v)]
        for k in range(n):
            sa[k % 2] = sa[k % 2] + st_scr[k] * a[k:k + 1, :]
        sa = sa[0] + sa[1]
        y = [jnp.zeros_like(v), jnp.zeros_like(v)]
        for k in range(n):
            new = st_scr[k] * w[k:k + 1, :] + (sa * b[k:k + 1, :] + v * kv[k:k + 1, :])
            st_scr[k] = new
            y[k % 2] = y[k % 2] + new * r[k:k + 1, :]
        y_ref[0, t] = y[0] + y[1]
        return carry

    lax.fori_loop(0, tt, step, 0)


def _rwkv_scan(r, w, k, v, a, b, *, tt=8):
    _, s, n, c = r.shape
    nt = s // tt
    spec = pl.BlockSpec((1, tt, n, c), lambda d, i: (d, i + d * (nt - 1 - 2 * i), 0, 0))
    return pl.pallas_call(
        functools.partial(_rwkv_scan_kernel, tt=tt),
        grid=(2, nt),
        in_specs=[spec] * 6,
        out_specs=spec,
        out_shape=jax.ShapeDtypeStruct((2, s, n, c), F32),
        scratch_shapes=[pltpu.VMEM((n, n, c), F32)],
        compiler_params=_cparams(("parallel", "arbitrary")),
        name="rwkv_scan",
    )(r, w, k, v, a, b)


def _to_chain_lanes(t):
    _, b, s, _ = t.shape
    t = t.reshape(2, b, s, N_HEADS, HEAD_DIM).transpose(0, 2, 4, 1, 3)
    return t.reshape(2, s, HEAD_DIM, b * N_HEADS)


def _from_chain_lanes(t, b):
    _, s, _, _ = t.shape
    t = t.reshape(2, s, HEAD_DIM, b, N_HEADS).transpose(0, 3, 1, 4, 2)
    return t.reshape(2, b, s, W_MIX)


def _merge_kernel(y_ref, bonus_ref, l_ref, gz0_ref, gz1_ref, gz2_ref, oa_ref, ob_ref, x_ref, gt_ref,
                  gnw_ref, gnb_ref, ones_ref, gc_ref, wb_ref, wo_ref, o_ref):
    ones_bd = ones_ref[...]
    inv_n = 1.0 / HEAD_DIM
    acc = bonus_ref[0]
    for d in range(2):
        y = y_ref[d, 0]
        mean = _seg_sum(y, ones_bd) * inv_n
        yc = y - mean
        var = _seg_sum(yc * yc, ones_bd) * inv_n
        acc = acc + (yc * lax.rsqrt(var + GN_EPS)) * gnw_ref[...] + gnb_ref[...]
    g = jnp.dot(jax.nn.sigmoid(l_ref[0]).astype(BF16), gc_ref[...], preferred_element_type=F32)
    o_c = acc * g
    merged = (jax.nn.sigmoid(gz0_ref[0])
              * jnp.dot(oa_ref[0].astype(BF16), wb_ref[0], preferred_element_type=F32)
              + jax.nn.sigmoid(gz1_ref[0])
              * jnp.dot(ob_ref[0].astype(BF16), wb_ref[1], preferred_element_type=F32)
              + jax.nn.sigmoid(gz2_ref[0])
              * jnp.dot(o_c.astype(BF16), wb_ref[2], preferred_element_type=F32))
    out = jnp.dot(merged.astype(BF16), wo_ref[...], preferred_element_type=F32)
    o_ref[0] = x_ref[0] + gt_ref[0] * out


def _merge(y, bonus, proj, o_a, o_b, x, gate, gn_w, gn_b, ones_bd, g_comb, w_branch, w_out, *, tm=512):
    b, s, d = x.shape
    gz_per = d // COLBLK
    tok = lambda width, cb: pl.BlockSpec((1, tm, width), lambda bi, i, cb=cb: (bi, i, cb))
    const = lambda shape: pl.BlockSpec(shape, lambda bi, i: (0,) * len(shape))
    return pl.pallas_call(
        _merge_kernel,
        grid=(b, s // tm),
        in_specs=[pl.BlockSpec((2, 1, tm, COLBLK), lambda bi, i: (0, bi, i, 0)),
                  tok(COLBLK, 0),
                  tok(COLBLK, CB_LORA),
                  tok(d, CB_GZ // gz_per), tok(d, CB_GZ // gz_per + 1), tok(d, CB_GZ // gz_per + 2),
                  tok(COLBLK, 0), tok(COLBLK, 0), tok(d, 0),
                  pl.BlockSpec((1, 1, d), lambda bi, i: (bi, 0, 0)),
                  const((1, COLBLK)), const((1, COLBLK)), const((COLBLK, COLBLK)),
                  const((COLBLK, COLBLK)), const((3, COLBLK, d)), const((d, d))],
        out_specs=pl.BlockSpec((1, tm, d), lambda bi, i: (bi, i, 0)),
        out_shape=jax.ShapeDtypeStruct((b, s, d), F32),
        compiler_params=_cparams(("parallel", "parallel")),
        name="merge_out",
    )(y, bonus, proj, proj, proj, proj, o_a, o_b, x, gate, gn_w, gn_b, ones_bd, g_comb, w_branch, w_out)


def _pad_w_in(w_in):
    d = w_in.shape[0]
    n_front = CB_LORA * COLBLK + 4 * DECAY_LORA + GATE_LORA
    front, gates = w_in[:, :n_front], w_in[:, n_front:]
    zpad = jnp.zeros((d, CB_GZ * COLBLK - n_front), w_in.dtype)
    return jnp.concatenate([front, zpad, gates], axis=1).astype(BF16)


def _lora_params(mu_w, mu_a, w_up, a_up, g_up):
    r = DECAY_LORA
    mu_l = jnp.zeros((2, COLBLK), F32)
    w_comb = jnp.zeros((2, COLBLK, 2 * COLBLK), F32)
    for d in range(2):
        wcol, acol = d * r, 2 * r + d * r
        mu_l = mu_l.at[d, wcol:wcol + r].set(mu_w[d]).at[d, acol:acol + r].set(mu_a[d])
        w_comb = w_comb.at[d, wcol:wcol + r, :COLBLK].set(w_up[d])
        w_comb = w_comb.at[d, acol:acol + r, COLBLK:].set(a_up[d])
    g_comb = jnp.zeros((COLBLK, COLBLK), F32).at[4 * r:4 * r + GATE_LORA].set(g_up)
    return mu_l, w_comb.astype(BF16), g_comb.astype(BF16)


def _head_block_ones():
    seg = np.arange(COLBLK) // HEAD_DIM
    return jnp.asarray(seg[:, None] == seg[None, :], dtype=BF16)


def kernel(x, c, positions, ada_w, ada_b, norm_gains, ffn_wi, ffn_wo, w_in, rpb, mu_rkv, mu_w, mu_a,
           w0, w_up, a0, a_up, g_up, k_k, k_a, r_k, gn_w, gn_b, w_branch, w_out, final_norm):
    depth = ada_w.shape[0]
    b, s, d = x.shape
    assert d == 2 * COLBLK and s % 1024 == 0 and s % (GRID_W * NA_WIN_ROWS) == 0

    mod = _modulation(c, ada_w, ada_b)
    cos, s1, s2 = _rope_tables(positions)
    ones_bd = _head_block_ones()
    ffn_wi_bf, ffn_wo_bf = ffn_wi.astype(BF16), ffn_wo.astype(BF16)
    w_branch_bf, w_out_bf = w_branch.astype(BF16), w_out.astype(BF16)
    row = lambda t: t.reshape(1, -1)

    for l in range(depth):
        m = mod[l].reshape(b, N_MOD, 1, d)
        sh1, sc1, gt1, sh2, sc2, gt2, sh3, sc3, gt3 = (m[:, i] for i in range(N_MOD))

        x = _ffn(x, norm_gains[l, 0], sh1, sc1, gt1, ffn_wi_bf[l, 0], ffn_wo_bf[l, 0])

        proj = _in_proj(x, norm_gains[l, 1], sh2, sc2, _pad_w_in(w_in[l]))
        o_a = _dilated_attention(proj, cos, s1, s2)
        o_b = _neighborhood_attention(proj, _natten_bias_table(rpb[l], s // GRID_W))
        mu_l, w_comb, g_comb = _lora_params(mu_w[l], mu_a[l], w_up[l], a_up[l], g_up[l])
        *scan_in, bonus = _rwkv_pre(proj, mu_rkv[l], mu_l, w_comb, w0[l], a0[l],
                                    row(k_k[l]), row(k_a[l]), row(r_k[l]), ones_bd)
        y = _rwkv_scan(*(_to_chain_lanes(t) for t in scan_in))
        y = _from_chain_lanes(y, b)
        x = _merge(y, bonus, proj, o_a, o_b, x, gt2, row(gn_w[l]), row(gn_b[l]), ones_bd, g_comb,
                   w_branch_bf[l], w_out_bf[l])

        x = _ffn(x, norm_gains[l, 2], sh3, sc3, gt3, ffn_wi_bf[l, 1], ffn_wo_bf[l, 1],
                 final_norm if l == depth - 1 else None)
    return x
```

```python
import functools

import numpy as np
import jax
import jax.numpy as jnp
from jax import lax
from jax.experimental import pallas as pl
from jax.experimental.pallas import tpu as pltpu

F32 = jnp.float32
BF16 = jnp.bfloat16

HEAD_DIM = 64
N_HEADS = 8
W_MIX = N_HEADS * HEAD_DIM
DILATED_PAIRS = ((128, 1), (512, 4), (2048, 16))
QBLK = 128
ROT_DIM = HEAD_DIM // 4
ROPE_THETA = 500000.0
GRID_W = 64
NA_WIN_ROWS = 8
NA_WIN_COLS = 16
DECAY_LORA = 64
ICLR_LORA = 64
GATE_LORA = 128
DECAY_SCALE = 0.6065306597126334
N_MOD = 9
RMS_EPS = 1e-6
GN_EPS = 64e-5
NEG = -1e30

LANES = 128
SUBLANES = 8
VMEM_LIMIT = 48 * 1024 * 1024
ATTN_GROUP = 4

COLBLK = 512
CB_AQ, CB_AK, CB_AV = 0, 1, 2
CB_NQ, CB_NK, CB_NV = 3, 4, 5
CB_PR, CB_PK, CB_PV = 6, 7, 8
CB_LORA = 9
CB_GZ = 10
D_IN_PAD = 16 * COLBLK


def _cparams(sem):
    return pltpu.CompilerParams(dimension_semantics=sem, vmem_limit_bytes=VMEM_LIMIT)


def _mod_kernel(c_ref, w_ref, b_ref, o_ref):
    c = c_ref[...]
    cs = (c * jax.nn.sigmoid(c)).astype(BF16)
    o_ref[0] = jnp.dot(cs, w_ref[0].astype(BF16), preferred_element_type=F32) + b_ref[0]


def _modulation(c, ada_w, ada_b):
    depth, d, nd = ada_w.shape
    b = c.shape[0]
    return pl.pallas_call(
        _mod_kernel,
        grid=(depth, nd // d),
        in_specs=[pl.BlockSpec((b, d), lambda l, j: (0, 0)),
                  pl.BlockSpec((1, d, d), lambda l, j: (l, 0, j)),
                  pl.BlockSpec((1, 1, d), lambda l, j: (l, 0, j))],
        out_specs=pl.BlockSpec((1, b, d), lambda l, j: (l, 0, j)),
        out_shape=jax.ShapeDtypeStruct((depth, b, nd), F32),
        compiler_params=_cparams(("parallel", "parallel")),
        name="adaln_mod",
    )(c, ada_w, ada_b.reshape(depth, 1, nd))


def _norm_mod(x, gain, shift, scale):
    ms = jnp.mean(x * x, axis=-1, keepdims=True)
    y = x * lax.rsqrt(ms + RMS_EPS) * gain
    return y * (1.0 + scale) + shift


def _ffn_kernel(*refs, final):
    if final:
        (x_ref, g_ref, sh_ref, sc_ref, gt_ref, wg_ref, wu_ref, wo_ref, fin_ref,
         o_ref, h_scr, acc_scr) = refs
    else:
        (x_ref, g_ref, sh_ref, sc_ref, gt_ref, wg_ref, wu_ref, wo_ref,
         o_ref, h_scr, acc_scr) = refs
    j = pl.program_id(2)

    @pl.when(j == 0)
    def _():
        h = _norm_mod(x_ref[0], g_ref[...], sh_ref[0], sc_ref[0])
        h_scr[...] = h.astype(BF16)
        acc_scr[...] = jnp.zeros_like(acc_scr)

    h = h_scr[...]
    gate = jnp.dot(h, wg_ref[...], preferred_element_type=F32)
    up = jnp.dot(h, wu_ref[...], preferred_element_type=F32)
    act = (gate * jax.nn.sigmoid(gate)) * up
    acc_scr[...] += jnp.dot(act.astype(BF16), wo_ref[...], preferred_element_type=F32)

    @pl.when(j == pl.num_programs(2) - 1)
    def _():
        y = x_ref[0] + 0.5 * gt_ref[0] * acc_scr[...]
        if final:
            ms = jnp.mean(y * y, axis=-1, keepdims=True)
            y = y * lax.rsqrt(ms + RMS_EPS) * fin_ref[...]
        o_ref[0] = y


def _ffn(x, gain, shift, scale, gate, wi, wo, final_gain=None, *, tm=1024, tf=256):
    b, s, d = x.shape
    dff = wo.shape[0]
    nj = dff // tf
    final = final_gain is not None
    vec = pl.BlockSpec((1, 1, d), lambda bi, i, j: (bi, 0, 0))
    in_specs = [pl.BlockSpec((1, tm, d), lambda bi, i, j: (bi, i, 0)),
                pl.BlockSpec((1, d), lambda bi, i, j: (0, 0)),
                vec, vec, vec,
                pl.BlockSpec((d, tf), lambda bi, i, j: (0, j)),
                pl.BlockSpec((d, tf), lambda bi, i, j: (0, j + nj)),
                pl.BlockSpec((tf, d), lambda bi, i, j: (j, 0))]
    args = [x, gain.reshape(1, d), shift, scale, gate, wi, wi, wo]
    if final:
        in_specs.append(pl.BlockSpec((1, d), lambda bi, i, j: (0, 0)))
        args.append(final_gain.reshape(1, d))
    return pl.pallas_call(
        functools.partial(_ffn_kernel, final=final),
        grid=(b, s // tm, nj),
        in_specs=in_specs,
        out_specs=pl.BlockSpec((1, tm, d), lambda bi, i, j: (bi, i, 0)),
        out_shape=jax.ShapeDtypeStruct((b, s, d), F32),
        scratch_shapes=[pltpu.VMEM((tm, d), BF16), pltpu.VMEM((tm, d), F32)],
        compiler_params=_cparams(("parallel", "parallel", "arbitrary")),
        name="ffn_final" if final else "ffn",
    )(*args)


def _win_kernel(x_ref, g_ref, sh_ref, sc_ref, w_ref, o_ref, h_scr):
    @pl.when(pl.program_id(2) == 0)
    def _():
        h = _norm_mod(x_ref[0], g_ref[...], sh_ref[0], sc_ref[0])
        h_scr[...] = h.astype(BF16)

    o_ref[0] = jnp.dot(h_scr[...], w_ref[...], preferred_element_type=F32)


def _in_proj(x, gain, shift, scale, w, *, tm=1024, tn=COLBLK):
    b, s, d = x.shape
    n = w.shape[1]
    vec = pl.BlockSpec((1, 1, d), lambda bi, i, j: (bi, 0, 0))
    return pl.pallas_call(
        _win_kernel,
        grid=(b, s // tm, n // tn),
        in_specs=[pl.BlockSpec((1, tm, d), lambda bi, i, j: (bi, i, 0)),
                  pl.BlockSpec((1, d), lambda bi, i, j: (0, 0)),
                  vec, vec,
                  pl.BlockSpec((d, tn), lambda bi, i, j: (0, j))],
        out_specs=pl.BlockSpec((1, tm, tn), lambda bi, i, j: (bi, i, j)),
        out_shape=jax.ShapeDtypeStruct((b, s, n), F32),
        scratch_shapes=[pltpu.VMEM((tm, d), BF16)],
        compiler_params=_cparams(("parallel", "parallel", "arbitrary")),
        name="in_proj",
    )(x, gain.reshape(1, d), shift, scale, w)


def _dilated_kernel(q_ref, k_ref, v_ref, cos_ref, s1_ref, s2_ref, o_ref,
                    q_scr, k_scr, v_scr, og_scr, lse_scr, *, seq, pad):
    cos, s1, s2 = cos_ref[0], s1_ref[0], s2_ref[0]

    def rotary(t):
        return (t * cos + pltpu.roll(t, LANES - ROT_DIM // 2, axis=1) * s1
                + pltpu.roll(t, ROT_DIM // 2, axis=1) * s2)

    q_scr[...] = rotary(q_ref[0])
    zeros = jnp.zeros((pad, LANES), F32)
    k_scr[pl.ds(0, pad), :] = zeros
    k_scr[pl.ds(pad + seq, pad), :] = zeros
    v_scr[pl.ds(0, pad), :] = zeros
    v_scr[pl.ds(pad + seq, pad), :] = zeros
    k_scr[pl.ds(pad, seq), :] = rotary(k_ref[0])
    v_scr[pl.ds(pad, seq), :] = v_ref[0]

    lane = lax.broadcasted_iota(jnp.int32, (QBLK, LANES), 1)
    head0 = lane < HEAD_DIM
    qi = lax.broadcasted_iota(jnp.int32, (QBLK, 2 * QBLK), 0)
    kj = lax.broadcasted_iota(jnp.int32, (QBLK, 2 * QBLK), 1)
    scale = HEAD_DIM ** -0.5

    for g, (window, dil) in enumerate(DILATED_PAIRS):
        radius = window // (2 * dil)
        cls_len = seq // dil
        nblk = cls_len // QBLK
        band = jnp.abs(kj - radius - qi) <= radius

        def group(gi, carry, g=g, dil=dil, radius=radius, cls_len=cls_len, nblk=nblk, band=band):
            units = []
            for u in range(ATTN_GROUP):
                idx = gi * ATTN_GROUP + u
                cls = idx // nblk
                n = idx % nblk
                q_start = cls + dil * QBLK * n
                k_start = pad + cls + dil * (QBLK * n - radius)
                if dil == 1:
                    q_rows = pl.ds(q_start, QBLK)
                    k_rows = pl.ds(k_start, 2 * QBLK)
                else:
                    q_rows = pl.ds(q_start, QBLK, stride=dil)
                    k_rows = pl.ds(k_start, 2 * QBLK, stride=dil)
                key_pos = QBLK * n + kj - radius
                valid = band & (key_pos >= 0) & (key_pos < cls_len)
                units.append((q_rows, q_scr[q_rows, :], k_scr[k_rows, :].astype(BF16),
                              v_scr[k_rows, :].astype(BF16), valid))
            scores = []
            for _, q, kw, _, valid in units:
                for hmask in (head0, jnp.logical_not(head0)):
                    qh = jnp.where(hmask, q, 0.0).astype(BF16)
                    s = lax.dot_general(qh, kw, (((1,), (1,)), ((), ())),
                                        preferred_element_type=F32) * scale
                    scores.append(jnp.where(valid, s, NEG))
            probs, lses = [], []
            for s in scores:
                m = jnp.max(s, axis=-1, keepdims=True)
                p = jnp.exp(s - m)
                den = jnp.sum(p, axis=-1, keepdims=True)
                probs.append((p / den).astype(BF16))
                lses.append(m + jnp.log(den))
            for u, (q_rows, _, _, vw, _) in enumerate(units):
                o0 = jnp.dot(probs[2 * u], vw, preferred_element_type=F32)
                o1 = jnp.dot(probs[2 * u + 1], vw, preferred_element_type=F32)
                og_scr[g, q_rows, :] = jnp.where(head0, o0, o1)
                lse_scr[g, q_rows, :] = jnp.where(head0, lses[2 * u], lses[2 * u + 1])
            return carry

        lax.fori_loop(0, dil * nblk // ATTN_GROUP, group, 0)

    l0, l1, l2 = lse_scr[0], lse_scr[1], lse_scr[2]
    m = jnp.maximum(jnp.maximum(l0, l1), l2)
    e0, e1, e2 = jnp.exp(l0 - m), jnp.exp(l1 - m), jnp.exp(l2 - m)
    tot = e0 + e1 + e2
    o_ref[0] = (e0 / tot) * og_scr[0] + (e1 / tot) * og_scr[1] + (e2 / tot) * og_scr[2]


def _dilated_attention(proj, cos, s1, s2):
    b, s, _ = proj.shape
    pad = max((w // (2 * d)) * d for w, d in DILATED_PAIRS)
    hp = W_MIX // LANES
    per_cb = COLBLK // LANES
    col = lambda cb: pl.BlockSpec((1, s, LANES), lambda bi, h, cb=cb: (bi, 0, cb * per_cb + h))
    tab = pl.BlockSpec((1, s, LANES), lambda bi, h: (bi, 0, 0))
    return pl.pallas_call(
        functools.partial(_dilated_kernel, seq=s, pad=pad),
        grid=(b, hp),
        in_specs=[col(CB_AQ), col(CB_AK), col(CB_AV), tab, tab, tab],
        out_specs=pl.BlockSpec((1, s, LANES), lambda bi, h: (bi, 0, h)),
        out_shape=jax.ShapeDtypeStruct((b, s, W_MIX), F32),
        scratch_shapes=[pltpu.VMEM((s, LANES), F32),
                        pltpu.VMEM((s + 2 * pad, LANES), F32),
                        pltpu.VMEM((s + 2 * pad, LANES), F32),
                        pltpu.VMEM((len(DILATED_PAIRS), s, LANES), F32),
                        pltpu.VMEM((len(DILATED_PAIRS), s, LANES), F32)],
        compiler_params=_cparams(("parallel", "parallel")),
        name="dilated_attn",
    )(proj, proj, proj, cos, s1, s2)


def _rope_tables(positions):
    half = ROT_DIM // 2
    inv_freq = ROPE_THETA ** (-jnp.arange(half, dtype=F32) * 2.0 / ROT_DIM)
    ang = positions.astype(F32)[..., None] * inv_freq
    cos, sin = jnp.cos(ang), jnp.sin(ang)
    lead = ang.shape[:-1]
    ones = jnp.ones(lead + (HEAD_DIM - ROT_DIM,), F32)
    z = lambda n: jnp.zeros(lead + (n,), F32)
    c64 = jnp.concatenate([cos, cos, ones], axis=-1)
    s1 = jnp.concatenate([-sin, z(HEAD_DIM - half)], axis=-1)
    s2 = jnp.concatenate([z(half), sin, z(HEAD_DIM - ROT_DIM)], axis=-1)
    rep = lambda t: jnp.concatenate([t] * (LANES // HEAD_DIM), axis=-1)
    return rep(c64), rep(s1), rep(s2)


def _natten_kernel(q_ref, k_ref, v_ref, bias_ref, o_ref, *, rows, win_rows):
    lane = lax.broadcasted_iota(jnp.int32, (GRID_W, LANES), 1)
    head0 = lane < HEAD_DIM
    scale = HEAD_DIM ** -0.5
    nkeys = win_rows * GRID_W

    def group(gi, carry):
        units = []
        for u in range(ATTN_GROUP):
            r = gi * ATTN_GROUP + u
            r_start = jnp.clip(r - win_rows // 2, 0, rows - win_rows)
            q_rows = pl.ds(pl.multiple_of(r * GRID_W, GRID_W), GRID_W)
            k_rows = pl.ds(pl.multiple_of(r_start * GRID_W, GRID_W), nkeys)
            units.append((q_rows, r - r_start, q_ref[0, q_rows, :],
                          k_ref[0, k_rows, :].astype(BF16), v_ref[0, k_rows, :].astype(BF16)))
        scores = []
        for _, delta, q, kw, _ in units:
            for hh, hmask in enumerate((head0, jnp.logical_not(head0))):
                qh = jnp.where(hmask, q, 0.0).astype(BF16)
                s = lax.dot_general(qh, kw, (((1,), (1,)), ((), ())),
                                    preferred_element_type=F32) * scale
                scores.append(s + bias_ref[hh, delta])
        probs = []
        for s in scores:
            m = jnp.max(s, axis=-1, keepdims=True)
            p = jnp.exp(s - m)
            den = jnp.sum(p, axis=-1, keepdims=True)
            probs.append((p / den).astype(BF16))
        for u, (q_rows, _, _, _, vw) in enumerate(units):
            o0 = jnp.dot(probs[2 * u], vw, preferred_element_type=F32)
            o1 = jnp.dot(probs[2 * u + 1], vw, preferred_element_type=F32)
            o_ref[0, q_rows, :] = jnp.where(head0, o0, o1)
        return carry

    lax.fori_loop(0, rows // ATTN_GROUP, group, 0)


def _natten_bias_tables(rpb, rows):
    wr = min(NA_WIN_ROWS, rows)
    wc = NA_WIN_COLS
    cols = np.arange(GRID_W)
    c_start = np.clip(cols - wc // 2, 0, GRID_W - wc)
    col_in = (cols[None, :] >= c_start[:, None]) & (cols[None, :] < c_start[:, None] + wc)
    coff = np.clip(cols[None, :] - cols[:, None], -(wc - 1), wc - 1) + wc - 1
    onehot = (coff[None] == np.arange(2 * wc - 1)[:, None, None]).astype(np.float32)
    by_col = jnp.einsum("lhrc,cqk->lhrqk", rpb.astype(F32), onehot, precision=lax.Precision.HIGHEST)
    by_col = jnp.where(col_in, by_col, NEG)
    top = NA_WIN_ROWS - 1
    tabs = jnp.stack([by_col[:, :, top - dl:top - dl + wr] for dl in range(wr)], axis=2)
    tabs = tabs.transpose(0, 1, 2, 4, 3, 5)
    return tabs.reshape(rpb.shape[0], rpb.shape[1], wr, GRID_W, wr * GRID_W)


def _neighborhood_attention(proj, bias_tab):
    b, s, _ = proj.shape
    rows = s // GRID_W
    wr = min(NA_WIN_ROWS, rows)
    hp = W_MIX // LANES
    per_cb = COLBLK // LANES
    col = lambda cb: pl.BlockSpec((1, s, LANES), lambda bi, h, cb=cb: (bi, 0, cb * per_cb + h))
    return pl.pallas_call(
        functools.partial(_natten_kernel, rows=rows, win_rows=wr),
        grid=(b, hp),
        in_specs=[col(CB_NQ), col(CB_NK), col(CB_NV),
                  pl.BlockSpec((LANES // HEAD_DIM, wr, GRID_W, wr * GRID_W),
                               lambda bi, h: (h, 0, 0, 0))],
        out_specs=pl.BlockSpec((1, s, LANES), lambda bi, h: (bi, 0, h)),
        out_shape=jax.ShapeDtypeStruct((b, s, W_MIX), F32),
        compiler_params=_cparams(("parallel", "parallel")),
        name="natten",
    )(proj, proj, proj, bias_tab)


def _seg_sum(x, ones_bd):
    hi = x.astype(BF16)
    lo = (x - hi.astype(F32)).astype(BF16)
    return (jnp.dot(hi, ones_bd, preferred_element_type=F32)
            + jnp.dot(lo, ones_bd, preferred_element_type=F32))


def _rwkv_pre_kernel(r_ref, k_ref, v_ref, l_ref,
                     rp_ref, kp_ref, vp_ref, lp_ref,
                     rn_ref, kn_ref, vn_ref, ln_ref,
                     mu_ref, mul_ref, wc_ref, w0_ref, a0_ref, kk_ref, ka_ref, rk_ref, ones_ref,
                     r_o, w_o, k_o, v_o, a_o, b_o, bonus_o, *, tm):
    i = pl.program_id(1)
    first = i == 0
    last = i == pl.num_programs(1) - 1
    row = lax.broadcasted_iota(jnp.int32, (tm, COLBLK), 0)
    lane = lax.broadcasted_iota(jnp.int32, (tm, COLBLK), 1)
    tanh_cols = lane < 2 * DECAY_LORA
    ones_bd = ones_ref[...]

    def neighbours(x_ref, p_ref, n_ref):
        x = x_ref[0]
        prev_row = jnp.where(first, 0.0, p_ref[0, SUBLANES - 1:SUBLANES, :])
        next_row = jnp.where(last, 0.0, n_ref[0, 0:1, :])
        prev = jnp.where(row == 0, prev_row, pltpu.roll(x, 1, axis=0))
        nxt = jnp.where(row == tm - 1, next_row, pltpu.roll(x, tm - 1, axis=0))
        return x, (prev, nxt)

    pr, nb_r = neighbours(r_ref, rp_ref, rn_ref)
    pk, nb_k = neighbours(k_ref, kp_ref, kn_ref)
    pv, nb_v = neighbours(v_ref, vp_ref, vn_ref)
    pl_, nb_l = neighbours(l_ref, lp_ref, ln_ref)

    bonus = jnp.zeros((tm, COLBLK), F32)
    for d in range(2):
        r = pr + (nb_r[d] - pr) * mu_ref[d, 0:1, :]
        k = pk + (nb_k[d] - pk) * mu_ref[d, 1:2, :]
        v = pv + (nb_v[d] - pv) * mu_ref[d, 2:3, :]
        xl = pl_ + (nb_l[d] - pl_) * mul_ref[d:d + 1, :]
        tz = jnp.where(tanh_cols, jnp.tanh(xl), xl).astype(BF16)
        z = jnp.dot(tz, wc_ref[d], preferred_element_type=F32)
        wz = w0_ref[d:d + 1, :] + z[:, :COLBLK]
        az = a0_ref[d:d + 1, :] + z[:, COLBLK:]
        decay = jnp.exp(-DECAY_SCALE * jax.nn.sigmoid(wz))
        a = jax.nn.sigmoid(az)
        kk = k * kk_ref[...]
        norm = jnp.sqrt(_seg_sum(kk * kk, ones_bd))
        kk = kk / jnp.maximum(norm, 1e-12)
        k2 = k * (1.0 + (a - 1.0) * ka_ref[...])
        bonus = bonus + _seg_sum(r * k2 * rk_ref[...], ones_bd) * v
        r_o[d, 0] = r
        w_o[d, 0] = decay
        k_o[d, 0] = k2
        v_o[d, 0] = v
        a_o[d, 0] = -kk
        b_o[d, 0] = kk * a
    bonus_o[0] = bonus


def _rwkv_pre(proj, mu_rkv, mu_lora, w_comb, w0, a0, k_k, k_a, r_k, ones_bd, *, tm=256):
    b, s, _ = proj.shape
    nsub = tm // SUBLANES
    last_sub = s // SUBLANES - 1
    main = lambda cb: pl.BlockSpec((1, tm, COLBLK), lambda bi, i, cb=cb: (bi, i, cb))
    prev = lambda cb: pl.BlockSpec(
        (1, SUBLANES, COLBLK), lambda bi, i, cb=cb: (bi, jnp.maximum(i * nsub - 1, 0), cb))
    nxt = lambda cb: pl.BlockSpec(
        (1, SUBLANES, COLBLK), lambda bi, i, cb=cb: (bi, jnp.minimum((i + 1) * nsub, last_sub), cb))
    cbs = (CB_PR, CB_PK, CB_PV, CB_LORA)
    const = lambda shape: pl.BlockSpec(shape, lambda bi, i: (0,) * len(shape))
    dir_out = pl.BlockSpec((2, 1, tm, COLBLK), lambda bi, i: (0, bi, i, 0))
    dir_shape = jax.ShapeDtypeStruct((2, b, s, COLBLK), F32)
    return pl.pallas_call(
        functools.partial(_rwkv_pre_kernel, tm=tm),
        grid=(b, s // tm),
        in_specs=([main(cb) for cb in cbs] + [prev(cb) for cb in cbs] + [nxt(cb) for cb in cbs]
                  + [const((2, 3, COLBLK)), const((2, COLBLK)), const((2, COLBLK, 2 * COLBLK)),
                     const((2, COLBLK)), const((2, COLBLK)), const((1, COLBLK)), const((1, COLBLK)),
                     const((1, COLBLK)), const((COLBLK, COLBLK))]),
        out_specs=[dir_out] * 6 + [pl.BlockSpec((1, tm, COLBLK), lambda bi, i: (bi, i, 0))],
        out_shape=[dir_shape] * 6 + [jax.ShapeDtypeStruct((b, s, COLBLK), F32)],
        compiler_params=_cparams(("parallel", "parallel")),
        name="rwkv_pre",
    )(*([proj] * 12), mu_rkv, mu_lora, w_comb, w0, a0, k_k, k_a, r_k, ones_bd)


def _rwkv_scan_kernel(r_ref, w_ref, k_ref, v_ref, a_ref, b_ref, an_ref, y_ref,
                      st_scr, sa_scr, rows_scr, *, tt):
    d = pl.program_id(0)
    n = HEAD_DIM

    @pl.when(pl.program_id(1) == 0)
    def _():
        st_scr[...] = jnp.zeros_like(st_scr)
        sa_scr[...] = jnp.zeros_like(sa_scr)

    def row(slot, q, k):
        return jnp.broadcast_to(rows_scr[slot, q, k // SUBLANES, pl.ds(k % SUBLANES, 1), :],
                                sa_scr.shape[1:])

    def step(j, t, a_next, sa, gam):
        slot = j % 2
        gam = gam * w_ref[0, t]
        inv = 1.0 / gam
        v = v_ref[0, t]
        rows_scr[slot, 0] = b_ref[0, t] * inv
        rows_scr[slot, 1] = k_ref[0, t] * inv
        rows_scr[slot, 2] = r_ref[0, t] * gam
        rows_scr[slot, 3] = a_next * gam
        y = jnp.zeros_like(v)
        sa_next = [jnp.zeros_like(v), jnp.zeros_like(v)]
        for k in range(n):
            new = st_scr[k] + (sa * row(slot, 0, k) + v * row(slot, 1, k))
            st_scr[k] = new
            y = y + new * row(slot, 2, k)
            sa_next[k % 2] = sa_next[k % 2] + new * row(slot, 3, k)
        y_ref[0, t] = y
        return sa_next[0] + sa_next[1], gam

    def body(j, carry):
        t = j + d * (tt - 1 - 2 * j)
        return step(j, t, a_ref[0, t + 1 - 2 * d], *carry)

    carry = (sa_scr[...], jnp.ones(sa_scr.shape, F32))
    carry = lax.fori_loop(0, tt - 1, body, carry)
    sa, gam = step(tt - 1, (1 - d) * (tt - 1), an_ref[0, 0], *carry)
    sa_scr[...] = sa
    rows_scr[0, 0] = gam
    for k in range(n):
        st_scr[k] = st_scr[k] * row(0, 0, k)


def _rwkv_scan(r, w, k, v, a, b, *, tt=16):
    _, s, n, c = r.shape
    nt = s // tt
    tiled = (n // SUBLANES, SUBLANES, c)
    spec = pl.BlockSpec((1, tt) + tiled, lambda d, i: (d, i + d * (nt - 1 - 2 * i), 0, 0, 0))

    def next_first_step(d, i):
        fwd = jnp.minimum((i + 1) * tt, s - 1)
        bwd = jnp.maximum((nt - 1 - i) * tt - 1, 0)
        return (d, jnp.where(d == 0, fwd, bwd), 0, 0, 0)

    r, w, k, v, a, b = (t.reshape((2, s) + tiled) for t in (r, w, k, v, a, b))
    y = pl.pallas_call(
        functools.partial(_rwkv_scan_kernel, tt=tt),
        grid=(2, nt),
        in_specs=[spec] * 6 + [pl.BlockSpec((1, 1) + tiled, next_first_step)],
        out_specs=spec,
        out_shape=jax.ShapeDtypeStruct((2, s) + tiled, F32),
        scratch_shapes=[pltpu.VMEM((n,) + tiled, F32), pltpu.VMEM(tiled, F32),
                        pltpu.VMEM((2, 4) + tiled, F32)],
        compiler_params=_cparams(("parallel", "arbitrary")),
        name="rwkv_scan",
    )(r, w, k, v, a, b, a)
    return y.reshape(2, s, n, c)


def _to_chain_lanes(t):
    _, b, s, _ = t.shape
    t = t.reshape(2, b, s, N_HEADS, HEAD_DIM).transpose(0, 2, 4, 1, 3)
    return t.reshape(2, s, HEAD_DIM, b * N_HEADS)


def _from_chain_lanes(t, b):
    _, s, _, _ = t.shape
    t = t.reshape(2, s, HEAD_DIM, b, N_HEADS).transpose(0, 3, 1, 4, 2)
    return t.reshape(2, b, s, W_MIX)


def _merge_kernel(y_ref, bonus_ref, l_ref, gz0_ref, gz1_ref, gz2_ref, oa_ref, ob_ref, x_ref, gt_ref,
                  gnw_ref, gnb_ref, ones_ref, gc_ref, wb_ref, wo_ref, o_ref):
    ones_bd = ones_ref[...]
    inv_n = 1.0 / HEAD_DIM
    acc = bonus_ref[0]
    for d in range(2):
        y = y_ref[d, 0]
        mean = _seg_sum(y, ones_bd) * inv_n
        yc = y - mean
        var = _seg_sum(yc * yc, ones_bd) * inv_n
        acc = acc + (yc * lax.rsqrt(var + GN_EPS)) * gnw_ref[...] + gnb_ref[...]
    g = jnp.dot(jax.nn.sigmoid(l_ref[0]).astype(BF16), gc_ref[...], preferred_element_type=F32)
    o_c = acc * g
    merged = (jax.nn.sigmoid(gz0_ref[0])
              * jnp.dot(oa_ref[0].astype(BF16), wb_ref[0], preferred_element_type=F32)
              + jax.nn.sigmoid(gz1_ref[0])
              * jnp.dot(ob_ref[0].astype(BF16), wb_ref[1], preferred_element_type=F32)
              + jax.nn.sigmoid(gz2_ref[0])
              * jnp.dot(o_c.astype(BF16), wb_ref[2], preferred_element_type=F32))
    out = jnp.dot(merged.astype(BF16), wo_ref[...], preferred_element_type=F32)
    o_ref[0] = x_ref[0] + gt_ref[0] * out


def _merge(y, bonus, proj, o_a, o_b, x, gate, gn_w, gn_b, ones_bd, g_comb, w_branch, w_out, *, tm=512):
    b, s, d = x.shape
    gz_per = d // COLBLK
    tok = lambda width, cb: pl.BlockSpec((1, tm, width), lambda bi, i, cb=cb: (bi, i, cb))
    const = lambda shape: pl.BlockSpec(shape, lambda bi, i: (0,) * len(shape))
    return pl.pallas_call(
        _merge_kernel,
        grid=(b, s // tm),
        in_specs=[pl.BlockSpec((2, 1, tm, COLBLK), lambda bi, i: (0, bi, i, 0)),
                  tok(COLBLK, 0),
                  tok(COLBLK, CB_LORA),
                  tok(d, CB_GZ // gz_per), tok(d, CB_GZ // gz_per + 1), tok(d, CB_GZ // gz_per + 2),
                  tok(COLBLK, 0), tok(COLBLK, 0), tok(d, 0),
                  pl.BlockSpec((1, 1, d), lambda bi, i: (bi, 0, 0)),
                  const((1, COLBLK)), const((1, COLBLK)), const((COLBLK, COLBLK)),
                  const((COLBLK, COLBLK)), const((3, COLBLK, d)), const((d, d))],
        out_specs=pl.BlockSpec((1, tm, d), lambda bi, i: (bi, i, 0)),
        out_shape=jax.ShapeDtypeStruct((b, s, d), F32),
        compiler_params=_cparams(("parallel", "parallel")),
        name="merge_out",
    )(y, bonus, proj, proj, proj, proj, o_a, o_b, x, gate, gn_w, gn_b, ones_bd, g_comb, w_branch, w_out)


def _pad_w_in(w_in):
    d = w_in.shape[0]
    n_front = CB_LORA * COLBLK + 4 * DECAY_LORA + GATE_LORA
    front, gates = w_in[:, :n_front], w_in[:, n_front:]
    zpad = jnp.zeros((d, CB_GZ * COLBLK - n_front), w_in.dtype)
    return jnp.concatenate([front, zpad, gates], axis=1).astype(BF16)


def _lora_params(mu_w, mu_a, w_up, a_up, g_up):
    r = DECAY_LORA
    mu_l = jnp.zeros((2, COLBLK), F32)
    w_comb = jnp.zeros((2, COLBLK, 2 * COLBLK), F32)
    for d in range(2):
        wcol, acol = d * r, 2 * r + d * r
        mu_l = mu_l.at[d, wcol:wcol + r].set(mu_w[d]).at[d, acol:acol + r].set(mu_a[d])
        w_comb = w_comb.at[d, wcol:wcol + r, :COLBLK].set(w_up[d])
        w_comb = w_comb.at[d, acol:acol + r, COLBLK:].set(a_up[d])
    g_comb = jnp.zeros((COLBLK, COLBLK), F32).at[4 * r:4 * r + GATE_LORA].set(g_up)
    return mu_l, w_comb.astype(BF16), g_comb.astype(BF16)


def _head_block_ones():
    seg = np.arange(COLBLK) // HEAD_DIM
    return jnp.asarray(seg[:, None] == seg[None, :], dtype=BF16)


def kernel(x, c, positions, ada_w, ada_b, norm_gains, ffn_wi, ffn_wo, w_in, rpb, mu_rkv, mu_w, mu_a,
           w0, w_up, a0, a_up, g_up, k_k, k_a, r_k, gn_w, gn_b, w_branch, w_out, final_norm):
    depth = ada_w.shape[0]
    b, s, d = x.shape
    assert d == 2 * COLBLK and s % 1024 == 0 and s % (GRID_W * NA_WIN_ROWS) == 0

    mod = _modulation(c, ada_w, ada_b)
    cos, s1, s2 = _rope_tables(positions)
    ones_bd = _head_block_ones()
    bias_tabs = _natten_bias_tables(rpb, s // GRID_W)
    ffn_wi_bf, ffn_wo_bf = ffn_wi.astype(BF16), ffn_wo.astype(BF16)
    w_branch_bf, w_out_bf = w_branch.astype(BF16), w_out.astype(BF16)
    row = lambda t: t.reshape(1, -1)

    for l in range(depth):
        m = mod[l].reshape(b, N_MOD, 1, d)
        sh1, sc1, gt1, sh2, sc2, gt2, sh3, sc3, gt3 = (m[:, i] for i in range(N_MOD))

        x = _ffn(x, norm_gains[l, 0], sh1, sc1, gt1, ffn_wi_bf[l, 0], ffn_wo_bf[l, 0])

        proj = _in_proj(x, norm_gains[l, 1], sh2, sc2, _pad_w_in(w_in[l]))
        o_a = _dilated_attention(proj, cos, s1, s2)
        o_b = _neighborhood_attention(proj, bias_tabs[l])
        mu_l, w_comb, g_comb = _lora_params(mu_w[l], mu_a[l], w_up[l], a_up[l], g_up[l])
        *scan_in, bonus = _rwkv_pre(proj, mu_rkv[l], mu_l, w_comb, w0[l], a0[l],
                                    row(k_k[l]), row(k_a[l]), row(r_k[l]), ones_bd)
        y = _rwkv_scan(*(_to_chain_lanes(t) for t in scan_in))
        y = _from_chain_lanes(y, b)
        x = _merge(y, bonus, proj, o_a, o_b, x, gt2, row(gn_w[l]), row(gn_b[l]), ones_bd, g_comb,
                   w_branch_bf[l], w_out_bf[l])

        x = _ffn(x, norm_gains[l, 2], sh3, sc3, gt3, ffn_wi_bf[l, 1], ffn_wo_bf[l, 1],
                 final_norm if l == depth - 1 else None)
    return x
```

```python
import functools

import numpy as np
import jax
import jax.numpy as jnp
from jax import lax
from jax.experimental import pallas as pl
from jax.experimental.pallas import tpu as pltpu

F32 = jnp.float32
BF16 = jnp.bfloat16

HEAD_DIM = 64
N_HEADS = 8
W_MIX = N_HEADS * HEAD_DIM
DILATED_PAIRS = ((128, 1), (512, 4), (2048, 16))
QBLK = 128
ROT_DIM = HEAD_DIM // 4
ROPE_THETA = 500000.0
GRID_W = 64
NA_WIN_ROWS = 8
NA_WIN_COLS = 16
DECAY_LORA = 64
ICLR_LORA = 64
GATE_LORA = 128
DECAY_SCALE = 0.6065306597126334
N_MOD = 9
RMS_EPS = 1e-6
GN_EPS = 64e-5
NEG = -1e30

LANES = 128
SUBLANES = 8
VMEM_LIMIT = 48 * 1024 * 1024
ATTN_GROUP = 4

COLBLK = 512
CB_AQ, CB_AK, CB_AV = 0, 1, 2
CB_NQ, CB_NK, CB_NV = 3, 4, 5
N_ATTN = 6 * COLBLK
CB_PR, CB_PK, CB_PV = 0, 1, 2
CB_LORA = 3
CB_GZ = 4


def _cparams(sem):
    return pltpu.CompilerParams(dimension_semantics=sem, vmem_limit_bytes=VMEM_LIMIT)


def _mod_kernel(c_ref, w_ref, b_ref, o_ref):
    c = c_ref[...]
    cs = (c * jax.nn.sigmoid(c)).astype(BF16)
    o_ref[0] = jnp.dot(cs, w_ref[0].astype(BF16), preferred_element_type=F32) + b_ref[0]


def _modulation(c, ada_w, ada_b):
    depth, d, nd = ada_w.shape
    b = c.shape[0]
    return pl.pallas_call(
        _mod_kernel,
        grid=(depth, nd // d),
        in_specs=[pl.BlockSpec((b, d), lambda l, j: (0, 0)),
                  pl.BlockSpec((1, d, d), lambda l, j: (l, 0, j)),
                  pl.BlockSpec((1, 1, d), lambda l, j: (l, 0, j))],
        out_specs=pl.BlockSpec((1, b, d), lambda l, j: (l, 0, j)),
        out_shape=jax.ShapeDtypeStruct((depth, b, nd), F32),
        compiler_params=_cparams(("parallel", "parallel")),
        name="adaln_mod",
    )(c, ada_w, ada_b.reshape(depth, 1, nd))


def _norm_mod(x, gain, shift, scale):
    ms = jnp.mean(x * x, axis=-1, keepdims=True)
    y = x * lax.rsqrt(ms + RMS_EPS) * gain
    return y * (1.0 + scale) + shift


def _ffn_kernel(*refs, final, tf):
    if final:
        x_ref, g_ref, sh_ref, sc_ref, gt_ref, wi_ref, wo_ref, fin_ref, o_ref = refs
    else:
        x_ref, g_ref, sh_ref, sc_ref, gt_ref, wi_ref, wo_ref, o_ref = refs
    x = x_ref[0]
    h = _norm_mod(x, g_ref[...], sh_ref[0], sc_ref[0]).astype(BF16)
    dff = wo_ref.shape[0]
    acc = None
    for lo in range(0, dff, tf):
        gate = jnp.dot(h, wi_ref[:, lo:lo + tf], preferred_element_type=F32)
        up = jnp.dot(h, wi_ref[:, dff + lo:dff + lo + tf], preferred_element_type=F32)
        act = ((gate * jax.nn.sigmoid(gate)) * up).astype(BF16)
        part = jnp.dot(act, wo_ref[lo:lo + tf, :], preferred_element_type=F32)
        acc = part if acc is None else acc + part
    y = x + 0.5 * gt_ref[0] * acc
    if final:
        ms = jnp.mean(y * y, axis=-1, keepdims=True)
        y = y * lax.rsqrt(ms + RMS_EPS) * fin_ref[...]
    o_ref[0] = y


def _ffn(x, gain, shift, scale, gate, wi, wo, final_gain=None, *, tm=512, tf=256):
    b, s, d = x.shape
    dff = wo.shape[0]
    final = final_gain is not None
    vec = pl.BlockSpec((1, 1, d), lambda bi, i: (bi, 0, 0))
    resident = lambda shape: pl.BlockSpec(shape, lambda bi, i: (0, 0), pipeline_mode=pl.Buffered(1))
    in_specs = [pl.BlockSpec((1, tm, d), lambda bi, i: (bi, i, 0)),
                pl.BlockSpec((1, d), lambda bi, i: (0, 0)),
                vec, vec, vec,
                resident((d, 2 * dff)), resident((dff, d))]
    args = [x, gain.reshape(1, d), shift, scale, gate, wi, wo]
    if final:
        in_specs.append(pl.BlockSpec((1, d), lambda bi, i: (0, 0)))
        args.append(final_gain.reshape(1, d))
    return pl.pallas_call(
        functools.partial(_ffn_kernel, final=final, tf=tf),
        grid=(b, s // tm),
        in_specs=in_specs,
        out_specs=pl.BlockSpec((1, tm, d), lambda bi, i: (bi, i, 0)),
        out_shape=jax.ShapeDtypeStruct((b, s, d), F32),
        compiler_params=_cparams(("parallel", "parallel")),
        name="ffn_final" if final else "ffn",
    )(*args)


def _win_kernel(x_ref, g_ref, sh_ref, sc_ref, wa_ref, wr_ref, oa_ref, or_ref, h_scr):
    @pl.when(pl.program_id(2) == 0)
    def _():
        h = _norm_mod(x_ref[0], g_ref[...], sh_ref[0], sc_ref[0])
        h_scr[...] = h.astype(BF16)

    h = h_scr[...]
    oa_ref[0] = jnp.dot(h, wa_ref[...], preferred_element_type=F32).astype(BF16)
    or_ref[0] = jnp.dot(h, wr_ref[...], preferred_element_type=F32)


def _in_proj(x, gain, shift, scale, w_attn, w_rest, *, tm=1024, nj=4):
    b, s, d = x.shape
    ta, tr = w_attn.shape[1] // nj, w_rest.shape[1] // nj
    vec = pl.BlockSpec((1, 1, d), lambda bi, i, j: (bi, 0, 0))
    return pl.pallas_call(
        _win_kernel,
        grid=(b, s // tm, nj),
        in_specs=[pl.BlockSpec((1, tm, d), lambda bi, i, j: (bi, i, 0)),
                  pl.BlockSpec((1, d), lambda bi, i, j: (0, 0)),
                  vec, vec,
                  pl.BlockSpec((d, ta), lambda bi, i, j: (0, j)),
                  pl.BlockSpec((d, tr), lambda bi, i, j: (0, j))],
        out_specs=[pl.BlockSpec((1, tm, ta), lambda bi, i, j: (bi, i, j)),
                   pl.BlockSpec((1, tm, tr), lambda bi, i, j: (bi, i, j))],
        out_shape=[jax.ShapeDtypeStruct((b, s, w_attn.shape[1]), BF16),
                   jax.ShapeDtypeStruct((b, s, w_rest.shape[1]), F32)],
        scratch_shapes=[pltpu.VMEM((tm, d), BF16)],
        compiler_params=_cparams(("parallel", "parallel", "arbitrary")),
        name="in_proj",
    )(x, gain.reshape(1, d), shift, scale, w_attn, w_rest)


def _dilated_kernel(q_ref, k_ref, v_ref, cos_ref, s1_ref, s2_ref, o_ref,
                    q_scr, k_scr, v_scr, og_scr, lse_scr, *, seq, pad):
    cos, s1, s2 = cos_ref[0], s1_ref[0], s2_ref[0]

    def rotary(t):
        return (t * cos + pltpu.roll(t, LANES - ROT_DIM // 2, axis=1) * s1
                + pltpu.roll(t, ROT_DIM // 2, axis=1) * s2)

    q_scr[...] = rotary(q_ref[0].astype(F32))
    zeros = jnp.zeros((pad, LANES), F32)
    k_scr[pl.ds(0, pad), :] = zeros
    k_scr[pl.ds(pad + seq, pad), :] = zeros
    v_scr[pl.ds(0, pad), :] = zeros
    v_scr[pl.ds(pad + seq, pad), :] = zeros
    k_scr[pl.ds(pad, seq), :] = rotary(k_ref[0].astype(F32))
    v_scr[pl.ds(pad, seq), :] = v_ref[0].astype(F32)

    lane = lax.broadcasted_iota(jnp.int32, (QBLK, LANES), 1)
    head0 = lane < HEAD_DIM
    qi = lax.broadcasted_iota(jnp.int32, (QBLK, 2 * QBLK), 0)
    kj = lax.broadcasted_iota(jnp.int32, (QBLK, 2 * QBLK), 1)
    scale = HEAD_DIM ** -0.5

    for g, (window, dil) in enumerate(DILATED_PAIRS):
        radius = window // (2 * dil)
        cls_len = seq // dil
        nblk = cls_len // QBLK
        band = jnp.abs(kj - radius - qi) <= radius

        def group(gi, carry, g=g, dil=dil, radius=radius, cls_len=cls_len, nblk=nblk, band=band):
            units = []
            for u in range(ATTN_GROUP):
                idx = gi * ATTN_GROUP + u
                cls = idx // nblk
                n = idx % nblk
                q_start = cls + dil * QBLK * n
                k_start = pad + cls + dil * (QBLK * n - radius)
                if dil == 1:
                    q_rows = pl.ds(q_start, QBLK)
                    k_rows = pl.ds(k_start, 2 * QBLK)
                else:
                    q_rows = pl.ds(q_start, QBLK, stride=dil)
                    k_rows = pl.ds(k_start, 2 * QBLK, stride=dil)
                key_pos = QBLK * n + kj - radius
                valid = band & (key_pos >= 0) & (key_pos < cls_len)
                units.append((q_rows, q_scr[q_rows, :], k_scr[k_rows, :].astype(BF16),
                              v_scr[k_rows, :].astype(BF16), valid))
            scores = []
            for _, q, kw, _, valid in units:
                for hmask in (head0, jnp.logical_not(head0)):
                    qh = jnp.where(hmask, q, 0.0).astype(BF16)
                    s = lax.dot_general(qh, kw, (((1,), (1,)), ((), ())),
                                        preferred_element_type=F32) * scale
                    scores.append(jnp.where(valid, s, NEG))
            probs, invs, lses = [], [], []
            for s in scores:
                m = jnp.max(s, axis=-1, keepdims=True)
                p = jnp.exp(s - m)
                den = jnp.sum(p, axis=-1, keepdims=True)
                probs.append(p.astype(BF16))
                invs.append(1.0 / den)
                lses.append(m + jnp.log(den))
            for u, (q_rows, _, _, vw, _) in enumerate(units):
                o0 = jnp.dot(probs[2 * u], vw, preferred_element_type=F32) * invs[2 * u]
                o1 = jnp.dot(probs[2 * u + 1], vw, preferred_element_type=F32) * invs[2 * u + 1]
                og_scr[g, q_rows, :] = jnp.where(head0, o0, o1)
                lse_scr[g, q_rows, :] = jnp.where(head0, lses[2 * u], lses[2 * u + 1])
            return carry

        lax.fori_loop(0, dil * nblk // ATTN_GROUP, group, 0)

    l0, l1, l2 = lse_scr[0], lse_scr[1], lse_scr[2]
    m = jnp.maximum(jnp.maximum(l0, l1), l2)
    e0, e1, e2 = jnp.exp(l0 - m), jnp.exp(l1 - m), jnp.exp(l2 - m)
    tot = e0 + e1 + e2
    o_ref[0] = (e0 / tot) * og_scr[0] + (e1 / tot) * og_scr[1] + (e2 / tot) * og_scr[2]


def _dilated_attention(attn, cos, s1, s2):
    b, s, _ = attn.shape
    pad = max((w // (2 * d)) * d for w, d in DILATED_PAIRS)
    hp = W_MIX // LANES
    per_cb = COLBLK // LANES
    col = lambda cb: pl.BlockSpec((1, s, LANES), lambda bi, h, cb=cb: (bi, 0, cb * per_cb + h))
    tab = pl.BlockSpec((1, s, LANES), lambda bi, h: (bi, 0, 0))
    return pl.pallas_call(
        functools.partial(_dilated_kernel, seq=s, pad=pad),
        grid=(b, hp),
        in_specs=[col(CB_AQ), col(CB_AK), col(CB_AV), tab, tab, tab],
        out_specs=pl.BlockSpec((1, s, LANES), lambda bi, h: (bi, 0, h)),
        out_shape=jax.ShapeDtypeStruct((b, s, W_MIX), F32),
        scratch_shapes=[pltpu.VMEM((s, LANES), F32),
                        pltpu.VMEM((s + 2 * pad, LANES), F32),
                        pltpu.VMEM((s + 2 * pad, LANES), F32),
                        pltpu.VMEM((len(DILATED_PAIRS), s, LANES), F32),
                        pltpu.VMEM((len(DILATED_PAIRS), s, LANES), F32)],
        compiler_params=_cparams(("parallel", "parallel")),
        name="dilated_attn",
    )(attn, attn, attn, cos, s1, s2)


def _rope_tables(positions):
    half = ROT_DIM // 2
    inv_freq = ROPE_THETA ** (-jnp.arange(half, dtype=F32) * 2.0 / ROT_DIM)
    ang = positions.astype(F32)[..., None] * inv_freq
    cos, sin = jnp.cos(ang), jnp.sin(ang)
    lead = ang.shape[:-1]
    ones = jnp.ones(lead + (HEAD_DIM - ROT_DIM,), F32)
    z = lambda n: jnp.zeros(lead + (n,), F32)
    c64 = jnp.concatenate([cos, cos, ones], axis=-1)
    s1 = jnp.concatenate([-sin, z(HEAD_DIM - half)], axis=-1)
    s2 = jnp.concatenate([z(half), sin, z(HEAD_DIM - ROT_DIM)], axis=-1)
    rep = lambda t: jnp.concatenate([t] * (LANES // HEAD_DIM), axis=-1)
    return rep(c64), rep(s1), rep(s2)


def _natten_kernel(q_ref, k_ref, v_ref, bias_ref, o_ref, *, rows, win_rows):
    lane = lax.broadcasted_iota(jnp.int32, (GRID_W, LANES), 1)
    head0 = lane < HEAD_DIM
    scale = HEAD_DIM ** -0.5
    nkeys = win_rows * GRID_W

    def group(gi, carry):
        units = []
        for u in range(ATTN_GROUP):
            r = gi * ATTN_GROUP + u
            r_start = jnp.clip(r - win_rows // 2, 0, rows - win_rows)
            q_rows = pl.ds(pl.multiple_of(r * GRID_W, GRID_W), GRID_W)
            k_rows = pl.ds(pl.multiple_of(r_start * GRID_W, GRID_W), nkeys)
            units.append((q_rows, r - r_start, q_ref[0, q_rows, :],
                          k_ref[0, k_rows, :], v_ref[0, k_rows, :]))
        scores = []
        for _, delta, q, kw, _ in units:
            for hh, hmask in enumerate((head0, jnp.logical_not(head0))):
                qh = jnp.where(hmask, q, jnp.zeros_like(q))
                s = lax.dot_general(qh, kw, (((1,), (1,)), ((), ())),
                                    preferred_element_type=F32) * scale
                scores.append(s + bias_ref[hh, delta])
        probs, invs = [], []
        for s in scores:
            m = jnp.max(s, axis=-1, keepdims=True)
            p = jnp.exp(s - m)
            probs.append(p.astype(BF16))
            invs.append(1.0 / jnp.sum(p, axis=-1, keepdims=True))
        for u, (q_rows, _, _, _, vw) in enumerate(units):
            o0 = jnp.dot(probs[2 * u], vw, preferred_element_type=F32) * invs[2 * u]
            o1 = jnp.dot(probs[2 * u + 1], vw, preferred_element_type=F32) * invs[2 * u + 1]
            o_ref[0, q_rows, :] = jnp.where(head0, o0, o1)
        return carry

    lax.fori_loop(0, rows // ATTN_GROUP, group, 0)


def _natten_bias_tables(rpb, rows):
    wr = min(NA_WIN_ROWS, rows)
    wc = NA_WIN_COLS
    cols = np.arange(GRID_W)
    c_start = np.clip(cols - wc // 2, 0, GRID_W - wc)
    col_in = (cols[None, :] >= c_start[:, None]) & (cols[None, :] < c_start[:, None] + wc)
    coff = np.clip(cols[None, :] - cols[:, None], -(wc - 1), wc - 1) + wc - 1
    onehot = (coff[None] == np.arange(2 * wc - 1)[:, None, None]).astype(np.float32)
    by_col = jnp.einsum("lhrc,cqk->lhrqk", rpb.astype(F32), onehot, precision=lax.Precision.HIGHEST)
    by_col = jnp.where(col_in, by_col, NEG)
    top = NA_WIN_ROWS - 1
    tabs = jnp.stack([by_col[:, :, top - dl:top - dl + wr] for dl in range(wr)], axis=2)
    tabs = tabs.transpose(0, 1, 2, 4, 3, 5)
    return tabs.reshape(rpb.shape[0], rpb.shape[1], wr, GRID_W, wr * GRID_W)


def _neighborhood_attention(attn, bias_tab):
    b, s, _ = attn.shape
    rows = s // GRID_W
    wr = min(NA_WIN_ROWS, rows)
    hp = W_MIX // LANES
    per_cb = COLBLK // LANES
    col = lambda cb: pl.BlockSpec((1, s, LANES), lambda bi, h, cb=cb: (bi, 0, cb * per_cb + h))
    return pl.pallas_call(
        functools.partial(_natten_kernel, rows=rows, win_rows=wr),
        grid=(b, hp),
        in_specs=[col(CB_NQ), col(CB_NK), col(CB_NV),
                  pl.BlockSpec((LANES // HEAD_DIM, wr, GRID_W, wr * GRID_W),
                               lambda bi, h: (h, 0, 0, 0))],
        out_specs=pl.BlockSpec((1, s, LANES), lambda bi, h: (bi, 0, h)),
        out_shape=jax.ShapeDtypeStruct((b, s, W_MIX), F32),
        compiler_params=_cparams(("parallel", "parallel")),
        name="natten",
    )(attn, attn, attn, bias_tab)


def _seg_sum(x, ones_bd):
    hi = x.astype(BF16)
    lo = (x - hi.astype(F32)).astype(BF16)
    return (jnp.dot(hi, ones_bd, preferred_element_type=F32)
            + jnp.dot(lo, ones_bd, preferred_element_type=F32))


def _rwkv_pre_kernel(r_ref, k_ref, v_ref, l_ref,
                     rp_ref, kp_ref, vp_ref, lp_ref,
                     rn_ref, kn_ref, vn_ref, ln_ref,
                     mu_ref, mul_ref, wc_ref, w0_ref, a0_ref, kk_ref, ka_ref, rk_ref, ones_ref,
                     r_o, w_o, k_o, v_o, a_o, b_o, bonus_o, *, tm):
    i = pl.program_id(1)
    first = i == 0
    last = i == pl.num_programs(1) - 1
    row = lax.broadcasted_iota(jnp.int32, (tm, COLBLK), 0)
    ones_bd = ones_ref[...]

    def neighbours(x_ref, p_ref, n_ref):
        x = x_ref[0]
        prev_row = jnp.where(first, 0.0, p_ref[0, SUBLANES - 1:SUBLANES, :])
        next_row = jnp.where(last, 0.0, n_ref[0, 0:1, :])
        prev = jnp.where(row == 0, prev_row, pltpu.roll(x, 1, axis=0))
        nxt = jnp.where(row == tm - 1, next_row, pltpu.roll(x, tm - 1, axis=0))
        return x, (prev, nxt)

    pr, nb_r = neighbours(r_ref, rp_ref, rn_ref)
    pk, nb_k = neighbours(k_ref, kp_ref, kn_ref)
    pv, nb_v = neighbours(v_ref, vp_ref, vn_ref)
    pl_, nb_l = neighbours(l_ref, lp_ref, ln_ref)

    bonus = jnp.zeros((tm, COLBLK), F32)
    for d in range(2):
        r = pr + (nb_r[d] - pr) * mu_ref[d, 0:1, :]
        k = pk + (nb_k[d] - pk) * mu_ref[d, 1:2, :]
        v = pv + (nb_v[d] - pv) * mu_ref[d, 2:3, :]
        xl = pl_ + (nb_l[d] - pl_) * mul_ref[d:d + 1, :]
        zw = jnp.dot(jnp.tanh(xl[:, :LANES]).astype(BF16), wc_ref[d, :LANES, :],
                     preferred_element_type=F32)
        za = jnp.dot(xl[:, LANES:2 * LANES].astype(BF16), wc_ref[d, LANES:2 * LANES, :],
                     preferred_element_type=F32)
        z = zw + za
        wz = w0_ref[d:d + 1, :] + z[:, :COLBLK]
        az = a0_ref[d:d + 1, :] + z[:, COLBLK:]
        decay = jnp.exp(-DECAY_SCALE * jax.nn.sigmoid(wz))
        a = jax.nn.sigmoid(az)
        kk = k * kk_ref[...]
        norm = jnp.sqrt(_seg_sum(kk * kk, ones_bd))
        kk = kk / jnp.maximum(norm, 1e-12)
        k2 = k * (1.0 + (a - 1.0) * ka_ref[...])
        bonus = bonus + _seg_sum(r * k2 * rk_ref[...], ones_bd) * v
        r_o[d, 0] = r
        w_o[d, 0] = decay
        k_o[d, 0] = k2
        v_o[d, 0] = v
        a_o[d, 0] = -kk
        b_o[d, 0] = kk * a
    bonus_o[0] = bonus


def _rwkv_pre(proj, mu_rkv, mu_lora, w_comb, w0, a0, k_k, k_a, r_k, ones_bd, *, tm=256):
    b, s, _ = proj.shape
    nsub = tm // SUBLANES
    last_sub = s // SUBLANES - 1
    main = lambda cb: pl.BlockSpec((1, tm, COLBLK), lambda bi, i, cb=cb: (bi, i, cb))
    prev = lambda cb: pl.BlockSpec(
        (1, SUBLANES, COLBLK), lambda bi, i, cb=cb: (bi, jnp.maximum(i * nsub - 1, 0), cb))
    nxt = lambda cb: pl.BlockSpec(
        (1, SUBLANES, COLBLK), lambda bi, i, cb=cb: (bi, jnp.minimum((i + 1) * nsub, last_sub), cb))
    cbs = (CB_PR, CB_PK, CB_PV, CB_LORA)
    const = lambda shape: pl.BlockSpec(shape, lambda bi, i: (0,) * len(shape))
    dir_out = pl.BlockSpec((2, 1, tm, COLBLK), lambda bi, i: (0, bi, i, 0))
    dir_shape = jax.ShapeDtypeStruct((2, b, s, COLBLK), F32)
    return pl.pallas_call(
        functools.partial(_rwkv_pre_kernel, tm=tm),
        grid=(b, s // tm),
        in_specs=([main(cb) for cb in cbs] + [prev(cb) for cb in cbs] + [nxt(cb) for cb in cbs]
                  + [const((2, 3, COLBLK)), const((2, COLBLK)), const((2, 2 * LANES, 2 * COLBLK)),
                     const((2, COLBLK)), const((2, COLBLK)), const((1, COLBLK)), const((1, COLBLK)),
                     const((1, COLBLK)), const((COLBLK, COLBLK))]),
        out_specs=[dir_out] * 6 + [pl.BlockSpec((1, tm, COLBLK), lambda bi, i: (bi, i, 0))],
        out_shape=[dir_shape] * 6 + [jax.ShapeDtypeStruct((b, s, COLBLK), F32)],
        compiler_params=_cparams(("parallel", "parallel")),
        name="rwkv_pre",
    )(*([proj] * 12), mu_rkv, mu_lora, w_comb, w0, a0, k_k, k_a, r_k, ones_bd)


def _rwkv_scan_kernel(r_ref, w_ref, k_ref, v_ref, a_ref, b_ref, an_ref, y_ref,
                      st_scr, sa_scr, rows_scr, *, tt):
    d = pl.program_id(0)
    n = HEAD_DIM

    @pl.when(pl.program_id(1) == 0)
    def _():
        st_scr[...] = jnp.zeros_like(st_scr)
        sa_scr[...] = jnp.zeros_like(sa_scr)

    def row(slot, q, k):
        return jnp.broadcast_to(rows_scr[slot, q, k // SUBLANES, pl.ds(k % SUBLANES, 1), :],
                                sa_scr.shape[1:])

    def step(j, t, a_next, sa, gam):
        slot = j % 2
        gam = gam * w_ref[0, t]
        inv = 1.0 / gam
        v = v_ref[0, t]
        rows_scr[slot, 0] = b_ref[0, t] * inv
        rows_scr[slot, 1] = k_ref[0, t] * inv
        rows_scr[slot, 2] = r_ref[0, t] * gam
        rows_scr[slot, 3] = a_next * gam
        y = jnp.zeros_like(v)
        sa_next = [jnp.zeros_like(v), jnp.zeros_like(v)]
        for k in range(n):
            new = st_scr[k] + (sa * row(slot, 0, k) + v * row(slot, 1, k))
            st_scr[k] = new
            y = y + new * row(slot, 2, k)
            sa_next[k % 2] = sa_next[k % 2] + new * row(slot, 3, k)
        y_ref[0, t] = y
        return sa_next[0] + sa_next[1], gam

    def body(j, carry):
        t = j + d * (tt - 1 - 2 * j)
        return step(j, t, a_ref[0, t + 1 - 2 * d], *carry)

    carry = (sa_scr[...], jnp.ones(sa_scr.shape, F32))
    carry = lax.fori_loop(0, tt - 1, body, carry)
    sa, gam = step(tt - 1, (1 - d) * (tt - 1), an_ref[0, 0], *carry)
    sa_scr[...] = sa
    rows_scr[0, 0] = gam
    for k in range(n):
        st_scr[k] = st_scr[k] * row(0, 0, k)


def _rwkv_scan(r, w, k, v, a, b, *, tt=16):
    _, s, n, c = r.shape
    nt = s // tt
    tiled = (n // SUBLANES, SUBLANES, c)
    spec = pl.BlockSpec((1, tt) + tiled, lambda d, i: (d, i + d * (nt - 1 - 2 * i), 0, 0, 0))

    def next_first_step(d, i):
        fwd = jnp.minimum((i + 1) * tt, s - 1)
        bwd = jnp.maximum((nt - 1 - i) * tt - 1, 0)
        return (d, jnp.where(d == 0, fwd, bwd), 0, 0, 0)

    r, w, k, v, a, b = (t.reshape((2, s) + tiled) for t in (r, w, k, v, a, b))
    y = pl.pallas_call(
        functools.partial(_rwkv_scan_kernel, tt=tt),
        grid=(2, nt),
        in_specs=[spec] * 6 + [pl.BlockSpec((1, 1) + tiled, next_first_step)],
        out_specs=spec,
        out_shape=jax.ShapeDtypeStruct((2, s) + tiled, F32),
        scratch_shapes=[pltpu.VMEM((n,) + tiled, F32), pltpu.VMEM(tiled, F32),
                        pltpu.VMEM((2, 4) + tiled, F32)],
        compiler_params=_cparams(("parallel", "arbitrary")),
        name="rwkv_scan",
    )(r, w, k, v, a, b, a)
    return y.reshape(2, s, n, c)


def _to_chain_lanes(t):
    _, b, s, _ = t.shape
    t = t.reshape(2, b, s, N_HEADS, HEAD_DIM).transpose(0, 2, 4, 1, 3)
    return t.reshape(2, s, HEAD_DIM, b * N_HEADS)


def _from_chain_lanes(t, b):
    _, s, _, _ = t.shape
    t = t.reshape(2, s, HEAD_DIM, b, N_HEADS).transpose(0, 3, 1, 4, 2)
    return t.reshape(2, b, s, W_MIX)


def _merge_kernel(y_ref, bonus_ref, l_ref, gz0_ref, gz1_ref, gz2_ref, oa_ref, ob_ref, x_ref, gt_ref,
                  gnw_ref, gnb_ref, ones_ref, gc_ref, wb_ref, wo_ref, o_ref):
    ones_bd = ones_ref[...]
    inv_n = 1.0 / HEAD_DIM
    acc = bonus_ref[0]
    for d in range(2):
        y = y_ref[d, 0]
        mean = _seg_sum(y, ones_bd) * inv_n
        yc = y - mean
        var = _seg_sum(yc * yc, ones_bd) * inv_n
        acc = acc + (yc * lax.rsqrt(var + GN_EPS)) * gnw_ref[...] + gnb_ref[...]
    g = jnp.dot(jax.nn.sigmoid(l_ref[0]).astype(BF16), gc_ref[...], preferred_element_type=F32)
    o_c = acc * g
    merged = (jax.nn.sigmoid(gz0_ref[0])
              * jnp.dot(oa_ref[0].astype(BF16), wb_ref[0], preferred_element_type=F32)
              + jax.nn.sigmoid(gz1_ref[0])
              * jnp.dot(ob_ref[0].astype(BF16), wb_ref[1], preferred_element_type=F32)
              + jax.nn.sigmoid(gz2_ref[0])
              * jnp.dot(o_c.astype(BF16), wb_ref[2], preferred_element_type=F32))
    out = jnp.dot(merged.astype(BF16), wo_ref[...], preferred_element_type=F32)
    o_ref[0] = x_ref[0] + gt_ref[0] * out


def _merge(y, bonus, proj, o_a, o_b, x, gate, gn_w, gn_b, ones_bd, g_comb, w_branch, w_out, *, tm=512):
    b, s, d = x.shape
    gz_per = d // COLBLK
    tok = lambda width, cb: pl.BlockSpec((1, tm, width), lambda bi, i, cb=cb: (bi, i, cb))
    const = lambda shape: pl.BlockSpec(shape, lambda bi, i: (0,) * len(shape))
    return pl.pallas_call(
        _merge_kernel,
        grid=(b, s // tm),
        in_specs=[pl.BlockSpec((2, 1, tm, COLBLK), lambda bi, i: (0, bi, i, 0)),
                  tok(COLBLK, 0),
                  tok(COLBLK, CB_LORA),
                  tok(d, CB_GZ // gz_per), tok(d, CB_GZ // gz_per + 1), tok(d, CB_GZ // gz_per + 2),
                  tok(COLBLK, 0), tok(COLBLK, 0), tok(d, 0),
                  pl.BlockSpec((1, 1, d), lambda bi, i: (bi, 0, 0)),
                  const((1, COLBLK)), const((1, COLBLK)), const((COLBLK, COLBLK)),
                  const((COLBLK, COLBLK)), const((3, COLBLK, d)), const((d, d))],
        out_specs=pl.BlockSpec((1, tm, d), lambda bi, i: (bi, i, 0)),
        out_shape=jax.ShapeDtypeStruct((b, s, d), F32),
        compiler_params=_cparams(("parallel", "parallel")),
        name="merge_out",
    )(y, bonus, proj, proj, proj, proj, o_a, o_b, x, gate, gn_w, gn_b, ones_bd, g_comb, w_branch, w_out)


def _split_w_in(w_in):
    d = w_in.shape[0]
    n_lora = 4 * DECAY_LORA + GATE_LORA
    n_rest = N_ATTN + CB_LORA * COLBLK + n_lora
    rest, gates = w_in[:, N_ATTN:n_rest], w_in[:, n_rest:]
    zpad = jnp.zeros((d, COLBLK - n_lora), w_in.dtype)
    return w_in[:, :N_ATTN].astype(BF16), jnp.concatenate([rest, zpad, gates], axis=1).astype(BF16)


def _lora_params(mu_w, mu_a, w_up, a_up, g_up):
    r = DECAY_LORA
    mu_l = jnp.zeros((2, COLBLK), F32)
    w_comb = jnp.zeros((2, 2 * LANES, 2 * COLBLK), F32)
    for d in range(2):
        wcol, acol = d * r, 2 * r + d * r
        mu_l = mu_l.at[d, wcol:wcol + r].set(mu_w[d]).at[d, acol:acol + r].set(mu_a[d])
        w_comb = w_comb.at[d, wcol:wcol + r, :COLBLK].set(w_up[d])
        w_comb = w_comb.at[d, acol:acol + r, COLBLK:].set(a_up[d])
    g_comb = jnp.zeros((COLBLK, COLBLK), F32).at[4 * r:4 * r + GATE_LORA].set(g_up)
    return mu_l, w_comb.astype(BF16), g_comb.astype(BF16)


def _head_block_ones():
    seg = np.arange(COLBLK) // HEAD_DIM
    return jnp.asarray(seg[:, None] == seg[None, :], dtype=BF16)


def kernel(x, c, positions, ada_w, ada_b, norm_gains, ffn_wi, ffn_wo, w_in, rpb, mu_rkv, mu_w, mu_a,
           w0, w_up, a0, a_up, g_up, k_k, k_a, r_k, gn_w, gn_b, w_branch, w_out, final_norm):
    depth = ada_w.shape[0]
    b, s, d = x.shape
    assert d == 2 * COLBLK and s % 1024 == 0 and s % (GRID_W * NA_WIN_ROWS) == 0

    mod = _modulation(c, ada_w, ada_b)
    cos, s1, s2 = _rope_tables(positions)
    ones_bd = _head_block_ones()
    bias_tabs = _natten_bias_tables(rpb, s // GRID_W)
    ffn_wi_bf, ffn_wo_bf = ffn_wi.astype(BF16), ffn_wo.astype(BF16)
    w_branch_bf, w_out_bf = w_branch.astype(BF16), w_out.astype(BF16)
    row = lambda t: t.reshape(1, -1)

    for l in range(depth):
        m = mod[l].reshape(b, N_MOD, 1, d)
        sh1, sc1, gt1, sh2, sc2, gt2, sh3, sc3, gt3 = (m[:, i] for i in range(N_MOD))

        x = _ffn(x, norm_gains[l, 0], sh1, sc1, gt1, ffn_wi_bf[l, 0], ffn_wo_bf[l, 0])

        attn, proj = _in_proj(x, norm_gains[l, 1], sh2, sc2, *_split_w_in(w_in[l]))
        o_a = _dilated_attention(attn, cos, s1, s2)
        o_b = _neighborhood_attention(attn, bias_tabs[l])
        mu_l, w_comb, g_comb = _lora_params(mu_w[l], mu_a[l], w_up[l], a_up[l], g_up[l])
        *scan_in, bonus = _rwkv_pre(proj, mu_rkv[l], mu_l, w_comb, w0[l], a0[l],
                                    row(k_k[l]), row(k_a[l]), row(r_k[l]), ones_bd)
        y = _rwkv_scan(*(_to_chain_lanes(t) for t in scan_in))
        y = _from_chain_lanes(y, b)
        x = _merge(y, bonus, proj, o_a, o_b, x, gt2, row(gn_w[l]), row(gn_b[l]), ones_bd, g_comb,
                   w_branch_bf[l], w_out_bf[l])

        x = _ffn(x, norm_gains[l, 2], sh3, sc3, gt3, ffn_wi_bf[l, 1], ffn_wo_bf[l, 1],
                 final_norm if l == depth - 1 else None)
    return x
```

```python
import functools

import numpy as np
import jax
import jax.numpy as jnp
from jax import lax
from jax.experimental import pallas as pl
from jax.experimental.pallas import tpu as pltpu

F32 = jnp.float32
BF16 = jnp.bfloat16

HEAD_DIM = 64
N_HEADS = 8
W_MIX = N_HEADS * HEAD_DIM
DILATED_PAIRS = ((128, 1), (512, 4), (2048, 16))
QBLK = 128
ROT_DIM = HEAD_DIM // 4
ROPE_THETA = 500000.0
GRID_W = 64
NA_WIN_ROWS = 8
NA_WIN_COLS = 16
DECAY_LORA = 64
ICLR_LORA = 64
GATE_LORA = 128
DECAY_SCALE = 0.6065306597126334
N_MOD = 9
RMS_EPS = 1e-6
GN_EPS = 64e-5
NEG = -1e30

LANES = 128
SUBLANES = 8
VMEM_LIMIT = 48 * 1024 * 1024
ATTN_GROUP = 4

COLBLK = 512
CB_AQ, CB_AK, CB_AV = 0, 1, 2
CB_NQ, CB_NK, CB_NV = 3, 4, 5
N_ATTN = 6 * COLBLK
CB_PR, CB_PK, CB_PV = 0, 1, 2
CB_LORA = 3
CB_GZ = 4


def _cparams(sem):
    return pltpu.CompilerParams(dimension_semantics=sem, vmem_limit_bytes=VMEM_LIMIT)


def _sigmoid(x):
    return 0.5 * jnp.tanh(0.5 * x) + 0.5


def _mod_kernel(c_ref, w_ref, b_ref, o_ref):
    c = c_ref[...]
    cs = (c * _sigmoid(c)).astype(BF16)
    o_ref[0] = jnp.dot(cs, w_ref[0].astype(BF16), preferred_element_type=F32) + b_ref[0]


def _modulation(c, ada_w, ada_b):
    depth, d, nd = ada_w.shape
    b = c.shape[0]
    return pl.pallas_call(
        _mod_kernel,
        grid=(depth, nd // d),
        in_specs=[pl.BlockSpec((b, d), lambda l, j: (0, 0)),
                  pl.BlockSpec((1, d, d), lambda l, j: (l, 0, j)),
                  pl.BlockSpec((1, 1, d), lambda l, j: (l, 0, j))],
        out_specs=pl.BlockSpec((1, b, d), lambda l, j: (l, 0, j)),
        out_shape=jax.ShapeDtypeStruct((depth, b, nd), F32),
        compiler_params=_cparams(("parallel", "parallel")),
        name="adaln_mod",
    )(c, ada_w, ada_b.reshape(depth, 1, nd))


def _norm_mod(x, gain, shift, scale):
    ms = jnp.mean(x * x, axis=-1, keepdims=True)
    y = x * lax.rsqrt(ms + RMS_EPS) * gain
    return y * (1.0 + scale) + shift


def _ffn_kernel(*refs, final, tf):
    if final:
        x_ref, g_ref, sh_ref, sc_ref, gt_ref, wi_ref, wo_ref, fin_ref, o_ref = refs
    else:
        x_ref, g_ref, sh_ref, sc_ref, gt_ref, wi_ref, wo_ref, o_ref = refs
    x = x_ref[0]
    h = _norm_mod(x, g_ref[...], sh_ref[0], sc_ref[0]).astype(BF16)
    dff = wo_ref.shape[0]
    acc = None
    for lo in range(0, dff, tf):
        gate = jnp.dot(h, wi_ref[:, lo:lo + tf], preferred_element_type=F32)
        up = jnp.dot(h, wi_ref[:, dff + lo:dff + lo + tf], preferred_element_type=F32)
        act = ((gate * _sigmoid(gate)) * up).astype(BF16)
        part = jnp.dot(act, wo_ref[lo:lo + tf, :], preferred_element_type=F32)
        acc = part if acc is None else acc + part
    y = x + 0.5 * gt_ref[0] * acc
    if final:
        ms = jnp.mean(y * y, axis=-1, keepdims=True)
        y = y * lax.rsqrt(ms + RMS_EPS) * fin_ref[...]
    o_ref[0] = y


def _ffn(x, gain, shift, scale, gate, wi, wo, final_gain=None, *, tm=512, tf=256):
    b, s, d = x.shape
    dff = wo.shape[0]
    final = final_gain is not None
    vec = pl.BlockSpec((1, 1, d), lambda bi, i: (bi, 0, 0))
    resident = lambda shape: pl.BlockSpec(shape, lambda bi, i: (0, 0), pipeline_mode=pl.Buffered(1))
    in_specs = [pl.BlockSpec((1, tm, d), lambda bi, i: (bi, i, 0)),
                pl.BlockSpec((1, d), lambda bi, i: (0, 0)),
                vec, vec, vec,
                resident((d, 2 * dff)), resident((dff, d))]
    args = [x, gain.reshape(1, d), shift, scale, gate, wi, wo]
    if final:
        in_specs.append(pl.BlockSpec((1, d), lambda bi, i: (0, 0)))
        args.append(final_gain.reshape(1, d))
    return pl.pallas_call(
        functools.partial(_ffn_kernel, final=final, tf=tf),
        grid=(b, s // tm),
        in_specs=in_specs,
        out_specs=pl.BlockSpec((1, tm, d), lambda bi, i: (bi, i, 0)),
        out_shape=jax.ShapeDtypeStruct((b, s, d), F32),
        compiler_params=_cparams(("parallel", "parallel")),
        name="ffn_final" if final else "ffn",
    )(*args)


def _win_kernel(x_ref, g_ref, sh_ref, sc_ref, wa_ref, wr_ref, oa_ref, or_ref, h_scr):
    @pl.when(pl.program_id(2) == 0)
    def _():
        h = _norm_mod(x_ref[0], g_ref[...], sh_ref[0], sc_ref[0])
        h_scr[...] = h.astype(BF16)

    h = h_scr[...]
    oa_ref[0] = jnp.dot(h, wa_ref[...], preferred_element_type=F32).astype(BF16)
    or_ref[0] = jnp.dot(h, wr_ref[...], preferred_element_type=F32)


def _in_proj(x, gain, shift, scale, w_attn, w_rest, *, tm=1024, nj=4):
    b, s, d = x.shape
    ta, tr = w_attn.shape[1] // nj, w_rest.shape[1] // nj
    vec = pl.BlockSpec((1, 1, d), lambda bi, i, j: (bi, 0, 0))
    return pl.pallas_call(
        _win_kernel,
        grid=(b, s // tm, nj),
        in_specs=[pl.BlockSpec((1, tm, d), lambda bi, i, j: (bi, i, 0)),
                  pl.BlockSpec((1, d), lambda bi, i, j: (0, 0)),
                  vec, vec,
                  pl.BlockSpec((d, ta), lambda bi, i, j: (0, j)),
                  pl.BlockSpec((d, tr), lambda bi, i, j: (0, j))],
        out_specs=[pl.BlockSpec((1, tm, ta), lambda bi, i, j: (bi, i, j)),
                   pl.BlockSpec((1, tm, tr), lambda bi, i, j: (bi, i, j))],
        out_shape=[jax.ShapeDtypeStruct((b, s, w_attn.shape[1]), BF16),
                   jax.ShapeDtypeStruct((b, s, w_rest.shape[1]), F32)],
        scratch_shapes=[pltpu.VMEM((tm, d), BF16)],
        compiler_params=_cparams(("parallel", "parallel", "arbitrary")),
        name="in_proj",
    )(x, gain.reshape(1, d), shift, scale, w_attn, w_rest)


def _dilated_kernel(q_ref, k_ref, v_ref, cos_ref, s1_ref, s2_ref, o_ref,
                    q_scr, k_scr, v_scr, og_scr, lse_scr, *, seq, pad):
    cos, s1, s2 = cos_ref[0], s1_ref[0], s2_ref[0]

    def rotary(t):
        return (t * cos + pltpu.roll(t, LANES - ROT_DIM // 2, axis=1) * s1
                + pltpu.roll(t, ROT_DIM // 2, axis=1) * s2)

    q_scr[...] = rotary(q_ref[0].astype(F32))
    zeros = jnp.zeros((pad, LANES), F32)
    k_scr[pl.ds(0, pad), :] = zeros
    k_scr[pl.ds(pad + seq, pad), :] = zeros
    v_scr[pl.ds(0, pad), :] = zeros
    v_scr[pl.ds(pad + seq, pad), :] = zeros
    k_scr[pl.ds(pad, seq), :] = rotary(k_ref[0].astype(F32))
    v_scr[pl.ds(pad, seq), :] = v_ref[0].astype(F32)

    lane = lax.broadcasted_iota(jnp.int32, (QBLK, LANES), 1)
    head0 = lane < HEAD_DIM
    scale = HEAD_DIM ** -0.5

    for g, (window, dil) in enumerate(DILATED_PAIRS):
        radius = window // (2 * dil)
        cls_len = seq // dil
        nblk = cls_len // QBLK
        lead = radius if nblk > 1 else 0
        nkeys = QBLK + 2 * lead
        qi = lax.broadcasted_iota(jnp.int32, (QBLK, nkeys), 0)
        kj = lax.broadcasted_iota(jnp.int32, (QBLK, nkeys), 1)
        band = jnp.abs(kj - lead - qi) <= radius

        def group(gi, carry, g=g, dil=dil, lead=lead, nkeys=nkeys, cls_len=cls_len, nblk=nblk,
                  band=band, kj=kj):
            units = []
            for u in range(ATTN_GROUP):
                idx = gi * ATTN_GROUP + u
                cls = idx // nblk
                n = idx % nblk
                q_start = cls + dil * QBLK * n
                k_start = pad + cls + dil * (QBLK * n - lead)
                if dil == 1:
                    q_rows = pl.ds(q_start, QBLK)
                    k_rows = pl.ds(k_start, nkeys)
                else:
                    q_rows = pl.ds(q_start, QBLK, stride=dil)
                    k_rows = pl.ds(k_start, nkeys, stride=dil)
                key_pos = QBLK * n + kj - lead
                valid = band & (key_pos >= 0) & (key_pos < cls_len)
                units.append((q_rows, q_scr[q_rows, :], k_scr[k_rows, :].astype(BF16),
                              v_scr[k_rows, :].astype(BF16), valid))
            scores = []
            for _, q, kw, _, valid in units:
                for hmask in (head0, jnp.logical_not(head0)):
                    qh = jnp.where(hmask, q, 0.0).astype(BF16)
                    s = lax.dot_general(qh, kw, (((1,), (1,)), ((), ())),
                                        preferred_element_type=F32) * scale
                    scores.append(jnp.where(valid, s, NEG))
            probs, invs, lses = [], [], []
            for s in scores:
                m = jnp.max(s, axis=-1, keepdims=True)
                p = jnp.exp(s - m)
                den = jnp.sum(p, axis=-1, keepdims=True)
                probs.append(p.astype(BF16))
                invs.append(1.0 / den)
                lses.append(m + jnp.log(den))
            for u, (q_rows, _, _, vw, _) in enumerate(units):
                o0 = jnp.dot(probs[2 * u], vw, preferred_element_type=F32) * invs[2 * u]
                o1 = jnp.dot(probs[2 * u + 1], vw, preferred_element_type=F32) * invs[2 * u + 1]
                og_scr[g, q_rows, :] = jnp.where(head0, o0, o1)
                lse_scr[g, q_rows, :] = jnp.where(head0, lses[2 * u], lses[2 * u + 1])
            return carry

        lax.fori_loop(0, dil * nblk // ATTN_GROUP, group, 0)

    l0, l1, l2 = lse_scr[0], lse_scr[1], lse_scr[2]
    m = jnp.maximum(jnp.maximum(l0, l1), l2)
    e0, e1, e2 = jnp.exp(l0 - m), jnp.exp(l1 - m), jnp.exp(l2 - m)
    tot = e0 + e1 + e2
    o_ref[0] = (e0 / tot) * og_scr[0] + (e1 / tot) * og_scr[1] + (e2 / tot) * og_scr[2]


def _dilated_attention(attn, cos, s1, s2):
    b, s, _ = attn.shape
    pad = max((w // (2 * d)) * d for w, d in DILATED_PAIRS)
    hp = W_MIX // LANES
    per_cb = COLBLK // LANES
    col = lambda cb: pl.BlockSpec((1, s, LANES), lambda bi, h, cb=cb: (bi, 0, cb * per_cb + h))
    tab = pl.BlockSpec((1, s, LANES), lambda bi, h: (bi, 0, 0))
    return pl.pallas_call(
        functools.partial(_dilated_kernel, seq=s, pad=pad),
        grid=(b, hp),
        in_specs=[col(CB_AQ), col(CB_AK), col(CB_AV), tab, tab, tab],
        out_specs=pl.BlockSpec((1, s, LANES), lambda bi, h: (bi, 0, h)),
        out_shape=jax.ShapeDtypeStruct((b, s, W_MIX), F32),
        scratch_shapes=[pltpu.VMEM((s, LANES), F32),
                        pltpu.VMEM((s + 2 * pad, LANES), F32),
                        pltpu.VMEM((s + 2 * pad, LANES), F32),
                        pltpu.VMEM((len(DILATED_PAIRS), s, LANES), F32),
                        pltpu.VMEM((len(DILATED_PAIRS), s, LANES), F32)],
        compiler_params=_cparams(("parallel", "parallel")),
        name="dilated_attn",
    )(attn, attn, attn, cos, s1, s2)


def _rope_tables(positions):
    half = ROT_DIM // 2
    inv_freq = ROPE_THETA ** (-jnp.arange(half, dtype=F32) * 2.0 / ROT_DIM)
    ang = positions.astype(F32)[..., None] * inv_freq
    cos, sin = jnp.cos(ang), jnp.sin(ang)
    lead = ang.shape[:-1]
    ones = jnp.ones(lead + (HEAD_DIM - ROT_DIM,), F32)
    z = lambda n: jnp.zeros(lead + (n,), F32)
    c64 = jnp.concatenate([cos, cos, ones], axis=-1)
    s1 = jnp.concatenate([-sin, z(HEAD_DIM - half)], axis=-1)
    s2 = jnp.concatenate([z(half), sin, z(HEAD_DIM - ROT_DIM)], axis=-1)
    rep = lambda t: jnp.concatenate([t] * (LANES // HEAD_DIM), axis=-1)
    return rep(c64), rep(s1), rep(s2)


def _natten_kernel(q_ref, k_ref, v_ref, bias_ref, o_ref, *, rows, win_rows):
    lane = lax.broadcasted_iota(jnp.int32, (GRID_W, LANES), 1)
    head0 = lane < HEAD_DIM
    scale = HEAD_DIM ** -0.5
    nkeys = win_rows * GRID_W

    def group(gi, carry):
        units = []
        for u in range(ATTN_GROUP):
            r = gi * ATTN_GROUP + u
            r_start = jnp.clip(r - win_rows // 2, 0, rows - win_rows)
            q_rows = pl.ds(pl.multiple_of(r * GRID_W, GRID_W), GRID_W)
            k_rows = pl.ds(pl.multiple_of(r_start * GRID_W, GRID_W), nkeys)
            units.append((q_rows, r - r_start, q_ref[0, q_rows, :],
                          k_ref[0, k_rows, :], v_ref[0, k_rows, :]))
        scores = []
        for _, delta, q, kw, _ in units:
            for hh, hmask in enumerate((head0, jnp.logical_not(head0))):
                qh = jnp.where(hmask, q, jnp.zeros_like(q))
                s = lax.dot_general(qh, kw, (((1,), (1,)), ((), ())),
                                    preferred_element_type=F32) * scale
                scores.append(s + bias_ref[hh, delta])
        probs, invs = [], []
        for s in scores:
            m = jnp.max(s, axis=-1, keepdims=True)
            p = jnp.exp(s - m)
            probs.append(p.astype(BF16))
            invs.append(1.0 / jnp.sum(p, axis=-1, keepdims=True))
        for u, (q_rows, _, _, _, vw) in enumerate(units):
            o0 = jnp.dot(probs[2 * u], vw, preferred_element_type=F32) * invs[2 * u]
            o1 = jnp.dot(probs[2 * u + 1], vw, preferred_element_type=F32) * invs[2 * u + 1]
            o_ref[0, q_rows, :] = jnp.where(head0, o0, o1)
        return carry

    lax.fori_loop(0, rows // ATTN_GROUP, group, 0)


def _natten_bias_tables(rpb, rows):
    wr = min(NA_WIN_ROWS, rows)
    wc = NA_WIN_COLS
    cols = np.arange(GRID_W)
    c_start = np.clip(cols - wc // 2, 0, GRID_W - wc)
    col_in = (cols[None, :] >= c_start[:, None]) & (cols[None, :] < c_start[:, None] + wc)
    coff = np.clip(cols[None, :] - cols[:, None], -(wc - 1), wc - 1) + wc - 1
    onehot = (coff[None] == np.arange(2 * wc - 1)[:, None, None]).astype(np.float32)
    by_col = jnp.einsum("lhrc,cqk->lhrqk", rpb.astype(F32), onehot, precision=lax.Precision.HIGHEST)
    by_col = jnp.where(col_in, by_col, NEG)
    top = NA_WIN_ROWS - 1
    tabs = jnp.stack([by_col[:, :, top - dl:top - dl + wr] for dl in range(wr)], axis=2)
    tabs = tabs.transpose(0, 1, 2, 4, 3, 5)
    return tabs.reshape(rpb.shape[0], rpb.shape[1], wr, GRID_W, wr * GRID_W)


def _neighborhood_attention(attn, bias_tab):
    b, s, _ = attn.shape
    rows = s // GRID_W
    wr = min(NA_WIN_ROWS, rows)
    hp = W_MIX // LANES
    per_cb = COLBLK // LANES
    col = lambda cb: pl.BlockSpec((1, s, LANES), lambda bi, h, cb=cb: (bi, 0, cb * per_cb + h))
    return pl.pallas_call(
        functools.partial(_natten_kernel, rows=rows, win_rows=wr),
        grid=(b, hp),
        in_specs=[col(CB_NQ), col(CB_NK), col(CB_NV),
                  pl.BlockSpec((LANES // HEAD_DIM, wr, GRID_W, wr * GRID_W),
                               lambda bi, h: (h, 0, 0, 0))],
        out_specs=pl.BlockSpec((1, s, LANES), lambda bi, h: (bi, 0, h)),
        out_shape=jax.ShapeDtypeStruct((b, s, W_MIX), F32),
        compiler_params=_cparams(("parallel", "parallel")),
        name="natten",
    )(attn, attn, attn, bias_tab)


def _seg_sum(x, ones_bd):
    hi = x.astype(BF16)
    lo = (x - hi.astype(F32)).astype(BF16)
    return (jnp.dot(hi, ones_bd, preferred_element_type=F32)
            + jnp.dot(lo, ones_bd, preferred_element_type=F32))


def _rwkv_pre_kernel(r_ref, k_ref, v_ref, l_ref,
                     rp_ref, kp_ref, vp_ref, lp_ref,
                     rn_ref, kn_ref, vn_ref, ln_ref,
                     mu_ref, mul_ref, wc_ref, w0_ref, a0_ref, kk_ref, ka_ref, rk_ref, ones_ref,
                     r_o, w_o, k_o, v_o, a_o, b_o, bonus_o, *, tm):
    i = pl.program_id(1)
    first = i == 0
    last = i == pl.num_programs(1) - 1
    row = lax.broadcasted_iota(jnp.int32, (tm, COLBLK), 0)
    ones_bd = ones_ref[...]

    def neighbours(x_ref, p_ref, n_ref):
        x = x_ref[0]
        prev_row = jnp.where(first, 0.0, p_ref[0, SUBLANES - 1:SUBLANES, :])
        next_row = jnp.where(last, 0.0, n_ref[0, 0:1, :])
        prev = jnp.where(row == 0, prev_row, pltpu.roll(x, 1, axis=0))
        nxt = jnp.where(row == tm - 1, next_row, pltpu.roll(x, tm - 1, axis=0))
        return x, (prev, nxt)

    pr, nb_r = neighbours(r_ref, rp_ref, rn_ref)
    pk, nb_k = neighbours(k_ref, kp_ref, kn_ref)
    pv, nb_v = neighbours(v_ref, vp_ref, vn_ref)
    pl_, nb_l = neighbours(l_ref, lp_ref, ln_ref)

    bonus = jnp.zeros((tm, COLBLK), F32)
    for d in range(2):
        r = pr + (nb_r[d] - pr) * mu_ref[d, 0:1, :]
        k = pk + (nb_k[d] - pk) * mu_ref[d, 1:2, :]
        v = pv + (nb_v[d] - pv) * mu_ref[d, 2:3, :]
        xl = pl_ + (nb_l[d] - pl_) * mul_ref[d:d + 1, :]
        zw = jnp.dot(jnp.tanh(xl[:, :LANES]).astype(BF16), wc_ref[d, :LANES, :],
                     preferred_element_type=F32)
        za = jnp.dot(xl[:, LANES:2 * LANES].astype(BF16), wc_ref[d, LANES:2 * LANES, :],
                     preferred_element_type=F32)
        z = zw + za
        wz = w0_ref[d:d + 1, :] + z[:, :COLBLK]
        az = a0_ref[d:d + 1, :] + z[:, COLBLK:]
        decay = jnp.exp(-DECAY_SCALE * _sigmoid(wz))
        a = _sigmoid(az)
        kk = k * kk_ref[...]
        kk = kk * lax.rsqrt(jnp.maximum(_seg_sum(kk * kk, ones_bd), 1e-24))
        k2 = k * (1.0 + (a - 1.0) * ka_ref[...])
        bonus = bonus + _seg_sum(r * k2 * rk_ref[...], ones_bd) * v
        r_o[d, 0] = r.astype(r_o.dtype)
        w_o[d, 0] = decay
        k_o[d, 0] = k2.astype(k_o.dtype)
        v_o[d, 0] = v
        a_o[d, 0] = -kk
        b_o[d, 0] = (kk * a).astype(b_o.dtype)
    bonus_o[0] = bonus


def _rwkv_pre(proj, mu_rkv, mu_lora, w_comb, w0, a0, k_k, k_a, r_k, ones_bd, *, tm=256):
    b, s, _ = proj.shape
    nsub = tm // SUBLANES
    last_sub = s // SUBLANES - 1
    main = lambda cb: pl.BlockSpec((1, tm, COLBLK), lambda bi, i, cb=cb: (bi, i, cb))
    prev = lambda cb: pl.BlockSpec(
        (1, SUBLANES, COLBLK), lambda bi, i, cb=cb: (bi, jnp.maximum(i * nsub - 1, 0), cb))
    nxt = lambda cb: pl.BlockSpec(
        (1, SUBLANES, COLBLK), lambda bi, i, cb=cb: (bi, jnp.minimum((i + 1) * nsub, last_sub), cb))
    cbs = (CB_PR, CB_PK, CB_PV, CB_LORA)
    const = lambda shape: pl.BlockSpec(shape, lambda bi, i: (0,) * len(shape))
    dir_out = pl.BlockSpec((2, 1, tm, COLBLK), lambda bi, i: (0, bi, i, 0))
    dir_shape = lambda dt: jax.ShapeDtypeStruct((2, b, s, COLBLK), dt)
    return pl.pallas_call(
        functools.partial(_rwkv_pre_kernel, tm=tm),
        grid=(b, s // tm),
        in_specs=([main(cb) for cb in cbs] + [prev(cb) for cb in cbs] + [nxt(cb) for cb in cbs]
                  + [const((2, 3, COLBLK)), const((2, COLBLK)), const((2, 2 * LANES, 2 * COLBLK)),
                     const((2, COLBLK)), const((2, COLBLK)), const((1, COLBLK)), const((1, COLBLK)),
                     const((1, COLBLK)), const((COLBLK, COLBLK))]),
        out_specs=[dir_out] * 6 + [pl.BlockSpec((1, tm, COLBLK), lambda bi, i: (bi, i, 0))],
        out_shape=([dir_shape(dt) for dt in (BF16, F32, BF16, F32, F32, BF16)]
                   + [jax.ShapeDtypeStruct((b, s, COLBLK), F32)]),
        compiler_params=_cparams(("parallel", "parallel")),
        name="rwkv_pre",
    )(*([proj] * 12), mu_rkv, mu_lora, w_comb, w0, a0, k_k, k_a, r_k, ones_bd)


def _rwkv_scan_kernel(r_ref, w_ref, k_ref, v_ref, a_ref, b_ref, an_ref, y_ref,
                      st_scr, sa_scr, rows_scr, *, tt):
    d = pl.program_id(0)
    n = HEAD_DIM

    @pl.when(pl.program_id(1) == 0)
    def _():
        st_scr[...] = jnp.zeros_like(st_scr)
        sa_scr[...] = jnp.zeros_like(sa_scr)

    pack = rows_scr.shape[-2]

    def row(slot, q, k):
        return jnp.broadcast_to(rows_scr[slot, q, k // pack, pl.ds(k % pack, 1), :],
                                sa_scr.shape[1:])

    def step(j, t, a_next, sa, gam):
        slot = j % 2
        gam = gam * w_ref[0, t]
        inv = 1.0 / gam
        v = v_ref[0, t]
        rows_scr[slot, 0] = b_ref[0, t].astype(F32) * inv
        rows_scr[slot, 1] = k_ref[0, t].astype(F32) * inv
        rows_scr[slot, 2] = r_ref[0, t].astype(F32) * gam
        rows_scr[slot, 3] = a_next * gam
        y = jnp.zeros_like(v)
        sa_next = [jnp.zeros_like(v), jnp.zeros_like(v)]
        for k in range(n):
            new = st_scr[k] + (sa * row(slot, 0, k) + v * row(slot, 1, k))
            st_scr[k] = new
            y = y + new * row(slot, 2, k)
            sa_next[k % 2] = sa_next[k % 2] + new * row(slot, 3, k)
        y_ref[0, t] = y
        return sa_next[0] + sa_next[1], gam

    def body(j, carry):
        t = j + d * (tt - 1 - 2 * j)
        return step(j, t, a_ref[0, t + 1 - 2 * d], *carry)

    carry = (sa_scr[...], jnp.ones(rows_scr.shape[2:], F32))
    carry = lax.fori_loop(0, tt - 1, body, carry)
    sa, gam = step(tt - 1, (1 - d) * (tt - 1), an_ref[0, 0], *carry)
    sa_scr[...] = sa
    rows_scr[0, 0] = gam
    for k in range(n):
        st_scr[k] = st_scr[k] * row(0, 0, k)


def _rwkv_scan(r, w, k, v, a, b, *, tt=16):
    _, s, n, c = v.shape
    nt = s // tt
    state_tile = (n // SUBLANES, SUBLANES, c)
    vec_tile = (n // (2 * SUBLANES), 2 * SUBLANES, c)
    chunk = lambda d, i: (d, i + d * (nt - 1 - 2 * i), 0, 0, 0)
    vec_spec = pl.BlockSpec((1, tt) + vec_tile, chunk)
    state_spec = pl.BlockSpec((1, tt) + state_tile, chunk)

    def next_first_step(d, i):
        fwd = jnp.minimum((i + 1) * tt, s - 1)
        bwd = jnp.maximum((nt - 1 - i) * tt - 1, 0)
        return (d, jnp.where(d == 0, fwd, bwd), 0, 0, 0)

    r, w, k, a, b = (t.reshape((2, s) + vec_tile) for t in (r, w, k, a, b))
    y = pl.pallas_call(
        functools.partial(_rwkv_scan_kernel, tt=tt),
        grid=(2, nt),
        in_specs=[vec_spec, vec_spec, vec_spec, state_spec, vec_spec, vec_spec,
                  pl.BlockSpec((1, 1) + vec_tile, next_first_step)],
        out_specs=state_spec,
        out_shape=jax.ShapeDtypeStruct((2, s) + state_tile, F32),
        scratch_shapes=[pltpu.VMEM((n,) + state_tile, F32), pltpu.VMEM(state_tile, F32),
                        pltpu.VMEM((2, 4) + vec_tile, F32)],
        compiler_params=_cparams(("parallel", "arbitrary")),
        name="rwkv_scan",
    )(r, w, k, v.reshape((2, s) + state_tile), a, b, a)
    return y.reshape(2, s, n, c)


def _to_chain_lanes(t):
    _, b, s, _ = t.shape
    t = t.reshape(2, b, s, N_HEADS, HEAD_DIM).transpose(0, 2, 4, 1, 3)
    return t.reshape(2, s, HEAD_DIM, b * N_HEADS)


def _from_chain_lanes(t, b):
    _, s, _, _ = t.shape
    t = t.reshape(2, s, HEAD_DIM, b, N_HEADS).transpose(0, 3, 1, 4, 2)
    return t.reshape(2, b, s, W_MIX)


def _merge_kernel(y_ref, bonus_ref, l_ref, gz0_ref, gz1_ref, gz2_ref, oa_ref, ob_ref, x_ref, gt_ref,
                  gnw_ref, gnb_ref, ones_ref, gc_ref, wb_ref, wo_ref, o_ref):
    ones_bd = ones_ref[...]
    inv_n = 1.0 / HEAD_DIM
    acc = bonus_ref[0]
    for d in range(2):
        y = y_ref[d, 0]
        mean = _seg_sum(y, ones_bd) * inv_n
        yc = y - mean
        var = _seg_sum(yc * yc, ones_bd) * inv_n
        acc = acc + (yc * lax.rsqrt(var + GN_EPS)) * gnw_ref[...] + gnb_ref[...]
    g = jnp.dot(_sigmoid(l_ref[0]).astype(BF16), gc_ref[...], preferred_element_type=F32)
    o_c = acc * g
    merged = (_sigmoid(gz0_ref[0])
              * jnp.dot(oa_ref[0].astype(BF16), wb_ref[0], preferred_element_type=F32)
              + _sigmoid(gz1_ref[0])
              * jnp.dot(ob_ref[0].astype(BF16), wb_ref[1], preferred_element_type=F32)
              + _sigmoid(gz2_ref[0])
              * jnp.dot(o_c.astype(BF16), wb_ref[2], preferred_element_type=F32))
    out = jnp.dot(merged.astype(BF16), wo_ref[...], preferred_element_type=F32)
    o_ref[0] = x_ref[0] + gt_ref[0] * out


def _merge(y, bonus, proj, o_a, o_b, x, gate, gn_w, gn_b, ones_bd, g_comb, w_branch, w_out, *, tm=512):
    b, s, d = x.shape
    gz_per = d // COLBLK
    tok = lambda width, cb: pl.BlockSpec((1, tm, width), lambda bi, i, cb=cb: (bi, i, cb))
    const = lambda shape: pl.BlockSpec(shape, lambda bi, i: (0,) * len(shape))
    return pl.pallas_call(
        _merge_kernel,
        grid=(b, s // tm),
        in_specs=[pl.BlockSpec((2, 1, tm, COLBLK), lambda bi, i: (0, bi, i, 0)),
                  tok(COLBLK, 0),
                  tok(COLBLK, CB_LORA),
                  tok(d, CB_GZ // gz_per), tok(d, CB_GZ // gz_per + 1), tok(d, CB_GZ // gz_per + 2),
                  tok(COLBLK, 0), tok(COLBLK, 0), tok(d, 0),
                  pl.BlockSpec((1, 1, d), lambda bi, i: (bi, 0, 0)),
                  const((1, COLBLK)), const((1, COLBLK)), const((COLBLK, COLBLK)),
                  const((COLBLK, COLBLK)), const((3, COLBLK, d)), const((d, d))],
        out_specs=pl.BlockSpec((1, tm, d), lambda bi, i: (bi, i, 0)),
        out_shape=jax.ShapeDtypeStruct((b, s, d), F32),
        compiler_params=_cparams(("parallel", "parallel")),
        name="merge_out",
    )(y, bonus, proj, proj, proj, proj, o_a, o_b, x, gate, gn_w, gn_b, ones_bd, g_comb, w_branch, w_out)


def _split_w_in(w_in):
    d = w_in.shape[0]
    n_lora = 4 * DECAY_LORA + GATE_LORA
    n_rest = N_ATTN + CB_LORA * COLBLK + n_lora
    rest, gates = w_in[:, N_ATTN:n_rest], w_in[:, n_rest:]
    zpad = jnp.zeros((d, COLBLK - n_lora), w_in.dtype)
    return w_in[:, :N_ATTN].astype(BF16), jnp.concatenate([rest, zpad, gates], axis=1).astype(BF16)


def _lora_params(mu_w, mu_a, w_up, a_up, g_up):
    r = DECAY_LORA
    mu_l = jnp.zeros((2, COLBLK), F32)
    w_comb = jnp.zeros((2, 2 * LANES, 2 * COLBLK), F32)
    for d in range(2):
        wcol, acol = d * r, 2 * r + d * r
        mu_l = mu_l.at[d, wcol:wcol + r].set(mu_w[d]).at[d, acol:acol + r].set(mu_a[d])
        w_comb = w_comb.at[d, wcol:wcol + r, :COLBLK].set(w_up[d])
        w_comb = w_comb.at[d, acol:acol + r, COLBLK:].set(a_up[d])
    g_comb = jnp.zeros((COLBLK, COLBLK), F32).at[4 * r:4 * r + GATE_LORA].set(g_up)
    return mu_l, w_comb.astype(BF16), g_comb.astype(BF16)


def _head_block_ones():
    seg = np.arange(COLBLK) // HEAD_DIM
    return jnp.asarray(seg[:, None] == seg[None, :], dtype=BF16)


def kernel(x, c, positions, ada_w, ada_b, norm_gains, ffn_wi, ffn_wo, w_in, rpb, mu_rkv, mu_w, mu_a,
           w0, w_up, a0, a_up, g_up, k_k, k_a, r_k, gn_w, gn_b, w_branch, w_out, final_norm):
    depth = ada_w.shape[0]
    b, s, d = x.shape
    assert d == 2 * COLBLK and s % 1024 == 0 and s % (GRID_W * NA_WIN_ROWS) == 0

    mod = _modulation(c, ada_w, ada_b)
    cos, s1, s2 = _rope_tables(positions)
    ones_bd = _head_block_ones()
    bias_tabs = _natten_bias_tables(rpb, s // GRID_W)
    ffn_wi_bf, ffn_wo_bf = ffn_wi.astype(BF16), ffn_wo.astype(BF16)
    w_branch_bf, w_out_bf = w_branch.astype(BF16), w_out.astype(BF16)
    row = lambda t: t.reshape(1, -1)

    for l in range(depth):
        m = mod[l].reshape(b, N_MOD, 1, d)
        sh1, sc1, gt1, sh2, sc2, gt2, sh3, sc3, gt3 = (m[:, i] for i in range(N_MOD))

        x = _ffn(x, norm_gains[l, 0], sh1, sc1, gt1, ffn_wi_bf[l, 0], ffn_wo_bf[l, 0])

        attn, proj = _in_proj(x, norm_gains[l, 1], sh2, sc2, *_split_w_in(w_in[l]))
        o_a = _dilated_attention(attn, cos, s1, s2)
        o_b = _neighborhood_attention(attn, bias_tabs[l])
        mu_l, w_comb, g_comb = _lora_params(mu_w[l], mu_a[l], w_up[l], a_up[l], g_up[l])
        *scan_in, bonus = _rwkv_pre(proj, mu_rkv[l], mu_l, w_comb, w0[l], a0[l],
                                    row(k_k[l]), row(k_a[l]), row(r_k[l]), ones_bd)
        y = _rwkv_scan(*(_to_chain_lanes(t) for t in scan_in))
        y = _from_chain_lanes(y, b)
        x = _merge(y, bonus, proj, o_a, o_b, x, gt2, row(gn_w[l]), row(gn_b[l]), ones_bd, g_comb,
                   w_branch_bf[l], w_out_bf[l])

        x = _ffn(x, norm_gains[l, 2], sh3, sc3, gt3, ffn_wi_bf[l, 1], ffn_wo_bf[l, 1],
                 final_norm if l == depth - 1 else None)
    return x
```

```python
import functools

import numpy as np
import jax
import jax.numpy as jnp
from jax import lax
from jax.experimental import pallas as pl
from jax.experimental.pallas import tpu as pltpu

F32 = jnp.float32
BF16 = jnp.bfloat16

HEAD_DIM = 64
N_HEADS = 8
W_MIX = N_HEADS * HEAD_DIM
DILATED_PAIRS = ((128, 1), (512, 4), (2048, 16))
QBLK = 128
ROT_DIM = HEAD_DIM // 4
ROPE_THETA = 500000.0
GRID_W = 64
NA_WIN_ROWS = 8
NA_WIN_COLS = 16
DECAY_LORA = 64
ICLR_LORA = 64
GATE_LORA = 128
DECAY_SCALE = 0.6065306597126334
N_MOD = 9
RMS_EPS = 1e-6
GN_EPS = 64e-5
NEG = -1e30

LANES = 128
SUBLANES = 8
VMEM_LIMIT = 48 * 1024 * 1024
DILATED_GROUP = 4
NATTEN_GROUP = 8

COLBLK = 512
CB_AQ, CB_AK, CB_AV = 0, 1, 2
CB_NQ, CB_NK, CB_NV = 3, 4, 5
N_ATTN = 6 * COLBLK
CB_PR, CB_PK, CB_PV = 0, 1, 2
CB_LORA = 3
CB_GZ = 4


def _cparams(sem):
    return pltpu.CompilerParams(dimension_semantics=sem, vmem_limit_bytes=VMEM_LIMIT)


def _sigmoid(x):
    return 0.5 * jnp.tanh(0.5 * x) + 0.5


def _mod_kernel(c_ref, w_ref, b_ref, o_ref):
    c = c_ref[...]
    cs = (c * _sigmoid(c)).astype(BF16)
    o_ref[0] = jnp.dot(cs, w_ref[0].astype(BF16), preferred_element_type=F32) + b_ref[0]


def _modulation(c, ada_w, ada_b):
    depth, d, nd = ada_w.shape
    b = c.shape[0]
    return pl.pallas_call(
        _mod_kernel,
        grid=(depth, nd // d),
        in_specs=[pl.BlockSpec((b, d), lambda l, j: (0, 0)),
                  pl.BlockSpec((1, d, d), lambda l, j: (l, 0, j)),
                  pl.BlockSpec((1, 1, d), lambda l, j: (l, 0, j))],
        out_specs=pl.BlockSpec((1, b, d), lambda l, j: (l, 0, j)),
        out_shape=jax.ShapeDtypeStruct((depth, b, nd), F32),
        compiler_params=_cparams(("parallel", "parallel")),
        name="adaln_mod",
    )(c, ada_w, ada_b.reshape(depth, 1, nd))


def _norm_mod(x, gain, shift, scale):
    ms = jnp.mean(x * x, axis=-1, keepdims=True)
    y = x * lax.rsqrt(ms + RMS_EPS) * gain
    return y * (1.0 + scale) + shift


def _ffn_kernel(*refs, final, tf):
    if final:
        x_ref, g_ref, sh_ref, sc_ref, gt_ref, wi_ref, wo_ref, fin_ref, o_ref = refs
    else:
        x_ref, g_ref, sh_ref, sc_ref, gt_ref, wi_ref, wo_ref, o_ref = refs
    x = x_ref[0]
    h = _norm_mod(x, g_ref[...], sh_ref[0], sc_ref[0]).astype(BF16)
    dff = wo_ref.shape[0]
    acc = None
    for lo in range(0, dff, tf):
        gate = jnp.dot(h, wi_ref[:, lo:lo + tf], preferred_element_type=F32)
        up = jnp.dot(h, wi_ref[:, dff + lo:dff + lo + tf], preferred_element_type=F32)
        act = ((gate * _sigmoid(gate)) * up).astype(BF16)
        part = jnp.dot(act, wo_ref[lo:lo + tf, :], preferred_element_type=F32)
        acc = part if acc is None else acc + part
    y = x + 0.5 * gt_ref[0] * acc
    if final:
        ms = jnp.mean(y * y, axis=-1, keepdims=True)
        y = y * lax.rsqrt(ms + RMS_EPS) * fin_ref[...]
    o_ref[0] = y


def _ffn(x, gain, shift, scale, gate, wi, wo, final_gain=None, *, tm=512, tf=256):
    b, s, d = x.shape
    dff = wo.shape[0]
    final = final_gain is not None
    vec = pl.BlockSpec((1, 1, d), lambda bi, i: (bi, 0, 0))
    resident = lambda shape: pl.BlockSpec(shape, lambda bi, i: (0, 0), pipeline_mode=pl.Buffered(1))
    in_specs = [pl.BlockSpec((1, tm, d), lambda bi, i: (bi, i, 0)),
                pl.BlockSpec((1, d), lambda bi, i: (0, 0)),
                vec, vec, vec,
                resident((d, 2 * dff)), resident((dff, d))]
    args = [x, gain.reshape(1, d), shift, scale, gate, wi, wo]
    if final:
        in_specs.append(pl.BlockSpec((1, d), lambda bi, i: (0, 0)))
        args.append(final_gain.reshape(1, d))
    return pl.pallas_call(
        functools.partial(_ffn_kernel, final=final, tf=tf),
        grid=(b, s // tm),
        in_specs=in_specs,
        out_specs=pl.BlockSpec((1, tm, d), lambda bi, i: (bi, i, 0)),
        out_shape=jax.ShapeDtypeStruct((b, s, d), F32),
        compiler_params=_cparams(("parallel", "parallel")),
        name="ffn_final" if final else "ffn",
    )(*args)


def _win_kernel(x_ref, g_ref, sh_ref, sc_ref, wa_ref, wr_ref, oa_ref, or_ref, h_scr):
    @pl.when(pl.program_id(2) == 0)
    def _():
        h = _norm_mod(x_ref[0], g_ref[...], sh_ref[0], sc_ref[0])
        h_scr[...] = h.astype(BF16)

    h = h_scr[...]
    oa_ref[0] = jnp.dot(h, wa_ref[...], preferred_element_type=F32).astype(BF16)
    or_ref[0] = jnp.dot(h, wr_ref[...], preferred_element_type=F32)


def _in_proj(x, gain, shift, scale, w_attn, w_rest, *, tm=1024, nj=4):
    b, s, d = x.shape
    ta, tr = w_attn.shape[1] // nj, w_rest.shape[1] // nj
    vec = pl.BlockSpec((1, 1, d), lambda bi, i, j: (bi, 0, 0))
    return pl.pallas_call(
        _win_kernel,
        grid=(b, s // tm, nj),
        in_specs=[pl.BlockSpec((1, tm, d), lambda bi, i, j: (bi, i, 0)),
                  pl.BlockSpec((1, d), lambda bi, i, j: (0, 0)),
                  vec, vec,
                  pl.BlockSpec((d, ta), lambda bi, i, j: (0, j)),
                  pl.BlockSpec((d, tr), lambda bi, i, j: (0, j))],
        out_specs=[pl.BlockSpec((1, tm, ta), lambda bi, i, j: (bi, i, j)),
                   pl.BlockSpec((1, tm, tr), lambda bi, i, j: (bi, i, j))],
        out_shape=[jax.ShapeDtypeStruct((b, s, w_attn.shape[1]), BF16),
                   jax.ShapeDtypeStruct((b, s, w_rest.shape[1]), F32)],
        scratch_shapes=[pltpu.VMEM((tm, d), BF16)],
        compiler_params=_cparams(("parallel", "parallel", "arbitrary")),
        name="in_proj",
    )(x, gain.reshape(1, d), shift, scale, w_attn, w_rest)


def _dilated_kernel(q_ref, k_ref, v_ref, cos_ref, s1_ref, s2_ref, o_ref,
                    q_scr, k_scr, v_scr, og_scr, lse_scr, *, seq, pad):
    cos, s1, s2 = cos_ref[0], s1_ref[0], s2_ref[0]

    def rotary(t):
        return (t * cos + pltpu.roll(t, LANES - ROT_DIM // 2, axis=1) * s1
                + pltpu.roll(t, ROT_DIM // 2, axis=1) * s2)

    q_scr[...] = rotary(q_ref[0].astype(F32))
    zeros = jnp.zeros((pad, LANES), F32)
    k_scr[pl.ds(0, pad), :] = zeros
    k_scr[pl.ds(pad + seq, pad), :] = zeros
    v_scr[pl.ds(0, pad), :] = zeros
    v_scr[pl.ds(pad + seq, pad), :] = zeros
    k_scr[pl.ds(pad, seq), :] = rotary(k_ref[0].astype(F32))
    v_scr[pl.ds(pad, seq), :] = v_ref[0].astype(F32)

    lane = lax.broadcasted_iota(jnp.int32, (QBLK, LANES), 1)
    head0 = lane < HEAD_DIM
    scale = HEAD_DIM ** -0.5

    for g, (window, dil) in enumerate(DILATED_PAIRS):
        radius = window // (2 * dil)
        cls_len = seq // dil
        nblk = cls_len // QBLK
        lead = radius if nblk > 1 else 0
        nkeys = QBLK + 2 * lead
        qi = lax.broadcasted_iota(jnp.int32, (QBLK, nkeys), 0)
        kj = lax.broadcasted_iota(jnp.int32, (QBLK, nkeys), 1)
        band = jnp.abs(kj - lead - qi) <= radius

        def group(gi, carry, g=g, dil=dil, lead=lead, nkeys=nkeys, cls_len=cls_len, nblk=nblk,
                  band=band, kj=kj):
            units = []
            for u in range(DILATED_GROUP):
                idx = gi * DILATED_GROUP + u
                cls = idx // nblk
                n = idx % nblk
                q_start = cls + dil * QBLK * n
                k_start = pad + cls + dil * (QBLK * n - lead)
                if dil == 1:
                    q_rows = pl.ds(q_start, QBLK)
                    k_rows = pl.ds(k_start, nkeys)
                else:
                    q_rows = pl.ds(q_start, QBLK, stride=dil)
                    k_rows = pl.ds(k_start, nkeys, stride=dil)
                key_pos = QBLK * n + kj - lead
                valid = band & (key_pos >= 0) & (key_pos < cls_len)
                units.append((q_rows, q_scr[q_rows, :], k_scr[k_rows, :].astype(BF16),
                              v_scr[k_rows, :].astype(BF16), valid))
            scores = []
            for _, q, kw, _, valid in units:
                for hmask in (head0, jnp.logical_not(head0)):
                    qh = jnp.where(hmask, q, 0.0).astype(BF16)
                    s = lax.dot_general(qh, kw, (((1,), (1,)), ((), ())),
                                        preferred_element_type=F32) * scale
                    scores.append(jnp.where(valid, s, NEG))
            probs, invs, lses = [], [], []
            for s in scores:
                m = jnp.max(s, axis=-1, keepdims=True)
                p = jnp.exp(s - m)
                den = jnp.sum(p, axis=-1, keepdims=True)
                probs.append(p.astype(BF16))
                invs.append(1.0 / den)
                lses.append(m + jnp.log(den))
            for u, (q_rows, _, _, vw, _) in enumerate(units):
                o0 = jnp.dot(probs[2 * u], vw, preferred_element_type=F32) * invs[2 * u]
                o1 = jnp.dot(probs[2 * u + 1], vw, preferred_element_type=F32) * invs[2 * u + 1]
                og_scr[g, q_rows, :] = jnp.where(head0, o0, o1)
                lse_scr[g, q_rows, :] = jnp.where(head0, lses[2 * u], lses[2 * u + 1])
            return carry

        lax.fori_loop(0, dil * nblk // DILATED_GROUP, group, 0)

    l0, l1, l2 = lse_scr[0], lse_scr[1], lse_scr[2]
    m = jnp.maximum(jnp.maximum(l0, l1), l2)
    e0, e1, e2 = jnp.exp(l0 - m), jnp.exp(l1 - m), jnp.exp(l2 - m)
    tot = e0 + e1 + e2
    o_ref[0] = (e0 / tot) * og_scr[0] + (e1 / tot) * og_scr[1] + (e2 / tot) * og_scr[2]


def _dilated_attention(attn, cos, s1, s2):
    b, s, _ = attn.shape
    pad = max((w // (2 * d)) * d for w, d in DILATED_PAIRS)
    hp = W_MIX // LANES
    per_cb = COLBLK // LANES
    col = lambda cb: pl.BlockSpec((1, s, LANES), lambda bi, h, cb=cb: (bi, 0, cb * per_cb + h))
    tab = pl.BlockSpec((1, s, LANES), lambda bi, h: (bi, 0, 0))
    return pl.pallas_call(
        functools.partial(_dilated_kernel, seq=s, pad=pad),
        grid=(b, hp),
        in_specs=[col(CB_AQ), col(CB_AK), col(CB_AV), tab, tab, tab],
        out_specs=pl.BlockSpec((1, s, LANES), lambda bi, h: (bi, 0, h)),
        out_shape=jax.ShapeDtypeStruct((b, s, W_MIX), F32),
        scratch_shapes=[pltpu.VMEM((s, LANES), F32),
                        pltpu.VMEM((s + 2 * pad, LANES), F32),
                        pltpu.VMEM((s + 2 * pad, LANES), F32),
                        pltpu.VMEM((len(DILATED_PAIRS), s, LANES), F32),
                        pltpu.VMEM((len(DILATED_PAIRS), s, LANES), F32)],
        compiler_params=_cparams(("parallel", "parallel")),
        name="dilated_attn",
    )(attn, attn, attn, cos, s1, s2)


def _rope_tables(positions):
    half = ROT_DIM // 2
    inv_freq = ROPE_THETA ** (-jnp.arange(half, dtype=F32) * 2.0 / ROT_DIM)
    ang = positions.astype(F32)[..., None] * inv_freq
    cos, sin = jnp.cos(ang), jnp.sin(ang)
    lead = ang.shape[:-1]
    ones = jnp.ones(lead + (HEAD_DIM - ROT_DIM,), F32)
    z = lambda n: jnp.zeros(lead + (n,), F32)
    c64 = jnp.concatenate([cos, cos, ones], axis=-1)
    s1 = jnp.concatenate([-sin, z(HEAD_DIM - half)], axis=-1)
    s2 = jnp.concatenate([z(half), sin, z(HEAD_DIM - ROT_DIM)], axis=-1)
    rep = lambda t: jnp.concatenate([t] * (LANES // HEAD_DIM), axis=-1)
    return rep(c64), rep(s1), rep(s2)


def _natten_kernel(q_ref, k_ref, v_ref, bias_ref, o_ref, *, rows, win_rows):
    lane = lax.broadcasted_iota(jnp.int32, (GRID_W, LANES), 1)
    head0 = lane < HEAD_DIM
    scale = HEAD_DIM ** -0.5
    nkeys = win_rows * GRID_W

    def group(gi, carry):
        units = []
        for u in range(NATTEN_GROUP):
            r = gi * NATTEN_GROUP + u
            r_start = jnp.clip(r - win_rows // 2, 0, rows - win_rows)
            q_rows = pl.ds(pl.multiple_of(r * GRID_W, GRID_W), GRID_W)
            k_rows = pl.ds(pl.multiple_of(r_start * GRID_W, GRID_W), nkeys)
            units.append((q_rows, r - r_start, q_ref[0, q_rows, :],
                          k_ref[0, k_rows, :], v_ref[0, k_rows, :]))
        scores = []
        for _, delta, q, kw, _ in units:
            for hh, hmask in enumerate((head0, jnp.logical_not(head0))):
                qh = jnp.where(hmask, q, jnp.zeros_like(q))
                s = lax.dot_general(qh, kw, (((1,), (1,)), ((), ())),
                                    preferred_element_type=F32) * scale
                scores.append(s + bias_ref[hh, delta])
        probs, invs = [], []
        for s in scores:
            m = jnp.max(s, axis=-1, keepdims=True)
            p = jnp.exp(s - m)
            probs.append(p.astype(BF16))
            invs.append(1.0 / jnp.sum(p, axis=-1, keepdims=True))
        for u, (q_rows, _, _, _, vw) in enumerate(units):
            o0 = jnp.dot(probs[2 * u], vw, preferred_element_type=F32) * invs[2 * u]
            o1 = jnp.dot(probs[2 * u + 1], vw, preferred_element_type=F32) * invs[2 * u + 1]
            o_ref[0, q_rows, :] = jnp.where(head0, o0, o1)
        return carry

    lax.fori_loop(0, rows // NATTEN_GROUP, group, 0)


def _natten_bias_tables(rpb, rows):
    wr = min(NA_WIN_ROWS, rows)
    wc = NA_WIN_COLS
    cols = np.arange(GRID_W)
    c_start = np.clip(cols - wc // 2, 0, GRID_W - wc)
    col_in = (cols[None, :] >= c_start[:, None]) & (cols[None, :] < c_start[:, None] + wc)
    coff = np.clip(cols[None, :] - cols[:, None], -(wc - 1), wc - 1) + wc - 1
    onehot = (coff[None] == np.arange(2 * wc - 1)[:, None, None]).astype(np.float32)
    by_col = jnp.einsum("lhrc,cqk->lhrqk", rpb.astype(F32), onehot, precision=lax.Precision.HIGHEST)
    by_col = jnp.where(col_in, by_col, NEG)
    top = NA_WIN_ROWS - 1
    tabs = jnp.stack([by_col[:, :, top - dl:top - dl + wr] for dl in range(wr)], axis=2)
    tabs = tabs.transpose(0, 1, 2, 4, 3, 5)
    return tabs.reshape(rpb.shape[0], rpb.shape[1], wr, GRID_W, wr * GRID_W)


def _neighborhood_attention(attn, bias_tab):
    b, s, _ = attn.shape
    rows = s // GRID_W
    wr = min(NA_WIN_ROWS, rows)
    hp = W_MIX // LANES
    per_cb = COLBLK // LANES
    col = lambda cb: pl.BlockSpec((1, s, LANES), lambda bi, h, cb=cb: (bi, 0, cb * per_cb + h))
    return pl.pallas_call(
        functools.partial(_natten_kernel, rows=rows, win_rows=wr),
        grid=(b, hp),
        in_specs=[col(CB_NQ), col(CB_NK), col(CB_NV),
                  pl.BlockSpec((LANES // HEAD_DIM, wr, GRID_W, wr * GRID_W),
                               lambda bi, h: (h, 0, 0, 0))],
        out_specs=pl.BlockSpec((1, s, LANES), lambda bi, h: (bi, 0, h)),
        out_shape=jax.ShapeDtypeStruct((b, s, W_MIX), F32),
        compiler_params=_cparams(("parallel", "parallel")),
        name="natten",
    )(attn, attn, attn, bias_tab)


def _seg_sum(x, ones_bd):
    return jnp.dot(x.astype(BF16), ones_bd, preferred_element_type=F32)


def _rwkv_pre_kernel(r_ref, k_ref, v_ref, l_ref,
                     rp_ref, kp_ref, vp_ref, lp_ref,
                     rn_ref, kn_ref, vn_ref, ln_ref,
                     mu_ref, mul_ref, wc_ref, w0_ref, a0_ref, kk_ref, ka_ref, rk_ref, ones_ref,
                     r_o, w_o, k_o, v_o, a_o, b_o, bonus_o, *, tm):
    i = pl.program_id(1)
    first = i == 0
    last = i == pl.num_programs(1) - 1
    row = lax.broadcasted_iota(jnp.int32, (tm, COLBLK), 0)
    ones_bd = ones_ref[...]

    def neighbours(x_ref, p_ref, n_ref):
        x = x_ref[0]
        prev_row = jnp.where(first, 0.0, p_ref[0, SUBLANES - 1:SUBLANES, :])
        next_row = jnp.where(last, 0.0, n_ref[0, 0:1, :])
        prev = jnp.where(row == 0, prev_row, pltpu.roll(x, 1, axis=0))
        nxt = jnp.where(row == tm - 1, next_row, pltpu.roll(x, tm - 1, axis=0))
        return x, (prev, nxt)

    pr, nb_r = neighbours(r_ref, rp_ref, rn_ref)
    pk, nb_k = neighbours(k_ref, kp_ref, kn_ref)
    pv, nb_v = neighbours(v_ref, vp_ref, vn_ref)
    pl_, nb_l = neighbours(l_ref, lp_ref, ln_ref)

    bonus = jnp.zeros((tm, COLBLK), F32)
    for d in range(2):
        r = pr + (nb_r[d] - pr) * mu_ref[d, 0:1, :]
        k = pk + (nb_k[d] - pk) * mu_ref[d, 1:2, :]
        v = pv + (nb_v[d] - pv) * mu_ref[d, 2:3, :]
        xl = pl_ + (nb_l[d] - pl_) * mul_ref[d:d + 1, :]
        zw = jnp.dot(jnp.tanh(xl[:, :LANES]).astype(BF16), wc_ref[d, :LANES, :],
                     preferred_element_type=F32)
        za = jnp.dot(xl[:, LANES:2 * LANES].astype(BF16), wc_ref[d, LANES:2 * LANES, :],
                     preferred_element_type=F32)
        z = zw + za
        wz = w0_ref[d:d + 1, :] + z[:, :COLBLK]
        az = a0_ref[d:d + 1, :] + z[:, COLBLK:]
        decay = jnp.exp(-DECAY_SCALE * _sigmoid(wz))
        a = _sigmoid(az)
        kk = k * kk_ref[...]
        kk = kk * lax.rsqrt(jnp.maximum(_seg_sum(kk * kk, ones_bd), 1e-24))
        k2 = k * (1.0 + (a - 1.0) * ka_ref[...])
        bonus = bonus + _seg_sum(r * k2 * rk_ref[...], ones_bd) * v
        r_o[d, 0] = r.astype(r_o.dtype)
        w_o[d, 0] = decay
        k_o[d, 0] = k2.astype(k_o.dtype)
        v_o[d, 0] = v
        a_o[d, 0] = -kk
        b_o[d, 0] = (kk * a).astype(b_o.dtype)
    bonus_o[0] = bonus


def _rwkv_pre(proj, mu_rkv, mu_lora, w_comb, w0, a0, k_k, k_a, r_k, ones_bd, *, tm=256):
    b, s, _ = proj.shape
    nsub = tm // SUBLANES
    last_sub = s // SUBLANES - 1
    main = lambda cb: pl.BlockSpec((1, tm, COLBLK), lambda bi, i, cb=cb: (bi, i, cb))
    prev = lambda cb: pl.BlockSpec(
        (1, SUBLANES, COLBLK), lambda bi, i, cb=cb: (bi, jnp.maximum(i * nsub - 1, 0), cb))
    nxt = lambda cb: pl.BlockSpec(
        (1, SUBLANES, COLBLK), lambda bi, i, cb=cb: (bi, jnp.minimum((i + 1) * nsub, last_sub), cb))
    cbs = (CB_PR, CB_PK, CB_PV, CB_LORA)
    const = lambda shape: pl.BlockSpec(shape, lambda bi, i: (0,) * len(shape))
    dir_out = pl.BlockSpec((2, 1, tm, COLBLK), lambda bi, i: (0, bi, i, 0))
    dir_shape = lambda dt: jax.ShapeDtypeStruct((2, b, s, COLBLK), dt)
    return pl.pallas_call(
        functools.partial(_rwkv_pre_kernel, tm=tm),
        grid=(b, s // tm),
        in_specs=([main(cb) for cb in cbs] + [prev(cb) for cb in cbs] + [nxt(cb) for cb in cbs]
                  + [const((2, 3, COLBLK)), const((2, COLBLK)), const((2, 2 * LANES, 2 * COLBLK)),
                     const((2, COLBLK)), const((2, COLBLK)), const((1, COLBLK)), const((1, COLBLK)),
                     const((1, COLBLK)), const((COLBLK, COLBLK))]),
        out_specs=[dir_out] * 6 + [pl.BlockSpec((1, tm, COLBLK), lambda bi, i: (bi, i, 0))],
        out_shape=([dir_shape(dt) for dt in (BF16, F32, BF16, F32, F32, BF16)]
                   + [jax.ShapeDtypeStruct((b, s, COLBLK), F32)]),
        compiler_params=_cparams(("parallel", "parallel")),
        name="rwkv_pre",
    )(*([proj] * 12), mu_rkv, mu_lora, w_comb, w0, a0, k_k, k_a, r_k, ones_bd)


def _rwkv_scan_kernel(r_ref, w_ref, k_ref, v_ref, a_ref, b_ref, an_ref, y_ref,
                      st_scr, sa_scr, rows_scr, *, tt):
    d = pl.program_id(0)
    n = HEAD_DIM

    @pl.when(pl.program_id(1) == 0)
    def _():
        st_scr[...] = jnp.zeros_like(st_scr)
        sa_scr[...] = jnp.zeros_like(sa_scr)

    pack = rows_scr.shape[-2]

    def row(slot, q, k):
        return jnp.broadcast_to(rows_scr[slot, q, k // pack, pl.ds(k % pack, 1), :],
                                sa_scr.shape[1:])

    def step(j, t, a_next, sa, gam):
        slot = j % 2
        gam = gam * w_ref[0, t]
        inv = 1.0 / gam
        v = v_ref[0, t]
        rows_scr[slot, 0] = b_ref[0, t].astype(F32) * inv
        rows_scr[slot, 1] = k_ref[0, t].astype(F32) * inv
        rows_scr[slot, 2] = r_ref[0, t].astype(F32) * gam
        rows_scr[slot, 3] = a_next * gam
        y = jnp.zeros_like(v)
        sa_next = [jnp.zeros_like(v), jnp.zeros_like(v)]
        for k in range(n):
            new = st_scr[k] + (sa * row(slot, 0, k) + v * row(slot, 1, k))
            st_scr[k] = new
            y = y + new * row(slot, 2, k)
            sa_next[k % 2] = sa_next[k % 2] + new * row(slot, 3, k)
        y_ref[0, t] = y
        return sa_next[0] + sa_next[1], gam

    def body(j, carry):
        t = j + d * (tt - 1 - 2 * j)
        return step(j, t, a_ref[0, t + 1 - 2 * d], *carry)

    carry = (sa_scr[...], jnp.ones(rows_scr.shape[2:], F32))
    carry = lax.fori_loop(0, tt - 1, body, carry)
    sa, gam = step(tt - 1, (1 - d) * (tt - 1), an_ref[0, 0], *carry)
    sa_scr[...] = sa
    rows_scr[0, 0] = gam
    for k in range(n):
        st_scr[k] = st_scr[k] * row(0, 0, k)


def _rwkv_scan(r, w, k, v, a, b, *, tt=32):
    _, s, n, c = v.shape
    nt = s // tt
    state_tile = (n // SUBLANES, SUBLANES, c)
    vec_tile = (n // (2 * SUBLANES), 2 * SUBLANES, c)
    chunk = lambda d, i: (d, i + d * (nt - 1 - 2 * i), 0, 0, 0)
    vec_spec = pl.BlockSpec((1, tt) + vec_tile, chunk)
    state_spec = pl.BlockSpec((1, tt) + state_tile, chunk)

    def next_first_step(d, i):
        fwd = jnp.minimum((i + 1) * tt, s - 1)
        bwd = jnp.maximum((nt - 1 - i) * tt - 1, 0)
        return (d, jnp.where(d == 0, fwd, bwd), 0, 0, 0)

    r, w, k, a, b = (t.reshape((2, s) + vec_tile) for t in (r, w, k, a, b))
    y = pl.pallas_call(
        functools.partial(_rwkv_scan_kernel, tt=tt),
        grid=(2, nt),
        in_specs=[vec_spec, vec_spec, vec_spec, state_spec, vec_spec, vec_spec,
                  pl.BlockSpec((1, 1) + vec_tile, next_first_step)],
        out_specs=state_spec,
        out_shape=jax.ShapeDtypeStruct((2, s) + state_tile, F32),
        scratch_shapes=[pltpu.VMEM((n,) + state_tile, F32), pltpu.VMEM(state_tile, F32),
                        pltpu.VMEM((2, 4) + vec_tile, F32)],
        compiler_params=_cparams(("parallel", "arbitrary")),
        name="rwkv_scan",
    )(r, w, k, v.reshape((2, s) + state_tile), a, b, a)
    return y.reshape(2, s, n, c)


def _to_chain_lanes(t):
    _, b, s, _ = t.shape
    t = t.reshape(2, b, s, N_HEADS, HEAD_DIM).transpose(0, 2, 4, 1, 3)
    return t.reshape(2, s, HEAD_DIM, b * N_HEADS)


def _from_chain_lanes(t, b):
    _, s, _, _ = t.shape
    t = t.reshape(2, s, HEAD_DIM, b, N_HEADS).transpose(0, 3, 1, 4, 2)
    return t.reshape(2, b, s, W_MIX)


def _merge_kernel(y_ref, bonus_ref, l_ref, gz0_ref, gz1_ref, gz2_ref, oa_ref, ob_ref, x_ref, gt_ref,
                  gnw_ref, gnb_ref, ones_ref, gc_ref, wb_ref, wo_ref, o_ref):
    ones_bd = ones_ref[...]
    inv_n = 1.0 / HEAD_DIM
    acc = bonus_ref[0]
    for d in range(2):
        y = y_ref[d, 0]
        mean = _seg_sum(y, ones_bd) * inv_n
        yc = y - mean
        var = _seg_sum(yc * yc, ones_bd) * inv_n
        acc = acc + (yc * lax.rsqrt(var + GN_EPS)) * gnw_ref[...] + gnb_ref[...]
    g = jnp.dot(_sigmoid(l_ref[0]).astype(BF16), gc_ref[...], preferred_element_type=F32)
    o_c = acc * g
    merged = (_sigmoid(gz0_ref[0])
              * jnp.dot(oa_ref[0].astype(BF16), wb_ref[0], preferred_element_type=F32)
              + _sigmoid(gz1_ref[0])
              * jnp.dot(ob_ref[0].astype(BF16), wb_ref[1], preferred_element_type=F32)
              + _sigmoid(gz2_ref[0])
              * jnp.dot(o_c.astype(BF16), wb_ref[2], preferred_element_type=F32))
    out = jnp.dot(merged.astype(BF16), wo_ref[...], preferred_element_type=F32)
    o_ref[0] = x_ref[0] + gt_ref[0] * out


def _merge(y, bonus, proj, o_a, o_b, x, gate, gn_w, gn_b, ones_bd, g_comb, w_branch, w_out, *, tm=512):
    b, s, d = x.shape
    gz_per = d // COLBLK
    tok = lambda width, cb: pl.BlockSpec((1, tm, width), lambda bi, i, cb=cb: (bi, i, cb))
    const = lambda shape: pl.BlockSpec(shape, lambda bi, i: (0,) * len(shape))
    return pl.pallas_call(
        _merge_kernel,
        grid=(b, s // tm),
        in_specs=[pl.BlockSpec((2, 1, tm, COLBLK), lambda bi, i: (0, bi, i, 0)),
                  tok(COLBLK, 0),
                  tok(COLBLK, CB_LORA),
                  tok(d, CB_GZ // gz_per), tok(d, CB_GZ // gz_per + 1), tok(d, CB_GZ // gz_per + 2),
                  tok(COLBLK, 0), tok(COLBLK, 0), tok(d, 0),
                  pl.BlockSpec((1, 1, d), lambda bi, i: (bi, 0, 0)),
                  const((1, COLBLK)), const((1, COLBLK)), const((COLBLK, COLBLK)),
                  const((COLBLK, COLBLK)), const((3, COLBLK, d)), const((d, d))],
        out_specs=pl.BlockSpec((1, tm, d), lambda bi, i: (bi, i, 0)),
        out_shape=jax.ShapeDtypeStruct((b, s, d), F32),
        compiler_params=_cparams(("parallel", "parallel")),
        name="merge_out",
    )(y, bonus, proj, proj, proj, proj, o_a, o_b, x, gate, gn_w, gn_b, ones_bd, g_comb, w_branch, w_out)


def _split_w_in(w_in):
    d = w_in.shape[0]
    n_lora = 4 * DECAY_LORA + GATE_LORA
    n_rest = N_ATTN + CB_LORA * COLBLK + n_lora
    rest, gates = w_in[:, N_ATTN:n_rest], w_in[:, n_rest:]
    zpad = jnp.zeros((d, COLBLK - n_lora), w_in.dtype)
    return w_in[:, :N_ATTN].astype(BF16), jnp.concatenate([rest, zpad, gates], axis=1).astype(BF16)


def _lora_params(mu_w, mu_a, w_up, a_up, g_up):
    r = DECAY_LORA
    mu_l = jnp.zeros((2, COLBLK), F32)
    w_comb = jnp.zeros((2, 2 * LANES, 2 * COLBLK), F32)
    for d in range(2):
        wcol, acol = d * r, 2 * r + d * r
        mu_l = mu_l.at[d, wcol:wcol + r].set(mu_w[d]).at[d, acol:acol + r].set(mu_a[d])
        w_comb = w_comb.at[d, wcol:wcol + r, :COLBLK].set(w_up[d])
        w_comb = w_comb.at[d, acol:acol + r, COLBLK:].set(a_up[d])
    g_comb = jnp.zeros((COLBLK, COLBLK), F32).at[4 * r:4 * r + GATE_LORA].set(g_up)
    return mu_l, w_comb.astype(BF16), g_comb.astype(BF16)


def _head_block_ones():
    seg = np.arange(COLBLK) // HEAD_DIM
    return jnp.asarray(seg[:, None] == seg[None, :], dtype=BF16)


def kernel(x, c, positions, ada_w, ada_b, norm_gains, ffn_wi, ffn_wo, w_in, rpb, mu_rkv, mu_w, mu_a,
           w0, w_up, a0, a_up, g_up, k_k, k_a, r_k, gn_w, gn_b, w_branch, w_out, final_norm):
    depth = ada_w.shape[0]
    b, s, d = x.shape
    assert d == 2 * COLBLK and s % 1024 == 0 and s % (GRID_W * NA_WIN_ROWS) == 0

    mod = _modulation(c, ada_w, ada_b)
    cos, s1, s2 = _rope_tables(positions)
    ones_bd = _head_block_ones()
    bias_tabs = _natten_bias_tables(rpb, s // GRID_W)
    ffn_wi_bf, ffn_wo_bf = ffn_wi.astype(BF16), ffn_wo.astype(BF16)
    w_branch_bf, w_out_bf = w_branch.astype(BF16), w_out.astype(BF16)
    row = lambda t: t.reshape(1, -1)

    for l in range(depth):
        m = mod[l].reshape(b, N_MOD, 1, d)
        sh1, sc1, gt1, sh2, sc2, gt2, sh3, sc3, gt3 = (m[:, i] for i in range(N_MOD))

        x = _ffn(x, norm_gains[l, 0], sh1, sc1, gt1, ffn_wi_bf[l, 0], ffn_wo_bf[l, 0])

        attn, proj = _in_proj(x, norm_gains[l, 1], sh2, sc2, *_split_w_in(w_in[l]))
        o_a = _dilated_attention(attn, cos, s1, s2)
        o_b = _neighborhood_attention(attn, bias_tabs[l])
        mu_l, w_comb, g_comb = _lora_params(mu_w[l], mu_a[l], w_up[l], a_up[l], g_up[l])
        *scan_in, bonus = _rwkv_pre(proj, mu_rkv[l], mu_l, w_comb, w0[l], a0[l],
                                    row(k_k[l]), row(k_a[l]), row(r_k[l]), ones_bd)
        y = _rwkv_scan(*(_to_chain_lanes(t) for t in scan_in))
        y = _from_chain_lanes(y, b)
        x = _merge(y, bonus, proj, o_a, o_b, x, gt2, row(gn_w[l]), row(gn_b[l]), ones_bd, g_comb,
                   w_branch_bf[l], w_out_bf[l])

        x = _ffn(x, norm_gains[l, 2], sh3, sc3, gt3, ffn_wi_bf[l, 1], ffn_wo_bf[l, 1],
                 final_norm if l == depth - 1 else None)
    return x
```

```python
import functools

import numpy as np
import jax
import jax.numpy as jnp
from jax import lax
from jax.experimental import pallas as pl
from jax.experimental.pallas import tpu as pltpu

F32 = jnp.float32
BF16 = jnp.bfloat16

HEAD_DIM = 64
N_HEADS = 8
W_MIX = N_HEADS * HEAD_DIM
DILATED_PAIRS = ((128, 1), (512, 4), (2048, 16))
QBLK = 128
ROT_DIM = HEAD_DIM // 4
ROPE_THETA = 500000.0
GRID_W = 64
NA_WIN_ROWS = 8
NA_WIN_COLS = 16
DECAY_LORA = 64
ICLR_LORA = 64
GATE_LORA = 128
DECAY_SCALE = 0.6065306597126334
N_MOD = 9
RMS_EPS = 1e-6
GN_EPS = 64e-5
NEG = -1e30

LANES = 128
SUBLANES = 8
VMEM_LIMIT = 48 * 1024 * 1024
DILATED_GROUP = 4
NATTEN_GROUP = 8

COLBLK = 512
CB_AQ, CB_AK, CB_AV = 0, 1, 2
CB_NQ, CB_NK, CB_NV = 3, 4, 5
N_ATTN = 6 * COLBLK
CB_PR, CB_PK, CB_PV = 0, 1, 2
CB_LORA = 3
CB_GZ = 4


def _cparams(sem):
    return pltpu.CompilerParams(dimension_semantics=sem, vmem_limit_bytes=VMEM_LIMIT)


def _sigmoid(x):
    return 0.5 * jnp.tanh(0.5 * x) + 0.5


def _mod_kernel(c_ref, w_ref, b_ref, o_ref):
    c = c_ref[...]
    cs = (c * _sigmoid(c)).astype(BF16)
    o_ref[0] = jnp.dot(cs, w_ref[0].astype(BF16), preferred_element_type=F32) + b_ref[0]


def _modulation(c, ada_w, ada_b):
    depth, d, nd = ada_w.shape
    b = c.shape[0]
    return pl.pallas_call(
        _mod_kernel,
        grid=(depth, nd // d),
        in_specs=[pl.BlockSpec((b, d), lambda l, j: (0, 0)),
                  pl.BlockSpec((1, d, d), lambda l, j: (l, 0, j)),
                  pl.BlockSpec((1, 1, d), lambda l, j: (l, 0, j))],
        out_specs=pl.BlockSpec((1, b, d), lambda l, j: (l, 0, j)),
        out_shape=jax.ShapeDtypeStruct((depth, b, nd), F32),
        compiler_params=_cparams(("parallel", "parallel")),
        name="adaln_mod",
    )(c, ada_w, ada_b.reshape(depth, 1, nd))


def _norm_mod(x, gain, shift, scale):
    ms = jnp.mean(x * x, axis=-1, keepdims=True)
    y = x * lax.rsqrt(ms + RMS_EPS) * gain
    return y * (1.0 + scale) + shift


def _ffn_kernel(*refs, final, tf):
    if final:
        x_ref, g_ref, sh_ref, sc_ref, gt_ref, wi_ref, wo_ref, fin_ref, o_ref = refs
    else:
        x_ref, g_ref, sh_ref, sc_ref, gt_ref, wi_ref, wo_ref, o_ref = refs
    x = x_ref[0]
    h = _norm_mod(x, g_ref[...], sh_ref[0], sc_ref[0]).astype(BF16)
    dff = wo_ref.shape[0]
    acc = None
    for lo in range(0, dff, tf):
        gate = jnp.dot(h, wi_ref[:, lo:lo + tf], preferred_element_type=F32)
        up = jnp.dot(h, wi_ref[:, dff + lo:dff + lo + tf], preferred_element_type=F32)
        act = ((gate * _sigmoid(gate)) * up).astype(BF16)
        part = jnp.dot(act, wo_ref[lo:lo + tf, :], preferred_element_type=F32)
        acc = part if acc is None else acc + part
    y = x + 0.5 * gt_ref[0] * acc
    if final:
        ms = jnp.mean(y * y, axis=-1, keepdims=True)
        y = y * lax.rsqrt(ms + RMS_EPS) * fin_ref[...]
    o_ref[0] = y


def _ffn(x, gain, shift, scale, gate, wi, wo, final_gain=None, *, tm=512, tf=256):
    b, s, d = x.shape
    dff = wo.shape[0]
    final = final_gain is not None
    vec = pl.BlockSpec((1, 1, d), lambda bi, i: (bi, 0, 0))
    resident = lambda shape: pl.BlockSpec(shape, lambda bi, i: (0, 0), pipeline_mode=pl.Buffered(1))
    in_specs = [pl.BlockSpec((1, tm, d), lambda bi, i: (bi, i, 0)),
                pl.BlockSpec((1, d), lambda bi, i: (0, 0)),
                vec, vec, vec,
                resident((d, 2 * dff)), resident((dff, d))]
    args = [x, gain.reshape(1, d), shift, scale, gate, wi, wo]
    if final:
        in_specs.append(pl.BlockSpec((1, d), lambda bi, i: (0, 0)))
        args.append(final_gain.reshape(1, d))
    return pl.pallas_call(
        functools.partial(_ffn_kernel, final=final, tf=tf),
        grid=(b, s // tm),
        in_specs=in_specs,
        out_specs=pl.BlockSpec((1, tm, d), lambda bi, i: (bi, i, 0)),
        out_shape=jax.ShapeDtypeStruct((b, s, d), F32),
        compiler_params=_cparams(("parallel", "parallel")),
        name="ffn_final" if final else "ffn",
    )(*args)


def _win_kernel(x_ref, g_ref, sh_ref, sc_ref, wa_ref, wr_ref, oa_ref, or_ref, h_scr):
    @pl.when(pl.program_id(2) == 0)
    def _():
        h = _norm_mod(x_ref[0], g_ref[...], sh_ref[0], sc_ref[0])
        h_scr[...] = h.astype(BF16)

    h = h_scr[...]
    oa_ref[0] = jnp.dot(h, wa_ref[...], preferred_element_type=F32).astype(BF16)
    or_ref[0] = jnp.dot(h, wr_ref[...], preferred_element_type=F32)


def _in_proj(x, gain, shift, scale, w_attn, w_rest, *, tm=1024, nj=4):
    b, s, d = x.shape
    ta, tr = w_attn.shape[1] // nj, w_rest.shape[1] // nj
    vec = pl.BlockSpec((1, 1, d), lambda bi, i, j: (bi, 0, 0))
    return pl.pallas_call(
        _win_kernel,
        grid=(b, s // tm, nj),
        in_specs=[pl.BlockSpec((1, tm, d), lambda bi, i, j: (bi, i, 0)),
                  pl.BlockSpec((1, d), lambda bi, i, j: (0, 0)),
                  vec, vec,
                  pl.BlockSpec((d, ta), lambda bi, i, j: (0, j)),
                  pl.BlockSpec((d, tr), lambda bi, i, j: (0, j))],
        out_specs=[pl.BlockSpec((1, tm, ta), lambda bi, i, j: (bi, i, j)),
                   pl.BlockSpec((1, tm, tr), lambda bi, i, j: (bi, i, j))],
        out_shape=[jax.ShapeDtypeStruct((b, s, w_attn.shape[1]), BF16),
                   jax.ShapeDtypeStruct((b, s, w_rest.shape[1]), F32)],
        scratch_shapes=[pltpu.VMEM((tm, d), BF16)],
        compiler_params=_cparams(("parallel", "parallel", "arbitrary")),
        name="in_proj",
    )(x, gain.reshape(1, d), shift, scale, w_attn, w_rest)


def _dilated_kernel(q_ref, k_ref, v_ref, cos_ref, s1_ref, s2_ref, *rest, seq, pad):
    n_pat = len(DILATED_PAIRS)
    mask_refs = rest[:n_pat]
    o_ref, q_scr, k_scr, v_scr, og_scr, lse_scr = rest[n_pat:]
    cos, s1, s2 = cos_ref[0], s1_ref[0], s2_ref[0]

    def rotary(t):
        return (t * cos + pltpu.roll(t, LANES - ROT_DIM // 2, axis=1) * s1
                + pltpu.roll(t, ROT_DIM // 2, axis=1) * s2)

    q_scr[...] = rotary(q_ref[0].astype(F32)) * (HEAD_DIM ** -0.5)
    zeros = jnp.zeros((pad, LANES), F32)
    k_scr[pl.ds(0, pad), :] = zeros
    k_scr[pl.ds(pad + seq, pad), :] = zeros
    v_scr[pl.ds(0, pad), :] = zeros
    v_scr[pl.ds(pad + seq, pad), :] = zeros
    k_scr[pl.ds(pad, seq), :] = rotary(k_ref[0].astype(F32))
    v_scr[pl.ds(pad, seq), :] = v_ref[0].astype(F32)

    lane = lax.broadcasted_iota(jnp.int32, (QBLK, LANES), 1)
    head0 = lane < HEAD_DIM

    for g, (window, dil) in enumerate(DILATED_PAIRS):
        nblk, lead, nkeys = _dilated_geometry(seq, window, dil)

        def group(gi, carry, g=g, dil=dil, lead=lead, nkeys=nkeys, nblk=nblk):
            units = []
            for u in range(DILATED_GROUP):
                idx = gi * DILATED_GROUP + u
                cls = idx // nblk
                n = idx % nblk
                q_start = cls + dil * QBLK * n
                k_start = pad + cls + dil * (QBLK * n - lead)
                if dil == 1:
                    q_rows = pl.ds(q_start, QBLK)
                    k_rows = pl.ds(k_start, nkeys)
                else:
                    q_rows = pl.ds(q_start, QBLK, stride=dil)
                    k_rows = pl.ds(k_start, nkeys, stride=dil)
                which = jnp.where(n == 0, 0, jnp.where(n == nblk - 1, 2, 1)) if nblk > 1 else 0
                units.append((q_rows, q_scr[q_rows, :], k_scr[k_rows, :].astype(BF16),
                              v_scr[k_rows, :].astype(BF16), which))
            scores = []
            for _, q, kw, _, which in units:
                for hmask in (head0, jnp.logical_not(head0)):
                    qh = jnp.where(hmask, q, 0.0).astype(BF16)
                    s = lax.dot_general(qh, kw, (((1,), (1,)), ((), ())),
                                        preferred_element_type=F32)
                    scores.append(s + mask_refs[g][which])
            probs, invs, lses = [], [], []
            for s in scores:
                m = jnp.max(s, axis=-1, keepdims=True)
                p = jnp.exp(s - m)
                den = jnp.sum(p, axis=-1, keepdims=True)
                probs.append(p.astype(BF16))
                invs.append(1.0 / den)
                lses.append(m + jnp.log(den))
            for u, (q_rows, _, _, vw, _) in enumerate(units):
                o0 = jnp.dot(probs[2 * u], vw, preferred_element_type=F32) * invs[2 * u]
                o1 = jnp.dot(probs[2 * u + 1], vw, preferred_element_type=F32) * invs[2 * u + 1]
                og_scr[g, q_rows, :] = jnp.where(head0, o0, o1)
                lse_scr[g, q_rows, :] = jnp.where(head0, lses[2 * u], lses[2 * u + 1])
            return carry

        lax.fori_loop(0, dil * nblk // DILATED_GROUP, group, 0)

    l0, l1, l2 = lse_scr[0], lse_scr[1], lse_scr[2]
    m = jnp.maximum(jnp.maximum(l0, l1), l2)
    e0, e1, e2 = jnp.exp(l0 - m), jnp.exp(l1 - m), jnp.exp(l2 - m)
    o_ref[0] = (e0 * og_scr[0] + e1 * og_scr[1] + e2 * og_scr[2]) * (1.0 / (e0 + e1 + e2))


def _dilated_geometry(seq, window, dil):
    radius = window // (2 * dil)
    nblk = seq // dil // QBLK
    lead = radius if nblk > 1 else 0
    return nblk, lead, QBLK + 2 * lead


def _dilated_masks(seq):
    out = []
    for window, dil in DILATED_PAIRS:
        radius = window // (2 * dil)
        nblk, lead, nkeys = _dilated_geometry(seq, window, dil)
        qi = np.arange(QBLK)[:, None]
        kj = np.arange(nkeys)[None, :]
        band = np.abs(kj - lead - qi) <= radius
        tabs = []
        for n in ((0, 1, nblk - 1) if nblk > 1 else (0,)):
            key_pos = QBLK * n + kj - lead
            tabs.append(band & (key_pos >= 0) & (key_pos < nblk * QBLK))
        out.append(jnp.asarray(np.where(np.stack(tabs), 0.0, NEG), F32))
    return out


def _dilated_attention(attn, cos, s1, s2, masks):
    b, s, _ = attn.shape
    pad = max((w // (2 * d)) * d for w, d in DILATED_PAIRS)
    hp = W_MIX // LANES
    per_cb = COLBLK // LANES
    col = lambda cb: pl.BlockSpec((1, s, LANES), lambda bi, h, cb=cb: (bi, 0, cb * per_cb + h))
    tab = pl.BlockSpec((1, s, LANES), lambda bi, h: (bi, 0, 0))
    const = lambda shape: pl.BlockSpec(shape, lambda bi, h: (0,) * len(shape))
    return pl.pallas_call(
        functools.partial(_dilated_kernel, seq=s, pad=pad),
        grid=(b, hp),
        in_specs=[col(CB_AQ), col(CB_AK), col(CB_AV), tab, tab, tab] + [const(m.shape) for m in masks],
        out_specs=pl.BlockSpec((1, s, LANES), lambda bi, h: (bi, 0, h)),
        out_shape=jax.ShapeDtypeStruct((b, s, W_MIX), F32),
        scratch_shapes=[pltpu.VMEM((s, LANES), F32),
                        pltpu.VMEM((s + 2 * pad, LANES), F32),
                        pltpu.VMEM((s + 2 * pad, LANES), F32),
                        pltpu.VMEM((len(DILATED_PAIRS), s, LANES), F32),
                        pltpu.VMEM((len(DILATED_PAIRS), s, LANES), F32)],
        compiler_params=_cparams(("parallel", "parallel")),
        name="dilated_attn",
    )(attn, attn, attn, cos, s1, s2, *masks)


def _rope_tables(positions):
    half = ROT_DIM // 2
    inv_freq = ROPE_THETA ** (-jnp.arange(half, dtype=F32) * 2.0 / ROT_DIM)
    ang = positions.astype(F32)[..., None] * inv_freq
    cos, sin = jnp.cos(ang), jnp.sin(ang)
    lead = ang.shape[:-1]
    ones = jnp.ones(lead + (HEAD_DIM - ROT_DIM,), F32)
    z = lambda n: jnp.zeros(lead + (n,), F32)
    c64 = jnp.concatenate([cos, cos, ones], axis=-1)
    s1 = jnp.concatenate([-sin, z(HEAD_DIM - half)], axis=-1)
    s2 = jnp.concatenate([z(half), sin, z(HEAD_DIM - ROT_DIM)], axis=-1)
    rep = lambda t: jnp.concatenate([t] * (LANES // HEAD_DIM), axis=-1)
    return rep(c64), rep(s1), rep(s2)


def _natten_kernel(q_ref, k_ref, v_ref, bias_ref, o_ref, *, rows, win_rows):
    lane = lax.broadcasted_iota(jnp.int32, (GRID_W, LANES), 1)
    head0 = lane < HEAD_DIM
    scale = HEAD_DIM ** -0.5
    nkeys = win_rows * GRID_W

    def group(gi, carry):
        units = []
        for u in range(NATTEN_GROUP):
            r = gi * NATTEN_GROUP + u
            r_start = jnp.clip(r - win_rows // 2, 0, rows - win_rows)
            q_rows = pl.ds(pl.multiple_of(r * GRID_W, GRID_W), GRID_W)
            k_rows = pl.ds(pl.multiple_of(r_start * GRID_W, GRID_W), nkeys)
            units.append((q_rows, r - r_start, q_ref[0, q_rows, :],
                          k_ref[0, k_rows, :], v_ref[0, k_rows, :]))
        scores = []
        for _, delta, q, kw, _ in units:
            q = q * scale
            zero = jnp.zeros_like(q)
            q2 = jnp.concatenate([jnp.where(head0, q, zero), jnp.where(head0, zero, q)], axis=0)
            s2 = lax.dot_general(q2, kw, (((1,), (1,)), ((), ())), preferred_element_type=F32)
            scores += [s2[:GRID_W] + bias_ref[0, delta], s2[GRID_W:] + bias_ref[1, delta]]
        probs, invs = [], []
        for s in scores:
            m = jnp.max(s, axis=-1, keepdims=True)
            p = jnp.exp(s - m)
            probs.append(p.astype(BF16))
            invs.append(1.0 / jnp.sum(p, axis=-1, keepdims=True))
        for u, (q_rows, _, _, _, vw) in enumerate(units):
            o2 = jnp.dot(jnp.concatenate(probs[2 * u:2 * u + 2], axis=0), vw,
                         preferred_element_type=F32)
            o_ref[0, q_rows, :] = jnp.where(head0, o2[:GRID_W] * invs[2 * u],
                                            o2[GRID_W:] * invs[2 * u + 1])
        return carry

    lax.fori_loop(0, rows // NATTEN_GROUP, group, 0)


def _natten_bias_tables(rpb, rows):
    wr = min(NA_WIN_ROWS, rows)
    wc = NA_WIN_COLS
    cols = np.arange(GRID_W)
    c_start = np.clip(cols - wc // 2, 0, GRID_W - wc)
    col_in = (cols[None, :] >= c_start[:, None]) & (cols[None, :] < c_start[:, None] + wc)
    coff = np.clip(cols[None, :] - cols[:, None], -(wc - 1), wc - 1) + wc - 1
    onehot = (coff[None] == np.arange(2 * wc - 1)[:, None, None]).astype(np.float32)
    by_col = jnp.einsum("lhrc,cqk->lhrqk", rpb.astype(F32), onehot, precision=lax.Precision.HIGHEST)
    by_col = jnp.where(col_in, by_col, NEG)
    top = NA_WIN_ROWS - 1
    tabs = jnp.stack([by_col[:, :, top - dl:top - dl + wr] for dl in range(wr)], axis=2)
    tabs = tabs.transpose(0, 1, 2, 4, 3, 5)
    return tabs.reshape(rpb.shape[0], rpb.shape[1], wr, GRID_W, wr * GRID_W)


def _neighborhood_attention(attn, bias_tab):
    b, s, _ = attn.shape
    rows = s // GRID_W
    wr = min(NA_WIN_ROWS, rows)
    hp = W_MIX // LANES
    per_cb = COLBLK // LANES
    col = lambda cb: pl.BlockSpec((1, s, LANES), lambda bi, h, cb=cb: (bi, 0, cb * per_cb + h))
    return pl.pallas_call(
        functools.partial(_natten_kernel, rows=rows, win_rows=wr),
        grid=(b, hp),
        in_specs=[col(CB_NQ), col(CB_NK), col(CB_NV),
                  pl.BlockSpec((LANES // HEAD_DIM, wr, GRID_W, wr * GRID_W),
                               lambda bi, h: (h, 0, 0, 0))],
        out_specs=pl.BlockSpec((1, s, LANES), lambda bi, h: (bi, 0, h)),
        out_shape=jax.ShapeDtypeStruct((b, s, W_MIX), F32),
        compiler_params=_cparams(("parallel", "parallel")),
        name="natten",
    )(attn, attn, attn, bias_tab)


def _seg_sum(x, ones_bd):
    return jnp.dot(x.astype(BF16), ones_bd, preferred_element_type=F32)


def _rwkv_pre_kernel(r_ref, k_ref, v_ref, l_ref,
                     rp_ref, kp_ref, vp_ref, lp_ref,
                     rn_ref, kn_ref, vn_ref, ln_ref,
                     mu_ref, mul_ref, wc_ref, w0_ref, a0_ref, kk_ref, ka_ref, rk_ref, ones_ref,
                     r_o, w_o, k_o, v_o, a_o, b_o, bonus_o, *, tm):
    i = pl.program_id(1)
    first = i == 0
    last = i == pl.num_programs(1) - 1
    row = lax.broadcasted_iota(jnp.int32, (tm, COLBLK), 0)
    ones_bd = ones_ref[...]

    def neighbours(x_ref, p_ref, n_ref):
        x = x_ref[0]
        prev_row = jnp.where(first, 0.0, p_ref[0, SUBLANES - 1:SUBLANES, :])
        next_row = jnp.where(last, 0.0, n_ref[0, 0:1, :])
        prev = jnp.where(row == 0, prev_row, pltpu.roll(x, 1, axis=0))
        nxt = jnp.where(row == tm - 1, next_row, pltpu.roll(x, tm - 1, axis=0))
        return x, (prev, nxt)

    pr, nb_r = neighbours(r_ref, rp_ref, rn_ref)
    pk, nb_k = neighbours(k_ref, kp_ref, kn_ref)
    pv, nb_v = neighbours(v_ref, vp_ref, vn_ref)
    pl_, nb_l = neighbours(l_ref, lp_ref, ln_ref)

    bonus = jnp.zeros((tm, COLBLK), F32)
    for d in range(2):
        r = pr + (nb_r[d] - pr) * mu_ref[d, 0:1, :]
        k = pk + (nb_k[d] - pk) * mu_ref[d, 1:2, :]
        v = pv + (nb_v[d] - pv) * mu_ref[d, 2:3, :]
        xl = pl_ + (nb_l[d] - pl_) * mul_ref[d:d + 1, :]
        zw = jnp.dot(jnp.tanh(xl[:, :LANES]).astype(BF16), wc_ref[d, :LANES, :],
                     preferred_element_type=F32)
        za = jnp.dot(xl[:, LANES:2 * LANES].astype(BF16), wc_ref[d, LANES:2 * LANES, :],
                     preferred_element_type=F32)
        z = zw + za
        wz = w0_ref[d:d + 1, :] + z[:, :COLBLK]
        az = a0_ref[d:d + 1, :] + z[:, COLBLK:]
        decay = jnp.exp(-DECAY_SCALE * _sigmoid(wz))
        a = _sigmoid(az)
        kk = k * kk_ref[...]
        kk = kk * lax.rsqrt(jnp.maximum(_seg_sum(kk * kk, ones_bd), 1e-24))
        k2 = k * (1.0 + (a - 1.0) * ka_ref[...])
        bonus = bonus + _seg_sum(r * k2 * rk_ref[...], ones_bd) * v
        r_o[d, 0] = r.astype(r_o.dtype)
        w_o[d, 0] = decay
        k_o[d, 0] = k2.astype(k_o.dtype)
        v_o[d, 0] = v
        a_o[d, 0] = -kk
        b_o[d, 0] = (kk * a).astype(b_o.dtype)
    bonus_o[0] = bonus


def _rwkv_pre(proj, mu_rkv, mu_lora, w_comb, w0, a0, k_k, k_a, r_k, ones_bd, *, tm=256):
    b, s, _ = proj.shape
    nsub = tm // SUBLANES
    last_sub = s // SUBLANES - 1
    main = lambda cb: pl.BlockSpec((1, tm, COLBLK), lambda bi, i, cb=cb: (bi, i, cb))
    prev = lambda cb: pl.BlockSpec(
        (1, SUBLANES, COLBLK), lambda bi, i, cb=cb: (bi, jnp.maximum(i * nsub - 1, 0), cb))
    nxt = lambda cb: pl.BlockSpec(
        (1, SUBLANES, COLBLK), lambda bi, i, cb=cb: (bi, jnp.minimum((i + 1) * nsub, last_sub), cb))
    cbs = (CB_PR, CB_PK, CB_PV, CB_LORA)
    const = lambda shape: pl.BlockSpec(shape, lambda bi, i: (0,) * len(shape))
    dir_out = pl.BlockSpec((2, 1, tm, COLBLK), lambda bi, i: (0, bi, i, 0))
    dir_shape = lambda dt: jax.ShapeDtypeStruct((2, b, s, COLBLK), dt)
    return pl.pallas_call(
        functools.partial(_rwkv_pre_kernel, tm=tm),
        grid=(b, s // tm),
        in_specs=([main(cb) for cb in cbs] + [prev(cb) for cb in cbs] + [nxt(cb) for cb in cbs]
                  + [const((2, 3, COLBLK)), const((2, COLBLK)), const((2, 2 * LANES, 2 * COLBLK)),
                     const((2, COLBLK)), const((2, COLBLK)), const((1, COLBLK)), const((1, COLBLK)),
                     const((1, COLBLK)), const((COLBLK, COLBLK))]),
        out_specs=[dir_out] * 6 + [pl.BlockSpec((1, tm, COLBLK), lambda bi, i: (bi, i, 0))],
        out_shape=([dir_shape(dt) for dt in (BF16, F32, BF16, F32, F32, BF16)]
                   + [jax.ShapeDtypeStruct((b, s, COLBLK), F32)]),
        compiler_params=_cparams(("parallel", "parallel")),
        name="rwkv_pre",
    )(*([proj] * 12), mu_rkv, mu_lora, w_comb, w0, a0, k_k, k_a, r_k, ones_bd)


def _rwkv_scan_kernel(r_ref, w_ref, k_ref, v_ref, a_ref, b_ref, an_ref, y_ref,
                      st_scr, sa_scr, rows_scr, *, tt):
    d = pl.program_id(0)
    n = HEAD_DIM

    @pl.when(pl.program_id(1) == 0)
    def _():
        st_scr[...] = jnp.zeros_like(st_scr)
        sa_scr[...] = jnp.zeros_like(sa_scr)

    pack = rows_scr.shape[-2]

    def row(slot, q, k):
        return jnp.broadcast_to(rows_scr[slot, q, k // pack, pl.ds(k % pack, 1), :],
                                sa_scr.shape[1:])

    def step(j, t, a_next, sa, gam):
        slot = j % 2
        gam = gam * w_ref[0, t]
        inv = 1.0 / gam
        v = v_ref[0, t]
        rows_scr[slot, 0] = b_ref[0, t].astype(F32) * inv
        rows_scr[slot, 1] = k_ref[0, t].astype(F32) * inv
        rows_scr[slot, 2] = r_ref[0, t].astype(F32) * gam
        rows_scr[slot, 3] = a_next * gam
        y = jnp.zeros_like(v)
        sa_next = [jnp.zeros_like(v), jnp.zeros_like(v)]
        for k in range(n):
            new = st_scr[k] + (sa * row(slot, 0, k) + v * row(slot, 1, k))
            st_scr[k] = new
            y = y + new * row(slot, 2, k)
            sa_next[k % 2] = sa_next[k % 2] + new * row(slot, 3, k)
        y_ref[0, t] = y
        return sa_next[0] + sa_next[1], gam

    def body(j, carry):
        t = j + d * (tt - 1 - 2 * j)
        return step(j, t, a_ref[0, t + 1 - 2 * d], *carry)

    carry = (sa_scr[...], jnp.ones(rows_scr.shape[2:], F32))
    carry = lax.fori_loop(0, tt - 1, body, carry)
    sa, gam = step(tt - 1, (1 - d) * (tt - 1), an_ref[0, 0], *carry)
    sa_scr[...] = sa
    rows_scr[0, 0] = gam
    for k in range(n):
        st_scr[k] = st_scr[k] * row(0, 0, k)


def _rwkv_scan(r, w, k, v, a, b, *, tt=32):
    _, s, n, c = v.shape
    nt = s // tt
    state_tile = (n // SUBLANES, SUBLANES, c)
    vec_tile = (n // (2 * SUBLANES), 2 * SUBLANES, c)
    chunk = lambda d, i: (d, i + d * (nt - 1 - 2 * i), 0, 0, 0)
    vec_spec = pl.BlockSpec((1, tt) + vec_tile, chunk)
    state_spec = pl.BlockSpec((1, tt) + state_tile, chunk)

    def next_first_step(d, i):
        fwd = jnp.minimum((i + 1) * tt, s - 1)
        bwd = jnp.maximum((nt - 1 - i) * tt - 1, 0)
        return (d, jnp.where(d == 0, fwd, bwd), 0, 0, 0)

    r, w, k, a, b = (t.reshape((2, s) + vec_tile) for t in (r, w, k, a, b))
    y = pl.pallas_call(
        functools.partial(_rwkv_scan_kernel, tt=tt),
        grid=(2, nt),
        in_specs=[vec_spec, vec_spec, vec_spec, state_spec, vec_spec, vec_spec,
                  pl.BlockSpec((1, 1) + vec_tile, next_first_step)],
        out_specs=state_spec,
        out_shape=jax.ShapeDtypeStruct((2, s) + state_tile, F32),
        scratch_shapes=[pltpu.VMEM((n,) + state_tile, F32), pltpu.VMEM(state_tile, F32),
                        pltpu.VMEM((2, 4) + vec_tile, F32)],
        compiler_params=_cparams(("parallel", "arbitrary")),
        name="rwkv_scan",
    )(r, w, k, v.reshape((2, s) + state_tile), a, b, a)
    return y.reshape(2, s, n, c)


def _to_chain_lanes(t):
    _, b, s, _ = t.shape
    t = t.reshape(2, b, s, N_HEADS, HEAD_DIM).transpose(0, 2, 4, 1, 3)
    return t.reshape(2, s, HEAD_DIM, b * N_HEADS)


def _from_chain_lanes(t, b):
    _, s, _, _ = t.shape
    t = t.reshape(2, s, HEAD_DIM, b, N_HEADS).transpose(0, 3, 1, 4, 2)
    return t.reshape(2, b, s, W_MIX)


def _merge_kernel(y_ref, bonus_ref, l_ref, gz0_ref, gz1_ref, gz2_ref, oa_ref, ob_ref, x_ref, gt_ref,
                  gnw_ref, gnb_ref, ones_ref, gc_ref, wb_ref, wo_ref, o_ref):
    ones_bd = ones_ref[...]
    inv_n = 1.0 / HEAD_DIM
    acc = bonus_ref[0]
    for d in range(2):
        y = y_ref[d, 0]
        mean = _seg_sum(y, ones_bd) * inv_n
        yc = y - mean
        var = _seg_sum(yc * yc, ones_bd) * inv_n
        acc = acc + (yc * lax.rsqrt(var + GN_EPS)) * gnw_ref[...] + gnb_ref[...]
    g = jnp.dot(_sigmoid(l_ref[0]).astype(BF16), gc_ref[...], preferred_element_type=F32)
    o_c = acc * g
    merged = (_sigmoid(gz0_ref[0])
              * jnp.dot(oa_ref[0].astype(BF16), wb_ref[0], preferred_element_type=F32)
              + _sigmoid(gz1_ref[0])
              * jnp.dot(ob_ref[0].astype(BF16), wb_ref[1], preferred_element_type=F32)
              + _sigmoid(gz2_ref[0])
              * jnp.dot(o_c.astype(BF16), wb_ref[2], preferred_element_type=F32))
    out = jnp.dot(merged.astype(BF16), wo_ref[...], preferred_element_type=F32)
    o_ref[0] = x_ref[0] + gt_ref[0] * out


def _merge(y, bonus, proj, o_a, o_b, x, gate, gn_w, gn_b, ones_bd, g_comb, w_branch, w_out, *, tm=512):
    b, s, d = x.shape
    gz_per = d // COLBLK
    tok = lambda width, cb: pl.BlockSpec((1, tm, width), lambda bi, i, cb=cb: (bi, i, cb))
    const = lambda shape: pl.BlockSpec(shape, lambda bi, i: (0,) * len(shape))
    return pl.pallas_call(
        _merge_kernel,
        grid=(b, s // tm),
        in_specs=[pl.BlockSpec((2, 1, tm, COLBLK), lambda bi, i: (0, bi, i, 0)),
                  tok(COLBLK, 0),
                  tok(COLBLK, CB_LORA),
                  tok(d, CB_GZ // gz_per), tok(d, CB_GZ // gz_per + 1), tok(d, CB_GZ // gz_per + 2),
                  tok(COLBLK, 0), tok(COLBLK, 0), tok(d, 0),
                  pl.BlockSpec((1, 1, d), lambda bi, i: (bi, 0, 0)),
                  const((1, COLBLK)), const((1, COLBLK)), const((COLBLK, COLBLK)),
                  const((COLBLK, COLBLK)), const((3, COLBLK, d)), const((d, d))],
        out_specs=pl.BlockSpec((1, tm, d), lambda bi, i: (bi, i, 0)),
        out_shape=jax.ShapeDtypeStruct((b, s, d), F32),
        compiler_params=_cparams(("parallel", "parallel")),
        name="merge_out",
    )(y, bonus, proj, proj, proj, proj, o_a, o_b, x, gate, gn_w, gn_b, ones_bd, g_comb, w_branch, w_out)


def _split_w_in(w_in):
    d = w_in.shape[0]
    n_lora = 4 * DECAY_LORA + GATE_LORA
    n_rest = N_ATTN + CB_LORA * COLBLK + n_lora
    rest, gates = w_in[:, N_ATTN:n_rest], w_in[:, n_rest:]
    zpad = jnp.zeros((d, COLBLK - n_lora), w_in.dtype)
    return w_in[:, :N_ATTN].astype(BF16), jnp.concatenate([rest, zpad, gates], axis=1).astype(BF16)


def _lora_params(mu_w, mu_a, w_up, a_up, g_up):
    r = DECAY_LORA
    mu_l = jnp.zeros((2, COLBLK), F32)
    w_comb = jnp.zeros((2, 2 * LANES, 2 * COLBLK), F32)
    for d in range(2):
        wcol, acol = d * r, 2 * r + d * r
        mu_l = mu_l.at[d, wcol:wcol + r].set(mu_w[d]).at[d, acol:acol + r].set(mu_a[d])
        w_comb = w_comb.at[d, wcol:wcol + r, :COLBLK].set(w_up[d])
        w_comb = w_comb.at[d, acol:acol + r, COLBLK:].set(a_up[d])
    g_comb = jnp.zeros((COLBLK, COLBLK), F32).at[4 * r:4 * r + GATE_LORA].set(g_up)
    return mu_l, w_comb.astype(BF16), g_comb.astype(BF16)


def _head_block_ones():
    seg = np.arange(COLBLK) // HEAD_DIM
    return jnp.asarray(seg[:, None] == seg[None, :], dtype=BF16)


def kernel(x, c, positions, ada_w, ada_b, norm_gains, ffn_wi, ffn_wo, w_in, rpb, mu_rkv, mu_w, mu_a,
           w0, w_up, a0, a_up, g_up, k_k, k_a, r_k, gn_w, gn_b, w_branch, w_out, final_norm):
    depth = ada_w.shape[0]
    b, s, d = x.shape
    assert d == 2 * COLBLK and s % 1024 == 0 and s % (GRID_W * NA_WIN_ROWS) == 0

    mod = _modulation(c, ada_w, ada_b)
    cos, s1, s2 = _rope_tables(positions)
    ones_bd = _head_block_ones()
    bias_tabs = _natten_bias_tables(rpb, s // GRID_W)
    dil_masks = _dilated_masks(s)
    ffn_wi_bf, ffn_wo_bf = ffn_wi.astype(BF16), ffn_wo.astype(BF16)
    w_branch_bf, w_out_bf = w_branch.astype(BF16), w_out.astype(BF16)
    row = lambda t: t.reshape(1, -1)

    for l in range(depth):
        m = mod[l].reshape(b, N_MOD, 1, d)
        sh1, sc1, gt1, sh2, sc2, gt2, sh3, sc3, gt3 = (m[:, i] for i in range(N_MOD))

        x = _ffn(x, norm_gains[l, 0], sh1, sc1, gt1, ffn_wi_bf[l, 0], ffn_wo_bf[l, 0])

        attn, proj = _in_proj(x, norm_gains[l, 1], sh2, sc2, *_split_w_in(w_in[l]))
        o_a = _dilated_attention(attn, cos, s1, s2, dil_masks)
        o_b = _neighborhood_attention(attn, bias_tabs[l])
        mu_l, w_comb, g_comb = _lora_params(mu_w[l], mu_a[l], w_up[l], a_up[l], g_up[l])
        *scan_in, bonus = _rwkv_pre(proj, mu_rkv[l], mu_l, w_comb, w0[l], a0[l],
                                    row(k_k[l]), row(k_a[l]), row(r_k[l]), ones_bd)
        y = _rwkv_scan(*(_to_chain_lanes(t) for t in scan_in))
        y = _from_chain_lanes(y, b)
        x = _merge(y, bonus, proj, o_a, o_b, x, gt2, row(gn_w[l]), row(gn_b[l]), ones_bd, g_comb,
                   w_branch_bf[l], w_out_bf[l])

        x = _ffn(x, norm_gains[l, 2], sh3, sc3, gt3, ffn_wi_bf[l, 1], ffn_wo_bf[l, 1],
                 final_norm if l == depth - 1 else None)
    return x
```

```python
import functools

import numpy as np
import jax
import jax.numpy as jnp
from jax import lax
from jax.experimental import pallas as pl
from jax.experimental.pallas import tpu as pltpu

F32 = jnp.float32
BF16 = jnp.bfloat16

HEAD_DIM = 64
N_HEADS = 8
W_MIX = N_HEADS * HEAD_DIM
DILATED_PAIRS = ((128, 1), (512, 4), (2048, 16))
QBLK = 128
ROT_DIM = HEAD_DIM // 4
ROPE_THETA = 500000.0
GRID_W = 64
NA_WIN_ROWS = 8
NA_WIN_COLS = 16
DECAY_LORA = 64
ICLR_LORA = 64
GATE_LORA = 128
DECAY_SCALE = 0.6065306597126334
N_MOD = 9
RMS_EPS = 1e-6
GN_EPS = 64e-5
NEG = -1e30

LANES = 128
SUBLANES = 8
VMEM_LIMIT = 48 * 1024 * 1024
DILATED_GROUP = 4
NATTEN_GROUP = 8

COLBLK = 512
CB_AQ, CB_AK, CB_AV = 0, 1, 2
CB_NQ, CB_NK, CB_NV = 3, 4, 5
N_ATTN = 6 * COLBLK
CB_PR, CB_PK, CB_PV = 0, 1, 2
CB_LORA = 3
CB_GZ = 4


def _cparams(sem):
    return pltpu.CompilerParams(dimension_semantics=sem, vmem_limit_bytes=VMEM_LIMIT)


def _sigmoid(x):
    return 0.5 * jnp.tanh(0.5 * x) + 0.5


def _mod_kernel(c_ref, w_ref, b_ref, o_ref):
    c = c_ref[...]
    cs = (c * _sigmoid(c)).astype(BF16)
    o_ref[0] = jnp.dot(cs, w_ref[0].astype(BF16), preferred_element_type=F32) + b_ref[0]


def _modulation(c, ada_w, ada_b):
    depth, d, nd = ada_w.shape
    b = c.shape[0]
    return pl.pallas_call(
        _mod_kernel,
        grid=(depth, nd // d),
        in_specs=[pl.BlockSpec((b, d), lambda l, j: (0, 0)),
                  pl.BlockSpec((1, d, d), lambda l, j: (l, 0, j)),
                  pl.BlockSpec((1, 1, d), lambda l, j: (l, 0, j))],
        out_specs=pl.BlockSpec((1, b, d), lambda l, j: (l, 0, j)),
        out_shape=jax.ShapeDtypeStruct((depth, b, nd), F32),
        compiler_params=_cparams(("parallel", "parallel")),
        name="adaln_mod",
    )(c, ada_w, ada_b.reshape(depth, 1, nd))


def _norm_mod(x, gain, shift, scale):
    ms = jnp.mean(x * x, axis=-1, keepdims=True)
    y = x * lax.rsqrt(ms + RMS_EPS) * gain
    return y * (1.0 + scale) + shift


def _ffn_kernel(*refs, final, tf):
    if final:
        x_ref, g_ref, sh_ref, sc_ref, gt_ref, wi_ref, wo_ref, fin_ref, o_ref = refs
    else:
        x_ref, g_ref, sh_ref, sc_ref, gt_ref, wi_ref, wo_ref, o_ref = refs
    x = x_ref[0]
    h = _norm_mod(x, g_ref[...], sh_ref[0], sc_ref[0]).astype(BF16)
    dff = wo_ref.shape[0]
    acc = None
    for lo in range(0, dff, tf):
        gate = jnp.dot(h, wi_ref[:, lo:lo + tf], preferred_element_type=F32)
        up = jnp.dot(h, wi_ref[:, dff + lo:dff + lo + tf], preferred_element_type=F32)
        act = ((gate * _sigmoid(gate)) * up).astype(BF16)
        part = jnp.dot(act, wo_ref[lo:lo + tf, :], preferred_element_type=F32)
        acc = part if acc is None else acc + part
    y = x + 0.5 * gt_ref[0] * acc
    if final:
        ms = jnp.mean(y * y, axis=-1, keepdims=True)
        y = y * lax.rsqrt(ms + RMS_EPS) * fin_ref[...]
    o_ref[0] = y


def _ffn(x, gain, shift, scale, gate, wi, wo, final_gain=None, *, tm=512, tf=256):
    b, s, d = x.shape
    dff = wo.shape[0]
    final = final_gain is not None
    vec = pl.BlockSpec((1, 1, d), lambda bi, i: (bi, 0, 0))
    resident = lambda shape: pl.BlockSpec(shape, lambda bi, i: (0, 0), pipeline_mode=pl.Buffered(1))
    in_specs = [pl.BlockSpec((1, tm, d), lambda bi, i: (bi, i, 0)),
                pl.BlockSpec((1, d), lambda bi, i: (0, 0)),
                vec, vec, vec,
                resident((d, 2 * dff)), resident((dff, d))]
    args = [x, gain.reshape(1, d), shift, scale, gate, wi, wo]
    if final:
        in_specs.append(pl.BlockSpec((1, d), lambda bi, i: (0, 0)))
        args.append(final_gain.reshape(1, d))
    return pl.pallas_call(
        functools.partial(_ffn_kernel, final=final, tf=tf),
        grid=(b, s // tm),
        in_specs=in_specs,
        out_specs=pl.BlockSpec((1, tm, d), lambda bi, i: (bi, i, 0)),
        out_shape=jax.ShapeDtypeStruct((b, s, d), F32),
        compiler_params=_cparams(("parallel", "parallel")),
        name="ffn_final" if final else "ffn",
    )(*args)


def _win_kernel(x_ref, g_ref, sh_ref, sc_ref, wa_ref, wr_ref, oa_ref, or_ref, h_scr):
    @pl.when(pl.program_id(2) == 0)
    def _():
        h = _norm_mod(x_ref[0], g_ref[...], sh_ref[0], sc_ref[0])
        h_scr[...] = h.astype(BF16)

    h = h_scr[...]
    oa_ref[0] = jnp.dot(h, wa_ref[...], preferred_element_type=F32).astype(BF16)
    or_ref[0] = jnp.dot(h, wr_ref[...], preferred_element_type=F32)


def _in_proj(x, gain, shift, scale, w_attn, w_rest, *, tm=1024, nj=4):
    b, s, d = x.shape
    ta, tr = w_attn.shape[1] // nj, w_rest.shape[1] // nj
    vec = pl.BlockSpec((1, 1, d), lambda bi, i, j: (bi, 0, 0))
    return pl.pallas_call(
        _win_kernel,
        grid=(b, s // tm, nj),
        in_specs=[pl.BlockSpec((1, tm, d), lambda bi, i, j: (bi, i, 0)),
                  pl.BlockSpec((1, d), lambda bi, i, j: (0, 0)),
                  vec, vec,
                  pl.BlockSpec((d, ta), lambda bi, i, j: (0, j)),
                  pl.BlockSpec((d, tr), lambda bi, i, j: (0, j))],
        out_specs=[pl.BlockSpec((1, tm, ta), lambda bi, i, j: (bi, i, j)),
                   pl.BlockSpec((1, tm, tr), lambda bi, i, j: (bi, i, j))],
        out_shape=[jax.ShapeDtypeStruct((b, s, w_attn.shape[1]), BF16),
                   jax.ShapeDtypeStruct((b, s, w_rest.shape[1]), F32)],
        scratch_shapes=[pltpu.VMEM((tm, d), BF16)],
        compiler_params=_cparams(("parallel", "parallel", "arbitrary")),
        name="in_proj",
    )(x, gain.reshape(1, d), shift, scale, w_attn, w_rest)


def _dilated_kernel(q_ref, k_ref, v_ref, cos_ref, s1_ref, s2_ref, *rest, seq, pad):
    n_pat = len(DILATED_PAIRS)
    mask_refs = rest[:n_pat]
    o_ref, q_scr, k_scr, v_scr, og_scr, lse_scr = rest[n_pat:]
    cos, s1, s2 = cos_ref[0], s1_ref[0], s2_ref[0]

    def rotary(t):
        return (t * cos + pltpu.roll(t, LANES - ROT_DIM // 2, axis=1) * s1
                + pltpu.roll(t, ROT_DIM // 2, axis=1) * s2)

    q_scr[...] = rotary(q_ref[0].astype(F32)) * (HEAD_DIM ** -0.5)
    zeros = jnp.zeros((pad, LANES), F32)
    k_scr[pl.ds(0, pad), :] = zeros
    k_scr[pl.ds(pad + seq, pad), :] = zeros
    v_scr[pl.ds(0, pad), :] = zeros
    v_scr[pl.ds(pad + seq, pad), :] = zeros
    k_scr[pl.ds(pad, seq), :] = rotary(k_ref[0].astype(F32))
    v_scr[pl.ds(pad, seq), :] = v_ref[0].astype(F32)

    lane = lax.broadcasted_iota(jnp.int32, (QBLK, LANES), 1)
    head0 = lane < HEAD_DIM

    for g, (window, dil) in enumerate(DILATED_PAIRS):
        nblk, lead, nkeys = _dilated_geometry(seq, window, dil)

        def group(gi, carry, g=g, dil=dil, lead=lead, nkeys=nkeys, nblk=nblk):
            units = []
            for u in range(DILATED_GROUP):
                idx = gi * DILATED_GROUP + u
                cls = idx // nblk
                n = idx % nblk
                q_start = cls + dil * QBLK * n
                k_start = pad + cls + dil * (QBLK * n - lead)
                if dil == 1:
                    q_rows = pl.ds(q_start, QBLK)
                    k_rows = pl.ds(k_start, nkeys)
                else:
                    q_rows = pl.ds(q_start, QBLK, stride=dil)
                    k_rows = pl.ds(k_start, nkeys, stride=dil)
                which = jnp.where(n == 0, 0, jnp.where(n == nblk - 1, 2, 1)) if nblk > 1 else 0
                units.append((q_rows, q_scr[q_rows, :], k_scr[k_rows, :].astype(BF16),
                              v_scr[k_rows, :].astype(BF16), which))
            scores = []
            for _, q, kw, _, which in units:
                for hmask in (head0, jnp.logical_not(head0)):
                    qh = jnp.where(hmask, q, 0.0).astype(BF16)
                    s = lax.dot_general(qh, kw, (((1,), (1,)), ((), ())),
                                        preferred_element_type=F32)
                    scores.append(s + mask_refs[g][which])
            probs, invs, lses = [], [], []
            for s in scores:
                m = jnp.max(s, axis=-1, keepdims=True)
                p = jnp.exp(s - m)
                den = jnp.sum(p, axis=-1, keepdims=True)
                probs.append(p.astype(BF16))
                invs.append(1.0 / den)
                lses.append(m + jnp.log(den))
            for u, (q_rows, _, _, vw, _) in enumerate(units):
                o0 = jnp.dot(probs[2 * u], vw, preferred_element_type=F32) * invs[2 * u]
                o1 = jnp.dot(probs[2 * u + 1], vw, preferred_element_type=F32) * invs[2 * u + 1]
                og_scr[g, q_rows, :] = jnp.where(head0, o0, o1)
                lse_scr[g, q_rows, :] = jnp.where(head0, lses[2 * u], lses[2 * u + 1])
            return carry

        lax.fori_loop(0, dil * nblk // DILATED_GROUP, group, 0)

    l0, l1, l2 = lse_scr[0], lse_scr[1], lse_scr[2]
    m = jnp.maximum(jnp.maximum(l0, l1), l2)
    e0, e1, e2 = jnp.exp(l0 - m), jnp.exp(l1 - m), jnp.exp(l2 - m)
    out = (e0 * og_scr[0] + e1 * og_scr[1] + e2 * og_scr[2]) * (1.0 / (e0 + e1 + e2))
    o_ref[0] = out.astype(o_ref.dtype)


def _dilated_geometry(seq, window, dil):
    radius = window // (2 * dil)
    nblk = seq // dil // QBLK
    lead = radius if nblk > 1 else 0
    return nblk, lead, QBLK + 2 * lead


def _dilated_masks(seq):
    out = []
    for window, dil in DILATED_PAIRS:
        radius = window // (2 * dil)
        nblk, lead, nkeys = _dilated_geometry(seq, window, dil)
        qi = np.arange(QBLK)[:, None]
        kj = np.arange(nkeys)[None, :]
        band = np.abs(kj - lead - qi) <= radius
        tabs = []
        for n in ((0, 1, nblk - 1) if nblk > 1 else (0,)):
            key_pos = QBLK * n + kj - lead
            tabs.append(band & (key_pos >= 0) & (key_pos < nblk * QBLK))
        out.append(jnp.asarray(np.where(np.stack(tabs), 0.0, NEG), F32))
    return out


def _dilated_attention(attn, cos, s1, s2, masks):
    b, s, _ = attn.shape
    pad = max((w // (2 * d)) * d for w, d in DILATED_PAIRS)
    hp = W_MIX // LANES
    per_cb = COLBLK // LANES
    col = lambda cb: pl.BlockSpec((1, s, LANES), lambda bi, h, cb=cb: (bi, 0, cb * per_cb + h))
    tab = pl.BlockSpec((1, s, LANES), lambda bi, h: (bi, 0, 0))
    const = lambda shape: pl.BlockSpec(shape, lambda bi, h: (0,) * len(shape))
    return pl.pallas_call(
        functools.partial(_dilated_kernel, seq=s, pad=pad),
        grid=(b, hp),
        in_specs=[col(CB_AQ), col(CB_AK), col(CB_AV), tab, tab, tab] + [const(m.shape) for m in masks],
        out_specs=pl.BlockSpec((1, s, LANES), lambda bi, h: (bi, 0, h)),
        out_shape=jax.ShapeDtypeStruct((b, s, W_MIX), BF16),
        scratch_shapes=[pltpu.VMEM((s, LANES), F32),
                        pltpu.VMEM((s + 2 * pad, LANES), F32),
                        pltpu.VMEM((s + 2 * pad, LANES), F32),
                        pltpu.VMEM((len(DILATED_PAIRS), s, LANES), F32),
                        pltpu.VMEM((len(DILATED_PAIRS), s, LANES), F32)],
        compiler_params=_cparams(("parallel", "parallel")),
        name="dilated_attn",
    )(attn, attn, attn, cos, s1, s2, *masks)


def _rope_tables(positions):
    half = ROT_DIM // 2
    lane = np.arange(LANES) % HEAD_DIM
    inv_freq = ROPE_THETA ** (-jnp.asarray(lane % half, F32) * 2.0 / ROT_DIM)
    ang = positions.astype(F32)[..., None] * inv_freq
    cos, sin = jnp.cos(ang), jnp.sin(ang)
    return (jnp.where(lane < ROT_DIM, cos, 1.0), jnp.where(lane < half, -sin, 0.0),
            jnp.where((lane >= half) & (lane < ROT_DIM), sin, 0.0))


def _natten_kernel(q_ref, k_ref, v_ref, bias_ref, o_ref, *, rows, win_rows):
    lane = lax.broadcasted_iota(jnp.int32, (GRID_W, LANES), 1)
    head0 = lane < HEAD_DIM
    scale = HEAD_DIM ** -0.5
    nkeys = win_rows * GRID_W

    def group(gi, carry):
        units = []
        for u in range(NATTEN_GROUP):
            r = gi * NATTEN_GROUP + u
            r_start = jnp.clip(r - win_rows // 2, 0, rows - win_rows)
            q_rows = pl.ds(pl.multiple_of(r * GRID_W, GRID_W), GRID_W)
            k_rows = pl.ds(pl.multiple_of(r_start * GRID_W, GRID_W), nkeys)
            units.append((q_rows, r - r_start, q_ref[0, q_rows, :],
                          k_ref[0, k_rows, :], v_ref[0, k_rows, :]))
        scores = []
        for _, delta, q, kw, _ in units:
            q = q * scale
            zero = jnp.zeros_like(q)
            q2 = jnp.concatenate([jnp.where(head0, q, zero), jnp.where(head0, zero, q)], axis=0)
            s2 = lax.dot_general(q2, kw, (((1,), (1,)), ((), ())), preferred_element_type=F32)
            scores += [s2[:GRID_W] + bias_ref[0, delta], s2[GRID_W:] + bias_ref[1, delta]]
        probs, invs = [], []
        for s in scores:
            m = jnp.max(s, axis=-1, keepdims=True)
            p = jnp.exp(s - m)
            probs.append(p.astype(BF16))
            invs.append(1.0 / jnp.sum(p, axis=-1, keepdims=True))
        for u, (q_rows, _, _, _, vw) in enumerate(units):
            o2 = jnp.dot(jnp.concatenate(probs[2 * u:2 * u + 2], axis=0), vw,
                         preferred_element_type=F32)
            o_ref[0, q_rows, :] = jnp.where(head0, o2[:GRID_W] * invs[2 * u],
                                            o2[GRID_W:] * invs[2 * u + 1]).astype(o_ref.dtype)
        return carry

    lax.fori_loop(0, rows // NATTEN_GROUP, group, 0)


def _natten_bias_tables(rpb, rows):
    wr = min(NA_WIN_ROWS, rows)
    wc = NA_WIN_COLS
    cols = np.arange(GRID_W)
    c_start = np.clip(cols - wc // 2, 0, GRID_W - wc)
    col_in = (cols[None, :] >= c_start[:, None]) & (cols[None, :] < c_start[:, None] + wc)
    coff = np.clip(cols[None, :] - cols[:, None], -(wc - 1), wc - 1) + wc - 1
    onehot = (coff[None] == np.arange(2 * wc - 1)[:, None, None]).astype(np.float32)
    by_col = jnp.einsum("lhrc,cqk->lhqrk", rpb.astype(F32), onehot, precision=lax.Precision.HIGHEST)
    by_col = jnp.where(col_in[:, None, :], by_col, NEG)
    top = NA_WIN_ROWS - 1
    tabs = jnp.stack([by_col[:, :, :, top - dl:top - dl + wr] for dl in range(wr)], axis=2)
    return tabs.reshape(rpb.shape[0], rpb.shape[1], wr, GRID_W, wr * GRID_W)


def _neighborhood_attention(attn, bias_tab):
    b, s, _ = attn.shape
    rows = s // GRID_W
    wr = min(NA_WIN_ROWS, rows)
    hp = W_MIX // LANES
    per_cb = COLBLK // LANES
    col = lambda cb: pl.BlockSpec((1, s, LANES), lambda bi, h, cb=cb: (bi, 0, cb * per_cb + h))
    return pl.pallas_call(
        functools.partial(_natten_kernel, rows=rows, win_rows=wr),
        grid=(b, hp),
        in_specs=[col(CB_NQ), col(CB_NK), col(CB_NV),
                  pl.BlockSpec((LANES // HEAD_DIM, wr, GRID_W, wr * GRID_W),
                               lambda bi, h: (h, 0, 0, 0))],
        out_specs=pl.BlockSpec((1, s, LANES), lambda bi, h: (bi, 0, h)),
        out_shape=jax.ShapeDtypeStruct((b, s, W_MIX), BF16),
        compiler_params=_cparams(("parallel", "parallel")),
        name="natten",
    )(attn, attn, attn, bias_tab)


def _seg_sum(x, ones_bd):
    return jnp.dot(x.astype(BF16), ones_bd, preferred_element_type=F32)


def _rwkv_pre_kernel(r_ref, k_ref, v_ref, l_ref,
                     rp_ref, kp_ref, vp_ref, lp_ref,
                     rn_ref, kn_ref, vn_ref, ln_ref,
                     mu_ref, mul_ref, wc_ref, w0_ref, a0_ref, kk_ref, ka_ref, rk_ref, ones_ref,
                     w_o, k_o, a_o, b_o, bonus_o, *, tm):
    i = pl.program_id(1)
    first = i == 0
    last = i == pl.num_programs(1) - 1
    row = lax.broadcasted_iota(jnp.int32, (tm, COLBLK), 0)
    ones_bd = ones_ref[...]

    def neighbours(x_ref, p_ref, n_ref):
        x = x_ref[0]
        prev_row = jnp.where(first, 0.0, p_ref[0, SUBLANES - 1:SUBLANES, :])
        next_row = jnp.where(last, 0.0, n_ref[0, 0:1, :])
        prev = jnp.where(row == 0, prev_row, pltpu.roll(x, 1, axis=0))
        nxt = jnp.where(row == tm - 1, next_row, pltpu.roll(x, tm - 1, axis=0))
        return x, (prev, nxt)

    pr, nb_r = neighbours(r_ref, rp_ref, rn_ref)
    pk, nb_k = neighbours(k_ref, kp_ref, kn_ref)
    pv, nb_v = neighbours(v_ref, vp_ref, vn_ref)
    pl_, nb_l = neighbours(l_ref, lp_ref, ln_ref)

    bonus = jnp.zeros((tm, COLBLK), F32)
    for d in range(2):
        r = pr + (nb_r[d] - pr) * mu_ref[d, 0:1, :]
        k = pk + (nb_k[d] - pk) * mu_ref[d, 1:2, :]
        v = pv + (nb_v[d] - pv) * mu_ref[d, 2:3, :]
        xl = pl_ + (nb_l[d] - pl_) * mul_ref[d:d + 1, :]
        zw = jnp.dot(jnp.tanh(xl[:, :LANES]).astype(BF16), wc_ref[d, :LANES, :],
                     preferred_element_type=F32)
        za = jnp.dot(xl[:, LANES:2 * LANES].astype(BF16), wc_ref[d, LANES:2 * LANES, :],
                     preferred_element_type=F32)
        z = zw + za
        wz = w0_ref[d:d + 1, :] + z[:, :COLBLK]
        az = a0_ref[d:d + 1, :] + z[:, COLBLK:]
        decay = jnp.exp(-DECAY_SCALE * _sigmoid(wz))
        a = _sigmoid(az)
        kk = k * kk_ref[...]
        kk = kk * lax.rsqrt(jnp.maximum(_seg_sum(kk * kk, ones_bd), 1e-24))
        k2 = k * (1.0 + (a - 1.0) * ka_ref[...])
        bonus = bonus + _seg_sum(r * k2 * rk_ref[...], ones_bd) * v
        w_o[d, 0] = decay
        k_o[d, 0] = k2.astype(k_o.dtype)
        a_o[d, 0] = -kk
        b_o[d, 0] = (kk * a).astype(b_o.dtype)
    bonus_o[0] = bonus


def _rwkv_pre(proj, mu_rkv, mu_lora, w_comb, w0, a0, k_k, k_a, r_k, ones_bd, *, tm=256):
    b, s, _ = proj.shape
    nsub = tm // SUBLANES
    last_sub = s // SUBLANES - 1
    main = lambda cb: pl.BlockSpec((1, tm, COLBLK), lambda bi, i, cb=cb: (bi, i, cb))
    prev = lambda cb: pl.BlockSpec(
        (1, SUBLANES, COLBLK), lambda bi, i, cb=cb: (bi, jnp.maximum(i * nsub - 1, 0), cb))
    nxt = lambda cb: pl.BlockSpec(
        (1, SUBLANES, COLBLK), lambda bi, i, cb=cb: (bi, jnp.minimum((i + 1) * nsub, last_sub), cb))
    cbs = (CB_PR, CB_PK, CB_PV, CB_LORA)
    const = lambda shape: pl.BlockSpec(shape, lambda bi, i: (0,) * len(shape))
    dir_out = pl.BlockSpec((2, 1, tm, COLBLK), lambda bi, i: (0, bi, i, 0))
    dir_shape = lambda dt: jax.ShapeDtypeStruct((2, b, s, COLBLK), dt)
    return pl.pallas_call(
        functools.partial(_rwkv_pre_kernel, tm=tm),
        grid=(b, s // tm),
        in_specs=([main(cb) for cb in cbs] + [prev(cb) for cb in cbs] + [nxt(cb) for cb in cbs]
                  + [const((2, 3, COLBLK)), const((2, COLBLK)), const((2, 2 * LANES, 2 * COLBLK)),
                     const((2, COLBLK)), const((2, COLBLK)), const((1, COLBLK)), const((1, COLBLK)),
                     const((1, COLBLK)), const((COLBLK, COLBLK))]),
        out_specs=[dir_out] * 4 + [pl.BlockSpec((1, tm, COLBLK), lambda bi, i: (bi, i, 0))],
        out_shape=([dir_shape(dt) for dt in (F32, BF16, F32, BF16)]
                   + [jax.ShapeDtypeStruct((b, s, COLBLK), F32)]),
        compiler_params=_cparams(("parallel", "parallel")),
        name="rwkv_pre",
    )(*([proj] * 12), mu_rkv, mu_lora, w_comb, w0, a0, k_k, k_a, r_k, ones_bd)


def _rwkv_scan_kernel(pr_ref, w_ref, k_ref, pv_ref, a_ref, b_ref, an_ref, mur_ref, muv_ref, y_ref,
                      st_scr, sa_scr, rows_scr, prevr_scr, prevv_scr, *, tt):
    d = pl.program_id(0)
    n = HEAD_DIM

    @pl.when(pl.program_id(1) == 0)
    def _():
        st_scr[...] = jnp.zeros_like(st_scr)
        sa_scr[...] = jnp.zeros_like(sa_scr)
        prevr_scr[...] = jnp.zeros_like(prevr_scr)
        prevv_scr[...] = jnp.zeros_like(prevv_scr)

    pack = rows_scr.shape[-2]
    mu_r, mu_v = mur_ref[0], muv_ref[0]

    def row(slot, q, k):
        return jnp.broadcast_to(rows_scr[slot, q, k // pack, pl.ds(k % pack, 1), :],
                                sa_scr.shape[1:])

    def step(j, t, a_next, sa, gam, pr_prev, pv_prev):
        slot = j % 2
        gam = gam * w_ref[0, t]
        inv = 1.0 / gam
        pr, pv = pr_ref[t].astype(F32), pv_ref[t]
        r = pr + (pr_prev - pr) * mu_r
        v = pv + (pv_prev - pv) * mu_v
        rows_scr[slot, 0] = b_ref[0, t].astype(F32) * inv
        rows_scr[slot, 1] = k_ref[0, t].astype(F32) * inv
        rows_scr[slot, 2] = r * gam
        rows_scr[slot, 3] = a_next * gam
        y = jnp.zeros_like(v)
        sa_next = [jnp.zeros_like(v), jnp.zeros_like(v)]
        for k in range(n):
            new = st_scr[k] + (sa * row(slot, 0, k) + v * row(slot, 1, k))
            st_scr[k] = new
            y = y + new * row(slot, 2, k)
            sa_next[k % 2] = sa_next[k % 2] + new * row(slot, 3, k)
        y_ref[0, t] = y
        return sa_next[0] + sa_next[1], gam, pr, pv

    def body(j, carry):
        t = j + d * (tt - 1 - 2 * j)
        return step(j, t, a_ref[0, t + 1 - 2 * d], *carry)

    carry = (sa_scr[...], jnp.ones(rows_scr.shape[2:], F32), prevr_scr[...], prevv_scr[...])
    carry = lax.fori_loop(0, tt - 1, body, carry)
    sa, gam, pr, pv = step(tt - 1, (1 - d) * (tt - 1), an_ref[0, 0], *carry)
    sa_scr[...] = sa
    prevr_scr[...] = pr
    prevv_scr[...] = pv
    rows_scr[0, 0] = gam
    for k in range(n):
        st_scr[k] = st_scr[k] * row(0, 0, k)


def _rwkv_scan(pr, w, k, pv, a, b, mu_r, mu_v, *, tt=32):
    s, n, c = pv.shape
    nt = s // tt
    state_tile = (n // SUBLANES, SUBLANES, c)
    vec_tile = (n // (2 * SUBLANES), 2 * SUBLANES, c)
    chunk = lambda d, i: (d, i + d * (nt - 1 - 2 * i), 0, 0, 0)
    shared = lambda d, i: (i + d * (nt - 1 - 2 * i), 0, 0, 0)
    vec_spec = pl.BlockSpec((1, tt) + vec_tile, chunk)
    state_spec = pl.BlockSpec((1, tt) + state_tile, chunk)

    def next_first_step(d, i):
        fwd = jnp.minimum((i + 1) * tt, s - 1)
        bwd = jnp.maximum((nt - 1 - i) * tt - 1, 0)
        return (d, jnp.where(d == 0, fwd, bwd), 0, 0, 0)

    w, k, a, b = (t.reshape((2, s) + vec_tile) for t in (w, k, a, b))
    y = pl.pallas_call(
        functools.partial(_rwkv_scan_kernel, tt=tt),
        grid=(2, nt),
        in_specs=[pl.BlockSpec((tt,) + vec_tile, shared), vec_spec, vec_spec,
                  pl.BlockSpec((tt,) + state_tile, shared), vec_spec, vec_spec,
                  pl.BlockSpec((1, 1) + vec_tile, next_first_step),
                  pl.BlockSpec((1,) + vec_tile, lambda d, i: (d, 0, 0, 0)),
                  pl.BlockSpec((1,) + state_tile, lambda d, i: (d, 0, 0, 0))],
        out_specs=state_spec,
        out_shape=jax.ShapeDtypeStruct((2, s) + state_tile, F32),
        scratch_shapes=[pltpu.VMEM((n,) + state_tile, F32), pltpu.VMEM(state_tile, F32),
                        pltpu.VMEM((2, 4) + vec_tile, F32),
                        pltpu.VMEM(vec_tile, F32), pltpu.VMEM(state_tile, F32)],
        compiler_params=_cparams(("parallel", "arbitrary")),
        name="rwkv_scan",
    )(pr.reshape((s,) + vec_tile), w, k, pv.reshape((s,) + state_tile), a, b, a,
      mu_r.reshape((2,) + vec_tile), mu_v.reshape((2,) + state_tile))
    return y.reshape(2, s, n, c)


def _to_chain_lanes(t):
    *lead, b, s, _ = t.shape
    k = len(lead)
    t = t.reshape(*lead, b, s, N_HEADS, HEAD_DIM)
    t = t.transpose(*range(k), k + 1, k + 3, k, k + 2)
    return t.reshape(*lead, s, HEAD_DIM, b * N_HEADS)


def _param_to_chain_lanes(p, b):
    lead = p.shape[:-1]
    p = jnp.swapaxes(p.reshape(*lead, N_HEADS, HEAD_DIM), -1, -2)
    return jnp.broadcast_to(p[..., None, :], lead + (HEAD_DIM, b, N_HEADS)).reshape(
        *lead, HEAD_DIM, b * N_HEADS)


def _from_chain_lanes(t, b):
    _, s, _, _ = t.shape
    t = t.reshape(2, s, HEAD_DIM, b, N_HEADS).transpose(0, 3, 1, 4, 2)
    return t.reshape(2, b, s, W_MIX)


def _merge_kernel(y_ref, bonus_ref, l_ref, gz0_ref, gz1_ref, gz2_ref, oa_ref, ob_ref, x_ref, gt_ref,
                  gnw_ref, gnb_ref, ones_ref, gc_ref, wb_ref, wo_ref, o_ref):
    ones_bd = ones_ref[...]
    inv_n = 1.0 / HEAD_DIM
    acc = bonus_ref[0]
    for d in range(2):
        y = y_ref[d, 0]
        mean = _seg_sum(y, ones_bd) * inv_n
        yc = y - mean
        var = _seg_sum(yc * yc, ones_bd) * inv_n
        acc = acc + (yc * lax.rsqrt(var + GN_EPS)) * gnw_ref[...] + gnb_ref[...]
    g = jnp.dot(_sigmoid(l_ref[0]).astype(BF16), gc_ref[...], preferred_element_type=F32)
    o_c = acc * g
    merged = (_sigmoid(gz0_ref[0])
              * jnp.dot(oa_ref[0], wb_ref[0], preferred_element_type=F32)
              + _sigmoid(gz1_ref[0])
              * jnp.dot(ob_ref[0], wb_ref[1], preferred_element_type=F32)
              + _sigmoid(gz2_ref[0])
              * jnp.dot(o_c.astype(BF16), wb_ref[2], preferred_element_type=F32))
    out = jnp.dot(merged.astype(BF16), wo_ref[...], preferred_element_type=F32)
    o_ref[0] = x_ref[0] + gt_ref[0] * out


def _merge(y, bonus, proj, o_a, o_b, x, gate, gn_w, gn_b, ones_bd, g_comb, w_branch, w_out, *, tm=512):
    b, s, d = x.shape
    gz_per = d // COLBLK
    tok = lambda width, cb: pl.BlockSpec((1, tm, width), lambda bi, i, cb=cb: (bi, i, cb))
    const = lambda shape: pl.BlockSpec(shape, lambda bi, i: (0,) * len(shape))
    return pl.pallas_call(
        _merge_kernel,
        grid=(b, s // tm),
        in_specs=[pl.BlockSpec((2, 1, tm, COLBLK), lambda bi, i: (0, bi, i, 0)),
                  tok(COLBLK, 0),
                  tok(COLBLK, CB_LORA),
                  tok(d, CB_GZ // gz_per), tok(d, CB_GZ // gz_per + 1), tok(d, CB_GZ // gz_per + 2),
                  tok(COLBLK, 0), tok(COLBLK, 0), tok(d, 0),
                  pl.BlockSpec((1, 1, d), lambda bi, i: (bi, 0, 0)),
                  const((1, COLBLK)), const((1, COLBLK)), const((COLBLK, COLBLK)),
                  const((COLBLK, COLBLK)), const((3, COLBLK, d)), const((d, d))],
        out_specs=pl.BlockSpec((1, tm, d), lambda bi, i: (bi, i, 0)),
        out_shape=jax.ShapeDtypeStruct((b, s, d), F32),
        compiler_params=_cparams(("parallel", "parallel")),
        name="merge_out",
    )(y, bonus, proj, proj, proj, proj, o_a, o_b, x, gate, gn_w, gn_b, ones_bd, g_comb, w_branch, w_out)


def _split_w_in(w_in):
    d = w_in.shape[0]
    n_lora = 4 * DECAY_LORA + GATE_LORA
    n_rest = N_ATTN + CB_LORA * COLBLK + n_lora
    rest, gates = w_in[:, N_ATTN:n_rest], w_in[:, n_rest:]
    zpad = jnp.zeros((d, COLBLK - n_lora), w_in.dtype)
    return w_in[:, :N_ATTN].astype(BF16), jnp.concatenate([rest, zpad, gates], axis=1).astype(BF16)


def _lora_params(mu_w, mu_a, w_up, a_up, g_up):
    r = DECAY_LORA
    mu_l = jnp.zeros((2, COLBLK), F32)
    w_comb = jnp.zeros((2, 2 * LANES, 2 * COLBLK), F32)
    for d in range(2):
        wcol, acol = d * r, 2 * r + d * r
        mu_l = mu_l.at[d, wcol:wcol + r].set(mu_w[d]).at[d, acol:acol + r].set(mu_a[d])
        w_comb = w_comb.at[d, wcol:wcol + r, :COLBLK].set(w_up[d])
        w_comb = w_comb.at[d, acol:acol + r, COLBLK:].set(a_up[d])
    g_comb = jnp.zeros((COLBLK, COLBLK), F32).at[4 * r:4 * r + GATE_LORA].set(g_up)
    return mu_l, w_comb.astype(BF16), g_comb.astype(BF16)


def _head_block_ones():
    seg = np.arange(COLBLK) // HEAD_DIM
    return jnp.asarray(seg[:, None] == seg[None, :], dtype=BF16)


def kernel(x, c, positions, ada_w, ada_b, norm_gains, ffn_wi, ffn_wo, w_in, rpb, mu_rkv, mu_w, mu_a,
           w0, w_up, a0, a_up, g_up, k_k, k_a, r_k, gn_w, gn_b, w_branch, w_out, final_norm):
    depth = ada_w.shape[0]
    b, s, d = x.shape
    assert d == 2 * COLBLK and s % 1024 == 0 and s % (GRID_W * NA_WIN_ROWS) == 0

    mod = _modulation(c, ada_w, ada_b)
    cos, s1, s2 = _rope_tables(positions)
    ones_bd = _head_block_ones()
    bias_tabs = _natten_bias_tables(rpb, s // GRID_W)
    dil_masks = _dilated_masks(s)
    ffn_wi_bf, ffn_wo_bf = ffn_wi.astype(BF16), ffn_wo.astype(BF16)
    w_branch_bf, w_out_bf = w_branch.astype(BF16), w_out.astype(BF16)
    row = lambda t: t.reshape(1, -1)

    for l in range(depth):
        m = mod[l].reshape(b, N_MOD, 1, d)
        sh1, sc1, gt1, sh2, sc2, gt2, sh3, sc3, gt3 = (m[:, i] for i in range(N_MOD))

        x = _ffn(x, norm_gains[l, 0], sh1, sc1, gt1, ffn_wi_bf[l, 0], ffn_wo_bf[l, 0])

        attn, proj = _in_proj(x, norm_gains[l, 1], sh2, sc2, *_split_w_in(w_in[l]))
        o_a = _dilated_attention(attn, cos, s1, s2, dil_masks)
        o_b = _neighborhood_attention(attn, bias_tabs[l])
        mu_l, w_comb, g_comb = _lora_params(mu_w[l], mu_a[l], w_up[l], a_up[l], g_up[l])
        decay, k2, a_vec, b_vec, bonus = _rwkv_pre(proj, mu_rkv[l], mu_l, w_comb, w0[l], a0[l],
                                                   row(k_k[l]), row(k_a[l]), row(r_k[l]), ones_bd)
        pr = proj[..., CB_PR * COLBLK:(CB_PR + 1) * COLBLK].astype(BF16)
        pv = proj[..., CB_PV * COLBLK:(CB_PV + 1) * COLBLK]
        y = _rwkv_scan(_to_chain_lanes(pr), _to_chain_lanes(decay), _to_chain_lanes(k2),
                       _to_chain_lanes(pv), _to_chain_lanes(a_vec), _to_chain_lanes(b_vec),
                       _param_to_chain_lanes(mu_rkv[l, :, 0], b), _param_to_chain_lanes(mu_rkv[l, :, 2], b))
        y = _from_chain_lanes(y, b)
        x = _merge(y, bonus, proj, o_a, o_b, x, gt2, row(gn_w[l]), row(gn_b[l]), ones_bd, g_comb,
                   w_branch_bf[l], w_out_bf[l])

        x = _ffn(x, norm_gains[l, 2], sh3, sc3, gt3, ffn_wi_bf[l, 1], ffn_wo_bf[l, 1],
                 final_norm if l == depth - 1 else None)
    return x
```

```python
import functools

import numpy as np
import jax
import jax.numpy as jnp
from jax import lax
from jax.experimental import pallas as pl
from jax.experimental.pallas import tpu as pltpu

F32 = jnp.float32
BF16 = jnp.bfloat16

HEAD_DIM = 64
N_HEADS = 8
W_MIX = N_HEADS * HEAD_DIM
DILATED_PAIRS = ((128, 1), (512, 4), (2048, 16))
QBLK = 128
ROT_DIM = HEAD_DIM // 4
ROPE_THETA = 500000.0
GRID_W = 64
NA_WIN_ROWS = 8
NA_WIN_COLS = 16
DECAY_LORA = 64
ICLR_LORA = 64
GATE_LORA = 128
DECAY_SCALE = 0.6065306597126334
N_MOD = 9
RMS_EPS = 1e-6
GN_EPS = 64e-5
NEG = -1e30

LANES = 128
SUBLANES = 8
VMEM_LIMIT = 48 * 1024 * 1024
DILATED_GROUP = 4
NATTEN_GROUP = 8

COLBLK = 512
CB_AQ, CB_AK, CB_AV = 0, 1, 2
CB_NQ, CB_NK, CB_NV = 3, 4, 5
N_ATTN = 6 * COLBLK
CB_PR, CB_PK, CB_PV = 0, 1, 2
CB_LORA = 3
CB_GZ = 4


def _cparams(sem):
    return pltpu.CompilerParams(dimension_semantics=sem, vmem_limit_bytes=VMEM_LIMIT)


def _sigmoid(x):
    return 0.5 * jnp.tanh(0.5 * x) + 0.5


def _mod_kernel(c_ref, w_ref, b_ref, o_ref):
    c = c_ref[...]
    cs = (c * _sigmoid(c)).astype(BF16)
    o_ref[0] = jnp.dot(cs, w_ref[0].astype(BF16), preferred_element_type=F32) + b_ref[0]


def _modulation(c, ada_w, ada_b):
    depth, d, nd = ada_w.shape
    b = c.shape[0]
    return pl.pallas_call(
        _mod_kernel,
        grid=(depth, nd // d),
        in_specs=[pl.BlockSpec((b, d), lambda l, j: (0, 0)),
                  pl.BlockSpec((1, d, d), lambda l, j: (l, 0, j)),
                  pl.BlockSpec((1, 1, d), lambda l, j: (l, 0, j))],
        out_specs=pl.BlockSpec((1, b, d), lambda l, j: (l, 0, j)),
        out_shape=jax.ShapeDtypeStruct((depth, b, nd), F32),
        compiler_params=_cparams(("parallel", "parallel")),
        name="adaln_mod",
    )(c, ada_w, ada_b.reshape(depth, 1, nd))


def _norm_mod(x, gain, shift, scale):
    ms = jnp.mean(x * x, axis=-1, keepdims=True)
    y = x * lax.rsqrt(ms + RMS_EPS) * gain
    return y * (1.0 + scale) + shift


def _ffn_kernel(*refs, final, tf):
    if final:
        x_ref, g_ref, sh_ref, sc_ref, gt_ref, wi_ref, wo_ref, fin_ref, o_ref = refs
    else:
        x_ref, g_ref, sh_ref, sc_ref, gt_ref, wi_ref, wo_ref, o_ref = refs
    x = x_ref[0]
    h = _norm_mod(x, g_ref[...], sh_ref[0], sc_ref[0]).astype(BF16)
    dff = wo_ref.shape[0]
    acc = None
    for lo in range(0, dff, tf):
        gate = jnp.dot(h, wi_ref[:, lo:lo + tf], preferred_element_type=F32)
        up = jnp.dot(h, wi_ref[:, dff + lo:dff + lo + tf], preferred_element_type=F32)
        act = ((gate * _sigmoid(gate)) * up).astype(BF16)
        part = jnp.dot(act, wo_ref[lo:lo + tf, :], preferred_element_type=F32)
        acc = part if acc is None else acc + part
    y = x + 0.5 * gt_ref[0] * acc
    if final:
        ms = jnp.mean(y * y, axis=-1, keepdims=True)
        y = y * lax.rsqrt(ms + RMS_EPS) * fin_ref[...]
    o_ref[0] = y


def _ffn(x, gain, shift, scale, gate, wi, wo, final_gain=None, *, tm=512, tf=256):
    b, s, d = x.shape
    dff = wo.shape[0]
    final = final_gain is not None
    vec = pl.BlockSpec((1, 1, d), lambda bi, i: (bi, 0, 0))
    resident = lambda shape: pl.BlockSpec(shape, lambda bi, i: (0, 0), pipeline_mode=pl.Buffered(1))
    in_specs = [pl.BlockSpec((1, tm, d), lambda bi, i: (bi, i, 0)),
                pl.BlockSpec((1, d), lambda bi, i: (0, 0)),
                vec, vec, vec,
                resident((d, 2 * dff)), resident((dff, d))]
    args = [x, gain.reshape(1, d), shift, scale, gate, wi, wo]
    if final:
        in_specs.append(pl.BlockSpec((1, d), lambda bi, i: (0, 0)))
        args.append(final_gain.reshape(1, d))
    return pl.pallas_call(
        functools.partial(_ffn_kernel, final=final, tf=tf),
        grid=(b, s // tm),
        in_specs=in_specs,
        out_specs=pl.BlockSpec((1, tm, d), lambda bi, i: (bi, i, 0)),
        out_shape=jax.ShapeDtypeStruct((b, s, d), F32),
        compiler_params=_cparams(("parallel", "parallel")),
        name="ffn_final" if final else "ffn",
    )(*args)


def _win_kernel(x_ref, g_ref, sh_ref, sc_ref, wa_ref, wr_ref, oa_ref, or_ref, h_scr):
    @pl.when(pl.program_id(2) == 0)
    def _():
        h = _norm_mod(x_ref[0], g_ref[...], sh_ref[0], sc_ref[0])
        h_scr[...] = h.astype(BF16)

    h = h_scr[...]
    oa_ref[0] = jnp.dot(h, wa_ref[...], preferred_element_type=F32).astype(BF16)
    or_ref[0] = jnp.dot(h, wr_ref[...], preferred_element_type=F32)


def _in_proj(x, gain, shift, scale, w_attn, w_rest, *, tm=1024, nj=4):
    b, s, d = x.shape
    ta, tr = w_attn.shape[1] // nj, w_rest.shape[1] // nj
    vec = pl.BlockSpec((1, 1, d), lambda bi, i, j: (bi, 0, 0))
    return pl.pallas_call(
        _win_kernel,
        grid=(b, s // tm, nj),
        in_specs=[pl.BlockSpec((1, tm, d), lambda bi, i, j: (bi, i, 0)),
                  pl.BlockSpec((1, d), lambda bi, i, j: (0, 0)),
                  vec, vec,
                  pl.BlockSpec((d, ta), lambda bi, i, j: (0, j)),
                  pl.BlockSpec((d, tr), lambda bi, i, j: (0, j))],
        out_specs=[pl.BlockSpec((1, tm, ta), lambda bi, i, j: (bi, i, j)),
                   pl.BlockSpec((1, tm, tr), lambda bi, i, j: (bi, i, j))],
        out_shape=[jax.ShapeDtypeStruct((b, s, w_attn.shape[1]), BF16),
                   jax.ShapeDtypeStruct((b, s, w_rest.shape[1]), F32)],
        scratch_shapes=[pltpu.VMEM((tm, d), BF16)],
        compiler_params=_cparams(("parallel", "parallel", "arbitrary")),
        name="in_proj",
    )(x, gain.reshape(1, d), shift, scale, w_attn, w_rest)


def _dilated_kernel(q_ref, k_ref, v_ref, cos_ref, s1_ref, s2_ref, *rest, seq, pad):
    n_pat = len(DILATED_PAIRS)
    mask_refs = rest[:n_pat]
    o_ref, q_scr, k_scr, v_scr, og_scr, lse_scr = rest[n_pat:]
    cos, s1, s2 = cos_ref[0], s1_ref[0], s2_ref[0]

    def rotary(t):
        return (t * cos + pltpu.roll(t, LANES - ROT_DIM // 2, axis=1) * s1
                + pltpu.roll(t, ROT_DIM // 2, axis=1) * s2)

    q_scr[...] = rotary(q_ref[0].astype(F32)) * (HEAD_DIM ** -0.5)
    zeros = jnp.zeros((pad, LANES), F32)
    k_scr[pl.ds(0, pad), :] = zeros
    k_scr[pl.ds(pad + seq, pad), :] = zeros
    v_scr[pl.ds(0, pad), :] = zeros
    v_scr[pl.ds(pad + seq, pad), :] = zeros
    k_scr[pl.ds(pad, seq), :] = rotary(k_ref[0].astype(F32))
    v_scr[pl.ds(pad, seq), :] = v_ref[0].astype(F32)

    lane = lax.broadcasted_iota(jnp.int32, (QBLK, LANES), 1)
    head0 = lane < HEAD_DIM

    for g, (window, dil) in enumerate(DILATED_PAIRS):
        nblk, lead, nkeys = _dilated_geometry(seq, window, dil)

        def group(gi, carry, g=g, dil=dil, lead=lead, nkeys=nkeys, nblk=nblk):
            units = []
            for u in range(DILATED_GROUP):
                idx = gi * DILATED_GROUP + u
                cls = idx // nblk
                n = idx % nblk
                q_start = cls + dil * QBLK * n
                k_start = pad + cls + dil * (QBLK * n - lead)
                if dil == 1:
                    q_rows = pl.ds(q_start, QBLK)
                    k_rows = pl.ds(k_start, nkeys)
                else:
                    q_rows = pl.ds(q_start, QBLK, stride=dil)
                    k_rows = pl.ds(k_start, nkeys, stride=dil)
                which = jnp.where(n == 0, 0, jnp.where(n == nblk - 1, 2, 1)) if nblk > 1 else 0
                units.append((q_rows, q_scr[q_rows, :], k_scr[k_rows, :].astype(BF16),
                              v_scr[k_rows, :].astype(BF16), which))
            scores = []
            for _, q, kw, _, which in units:
                for hmask in (head0, jnp.logical_not(head0)):
                    qh = jnp.where(hmask, q, 0.0).astype(BF16)
                    s = lax.dot_general(qh, kw, (((1,), (1,)), ((), ())),
                                        preferred_element_type=F32)
                    scores.append(s + mask_refs[g][which])
            probs, invs, lses = [], [], []
            for s in scores:
                m = jnp.max(s, axis=-1, keepdims=True)
                p = jnp.exp(s - m)
                den = jnp.sum(p, axis=-1, keepdims=True)
                probs.append(p.astype(BF16))
                invs.append(1.0 / den)
                lses.append(m + jnp.log(den))
            for u, (q_rows, _, _, vw, _) in enumerate(units):
                o0 = jnp.dot(probs[2 * u], vw, preferred_element_type=F32) * invs[2 * u]
                o1 = jnp.dot(probs[2 * u + 1], vw, preferred_element_type=F32) * invs[2 * u + 1]
                og_scr[g, q_rows, :] = jnp.where(head0, o0, o1)
                lse_scr[g, q_rows, :] = jnp.where(head0, lses[2 * u], lses[2 * u + 1])
            return carry

        lax.fori_loop(0, dil * nblk // DILATED_GROUP, group, 0)

    l0, l1, l2 = lse_scr[0], lse_scr[1], lse_scr[2]
    m = jnp.maximum(jnp.maximum(l0, l1), l2)
    e0, e1, e2 = jnp.exp(l0 - m), jnp.exp(l1 - m), jnp.exp(l2 - m)
    out = (e0 * og_scr[0] + e1 * og_scr[1] + e2 * og_scr[2]) * (1.0 / (e0 + e1 + e2))
    o_ref[0] = out.astype(o_ref.dtype)


def _dilated_geometry(seq, window, dil):
    radius = window // (2 * dil)
    nblk = seq // dil // QBLK
    lead = radius if nblk > 1 else 0
    return nblk, lead, QBLK + 2 * lead


def _dilated_masks(seq):
    out = []
    for window, dil in DILATED_PAIRS:
        radius = window // (2 * dil)
        nblk, lead, nkeys = _dilated_geometry(seq, window, dil)
        qi = np.arange(QBLK)[:, None]
        kj = np.arange(nkeys)[None, :]
        band = np.abs(kj - lead - qi) <= radius
        tabs = []
        for n in ((0, 1, nblk - 1) if nblk > 1 else (0,)):
            key_pos = QBLK * n + kj - lead
            tabs.append(band & (key_pos >= 0) & (key_pos < nblk * QBLK))
        out.append(jnp.asarray(np.where(np.stack(tabs), 0.0, NEG), F32))
    return out


def _dilated_attention(attn, cos, s1, s2, masks):
    b, s, _ = attn.shape
    pad = max((w // (2 * d)) * d for w, d in DILATED_PAIRS)
    hp = W_MIX // LANES
    per_cb = COLBLK // LANES
    col = lambda cb: pl.BlockSpec((1, s, LANES), lambda bi, h, cb=cb: (bi, 0, cb * per_cb + h))
    tab = pl.BlockSpec((1, s, LANES), lambda bi, h: (bi, 0, 0))
    const = lambda shape: pl.BlockSpec(shape, lambda bi, h: (0,) * len(shape))
    return pl.pallas_call(
        functools.partial(_dilated_kernel, seq=s, pad=pad),
        grid=(b, hp),
        in_specs=[col(CB_AQ), col(CB_AK), col(CB_AV), tab, tab, tab] + [const(m.shape) for m in masks],
        out_specs=pl.BlockSpec((1, s, LANES), lambda bi, h: (bi, 0, h)),
        out_shape=jax.ShapeDtypeStruct((b, s, W_MIX), BF16),
        scratch_shapes=[pltpu.VMEM((s, LANES), F32),
                        pltpu.VMEM((s + 2 * pad, LANES), F32),
                        pltpu.VMEM((s + 2 * pad, LANES), F32),
                        pltpu.VMEM((len(DILATED_PAIRS), s, LANES), F32),
                        pltpu.VMEM((len(DILATED_PAIRS), s, LANES), F32)],
        compiler_params=_cparams(("parallel", "parallel")),
        name="dilated_attn",
    )(attn, attn, attn, cos, s1, s2, *masks)


def _rope_tables(positions):
    half = ROT_DIM // 2
    lane = np.arange(LANES) % HEAD_DIM
    inv_freq = ROPE_THETA ** (-jnp.asarray(lane % half, F32) * 2.0 / ROT_DIM)
    ang = positions.astype(F32)[..., None] * inv_freq
    cos, sin = jnp.cos(ang), jnp.sin(ang)
    return (jnp.where(lane < ROT_DIM, cos, 1.0), jnp.where(lane < half, -sin, 0.0),
            jnp.where((lane >= half) & (lane < ROT_DIM), sin, 0.0))


def _natten_kernel(q_ref, k_ref, v_ref, bias_ref, o_ref, *, rows, win_rows):
    lane = lax.broadcasted_iota(jnp.int32, (GRID_W, LANES), 1)
    head0 = lane < HEAD_DIM
    scale = HEAD_DIM ** -0.5
    nkeys = win_rows * GRID_W

    def group(gi, carry):
        units = []
        for u in range(NATTEN_GROUP):
            r = gi * NATTEN_GROUP + u
            r_start = jnp.clip(r - win_rows // 2, 0, rows - win_rows)
            q_rows = pl.ds(pl.multiple_of(r * GRID_W, GRID_W), GRID_W)
            k_rows = pl.ds(pl.multiple_of(r_start * GRID_W, GRID_W), nkeys)
            units.append((q_rows, r - r_start, q_ref[0, q_rows, :],
                          k_ref[0, k_rows, :], v_ref[0, k_rows, :]))
        scores = []
        for _, delta, q, kw, _ in units:
            q = q * scale
            zero = jnp.zeros_like(q)
            q2 = jnp.concatenate([jnp.where(head0, q, zero), jnp.where(head0, zero, q)], axis=0)
            s2 = lax.dot_general(q2, kw, (((1,), (1,)), ((), ())), preferred_element_type=F32)
            scores += [s2[:GRID_W] + bias_ref[0, delta], s2[GRID_W:] + bias_ref[1, delta]]
        probs, invs = [], []
        for s in scores:
            m = jnp.max(s, axis=-1, keepdims=True)
            p = jnp.exp(s - m)
            probs.append(p.astype(BF16))
            invs.append(1.0 / jnp.sum(p, axis=-1, keepdims=True))
        for u, (q_rows, _, _, _, vw) in enumerate(units):
            o2 = jnp.dot(jnp.concatenate(probs[2 * u:2 * u + 2], axis=0), vw,
                         preferred_element_type=F32)
            o_ref[0, q_rows, :] = jnp.where(head0, o2[:GRID_W] * invs[2 * u],
                                            o2[GRID_W:] * invs[2 * u + 1]).astype(o_ref.dtype)
        return carry

    lax.fori_loop(0, rows // NATTEN_GROUP, group, 0)


def _natten_bias_tables(rpb, rows):
    wr = min(NA_WIN_ROWS, rows)
    wc = NA_WIN_COLS
    cols = np.arange(GRID_W)
    c_start = np.clip(cols - wc // 2, 0, GRID_W - wc)
    col_in = (cols[None, :] >= c_start[:, None]) & (cols[None, :] < c_start[:, None] + wc)
    coff = np.clip(cols[None, :] - cols[:, None], -(wc - 1), wc - 1) + wc - 1
    onehot = (coff[None] == np.arange(2 * wc - 1)[:, None, None]).astype(np.float32)
    by_col = jnp.einsum("lhrc,cqk->lhqrk", rpb.astype(F32), onehot, precision=lax.Precision.HIGHEST)
    by_col = jnp.where(col_in[:, None, :], by_col, NEG)
    top = NA_WIN_ROWS - 1
    tabs = jnp.stack([by_col[:, :, :, top - dl:top - dl + wr] for dl in range(wr)], axis=2)
    return tabs.reshape(rpb.shape[0], rpb.shape[1], wr, GRID_W, wr * GRID_W)


def _neighborhood_attention(attn, bias_tab):
    b, s, _ = attn.shape
    rows = s // GRID_W
    wr = min(NA_WIN_ROWS, rows)
    hp = W_MIX // LANES
    per_cb = COLBLK // LANES
    col = lambda cb: pl.BlockSpec((1, s, LANES), lambda bi, h, cb=cb: (bi, 0, cb * per_cb + h))
    return pl.pallas_call(
        functools.partial(_natten_kernel, rows=rows, win_rows=wr),
        grid=(b, hp),
        in_specs=[col(CB_NQ), col(CB_NK), col(CB_NV),
                  pl.BlockSpec((LANES // HEAD_DIM, wr, GRID_W, wr * GRID_W),
                               lambda bi, h: (h, 0, 0, 0))],
        out_specs=pl.BlockSpec((1, s, LANES), lambda bi, h: (bi, 0, h)),
        out_shape=jax.ShapeDtypeStruct((b, s, W_MIX), BF16),
        compiler_params=_cparams(("parallel", "parallel")),
        name="natten",
    )(attn, attn, attn, bias_tab)


def _seg_sum(x, ones_bd):
    return jnp.dot(x.astype(BF16), ones_bd, preferred_element_type=F32)


def _rwkv_pre_kernel(r_ref, k_ref, v_ref, l_ref,
                     rp_ref, kp_ref, vp_ref, lp_ref,
                     rn_ref, kn_ref, vn_ref, ln_ref,
                     mu_ref, mul_ref, wc_ref, w0_ref, a0_ref, kk_ref, ka_ref, rk_ref, ones_ref,
                     w_o, k_o, a_o, b_o, bonus_o, pr_o, pv_o, *, tm):
    i = pl.program_id(1)
    first = i == 0
    last = i == pl.num_programs(1) - 1
    row = lax.broadcasted_iota(jnp.int32, (tm, COLBLK), 0)
    ones_bd = ones_ref[...]

    def neighbours(x_ref, p_ref, n_ref):
        x = x_ref[0]
        prev_row = jnp.where(first, 0.0, p_ref[0, SUBLANES - 1:SUBLANES, :])
        next_row = jnp.where(last, 0.0, n_ref[0, 0:1, :])
        prev = jnp.where(row == 0, prev_row, pltpu.roll(x, 1, axis=0))
        nxt = jnp.where(row == tm - 1, next_row, pltpu.roll(x, tm - 1, axis=0))
        return x, (prev, nxt)

    pr, nb_r = neighbours(r_ref, rp_ref, rn_ref)
    pk, nb_k = neighbours(k_ref, kp_ref, kn_ref)
    pv, nb_v = neighbours(v_ref, vp_ref, vn_ref)
    pl_, nb_l = neighbours(l_ref, lp_ref, ln_ref)

    bonus = jnp.zeros((tm, COLBLK), F32)
    for d in range(2):
        r = pr + (nb_r[d] - pr) * mu_ref[d, 0:1, :]
        k = pk + (nb_k[d] - pk) * mu_ref[d, 1:2, :]
        v = pv + (nb_v[d] - pv) * mu_ref[d, 2:3, :]
        xl = pl_ + (nb_l[d] - pl_) * mul_ref[d:d + 1, :]
        zw = jnp.dot(jnp.tanh(xl[:, :LANES]).astype(BF16), wc_ref[d, :LANES, :],
                     preferred_element_type=F32)
        za = jnp.dot(xl[:, LANES:2 * LANES].astype(BF16), wc_ref[d, LANES:2 * LANES, :],
                     preferred_element_type=F32)
        z = zw + za
        wz = w0_ref[d:d + 1, :] + z[:, :COLBLK]
        az = a0_ref[d:d + 1, :] + z[:, COLBLK:]
        decay = jnp.exp(-DECAY_SCALE * _sigmoid(wz))
        a = _sigmoid(az)
        kk = k * kk_ref[...]
        kk = kk * lax.rsqrt(jnp.maximum(_seg_sum(kk * kk, ones_bd), 1e-24))
        k2 = k * (1.0 + (a - 1.0) * ka_ref[...])
        bonus = bonus + _seg_sum(r * k2 * rk_ref[...], ones_bd) * v
        w_o[d, 0] = decay
        k_o[d, 0] = k2.astype(k_o.dtype)
        a_o[d, 0] = -kk
        b_o[d, 0] = (kk * a).astype(b_o.dtype)
    bonus_o[0] = bonus
    pr_o[0] = pr.astype(pr_o.dtype)
    pv_o[0] = pv


def _rwkv_pre(proj, mu_rkv, mu_lora, w_comb, w0, a0, k_k, k_a, r_k, ones_bd, *, tm=256):
    b, s, _ = proj.shape
    nsub = tm // SUBLANES
    last_sub = s // SUBLANES - 1
    main = lambda cb: pl.BlockSpec((1, tm, COLBLK), lambda bi, i, cb=cb: (bi, i, cb))
    prev = lambda cb: pl.BlockSpec(
        (1, SUBLANES, COLBLK), lambda bi, i, cb=cb: (bi, jnp.maximum(i * nsub - 1, 0), cb))
    nxt = lambda cb: pl.BlockSpec(
        (1, SUBLANES, COLBLK), lambda bi, i, cb=cb: (bi, jnp.minimum((i + 1) * nsub, last_sub), cb))
    cbs = (CB_PR, CB_PK, CB_PV, CB_LORA)
    const = lambda shape: pl.BlockSpec(shape, lambda bi, i: (0,) * len(shape))
    dir_out = pl.BlockSpec((2, 1, tm, COLBLK), lambda bi, i: (0, bi, i, 0))
    dir_shape = lambda dt: jax.ShapeDtypeStruct((2, b, s, COLBLK), dt)
    return pl.pallas_call(
        functools.partial(_rwkv_pre_kernel, tm=tm),
        grid=(b, s // tm),
        in_specs=([main(cb) for cb in cbs] + [prev(cb) for cb in cbs] + [nxt(cb) for cb in cbs]
                  + [const((2, 3, COLBLK)), const((2, COLBLK)), const((2, 2 * LANES, 2 * COLBLK)),
                     const((2, COLBLK)), const((2, COLBLK)), const((1, COLBLK)), const((1, COLBLK)),
                     const((1, COLBLK)), const((COLBLK, COLBLK))]),
        out_specs=[dir_out] * 4 + [pl.BlockSpec((1, tm, COLBLK), lambda bi, i: (bi, i, 0))] * 3,
        out_shape=([dir_shape(dt) for dt in (F32, BF16, F32, BF16)]
                   + [jax.ShapeDtypeStruct((b, s, COLBLK), dt) for dt in (F32, BF16, F32)]),
        compiler_params=_cparams(("parallel", "parallel")),
        name="rwkv_pre",
    )(*([proj] * 12), mu_rkv, mu_lora, w_comb, w0, a0, k_k, k_a, r_k, ones_bd)


def _rwkv_scan_kernel(pr_ref, w_ref, k_ref, pv_ref, a_ref, b_ref, an_ref, mur_ref, muv_ref, y_ref,
                      st_scr, sa_scr, rows_scr, prevr_scr, prevv_scr, *, tt):
    d = pl.program_id(0)
    n = HEAD_DIM

    @pl.when(pl.program_id(1) == 0)
    def _():
        st_scr[...] = jnp.zeros_like(st_scr)
        sa_scr[...] = jnp.zeros_like(sa_scr)
        prevr_scr[...] = jnp.zeros_like(prevr_scr)
        prevv_scr[...] = jnp.zeros_like(prevv_scr)

    pack = rows_scr.shape[-2]
    mu_r, mu_v = mur_ref[0], muv_ref[0]

    def row(slot, q, k):
        return jnp.broadcast_to(rows_scr[slot, q, k // pack, pl.ds(k % pack, 1), :],
                                sa_scr.shape[1:])

    def step(j, t, a_next, sa, gam, pr_prev, pv_prev):
        slot = j % 2
        gam = gam * w_ref[0, t]
        inv = 1.0 / gam
        pr, pv = pr_ref[t].astype(F32), pv_ref[t]
        r = pr + (pr_prev - pr) * mu_r
        v = pv + (pv_prev - pv) * mu_v
        rows_scr[slot, 0] = b_ref[0, t].astype(F32) * inv
        rows_scr[slot, 1] = k_ref[0, t].astype(F32) * inv
        rows_scr[slot, 2] = r * gam
        rows_scr[slot, 3] = a_next * gam
        y = jnp.zeros_like(v)
        sa_next = [jnp.zeros_like(v), jnp.zeros_like(v)]
        for k in range(n):
            new = st_scr[k] + (sa * row(slot, 0, k) + v * row(slot, 1, k))
            st_scr[k] = new
            y = y + new * row(slot, 2, k)
            sa_next[k % 2] = sa_next[k % 2] + new * row(slot, 3, k)
        y_ref[0, t] = y
        return sa_next[0] + sa_next[1], gam, pr, pv

    def body(j, carry):
        t = j + d * (tt - 1 - 2 * j)
        return step(j, t, a_ref[0, t + 1 - 2 * d], *carry)

    carry = (sa_scr[...], jnp.ones(rows_scr.shape[2:], F32), prevr_scr[...], prevv_scr[...])
    carry = lax.fori_loop(0, tt - 1, body, carry)
    sa, gam, pr, pv = step(tt - 1, (1 - d) * (tt - 1), an_ref[0, 0], *carry)
    sa_scr[...] = sa
    prevr_scr[...] = pr
    prevv_scr[...] = pv
    rows_scr[0, 0] = gam
    for k in range(n):
        st_scr[k] = st_scr[k] * row(0, 0, k)


def _rwkv_scan(pr, w, k, pv, a, b, mu_r, mu_v, *, tt=32):
    s, n, c = pv.shape
    nt = s // tt
    state_tile = (n // SUBLANES, SUBLANES, c)
    vec_tile = (n // (2 * SUBLANES), 2 * SUBLANES, c)
    chunk = lambda d, i: (d, i + d * (nt - 1 - 2 * i), 0, 0, 0)
    shared = lambda d, i: (i + d * (nt - 1 - 2 * i), 0, 0, 0)
    vec_spec = pl.BlockSpec((1, tt) + vec_tile, chunk)
    state_spec = pl.BlockSpec((1, tt) + state_tile, chunk)

    def next_first_step(d, i):
        fwd = jnp.minimum((i + 1) * tt, s - 1)
        bwd = jnp.maximum((nt - 1 - i) * tt - 1, 0)
        return (d, jnp.where(d == 0, fwd, bwd), 0, 0, 0)

    w, k, a, b = (t.reshape((2, s) + vec_tile) for t in (w, k, a, b))
    y = pl.pallas_call(
        functools.partial(_rwkv_scan_kernel, tt=tt),
        grid=(2, nt),
        in_specs=[pl.BlockSpec((tt,) + vec_tile, shared), vec_spec, vec_spec,
                  pl.BlockSpec((tt,) + state_tile, shared), vec_spec, vec_spec,
                  pl.BlockSpec((1, 1) + vec_tile, next_first_step),
                  pl.BlockSpec((1,) + vec_tile, lambda d, i: (d, 0, 0, 0)),
                  pl.BlockSpec((1,) + state_tile, lambda d, i: (d, 0, 0, 0))],
        out_specs=state_spec,
        out_shape=jax.ShapeDtypeStruct((2, s) + state_tile, F32),
        scratch_shapes=[pltpu.VMEM((n,) + state_tile, F32), pltpu.VMEM(state_tile, F32),
                        pltpu.VMEM((2, 4) + vec_tile, F32),
                        pltpu.VMEM(vec_tile, F32), pltpu.VMEM(state_tile, F32)],
        compiler_params=_cparams(("parallel", "arbitrary")),
        name="rwkv_scan",
    )(pr.reshape((s,) + vec_tile), w, k, pv.reshape((s,) + state_tile), a, b, a,
      mu_r.reshape((2,) + vec_tile), mu_v.reshape((2,) + state_tile))
    return y.reshape(2, s, n, c)


def _to_chain_lanes(t):
    *lead, b, s, _ = t.shape
    k = len(lead)
    t = t.reshape(*lead, b, s, N_HEADS, HEAD_DIM)
    t = t.transpose(*range(k), k + 1, k + 3, k, k + 2)
    return t.reshape(*lead, s, HEAD_DIM, b * N_HEADS)


def _param_to_chain_lanes(p, b):
    lead = p.shape[:-1]
    p = jnp.swapaxes(p.reshape(*lead, N_HEADS, HEAD_DIM), -1, -2)
    return jnp.broadcast_to(p[..., None, :], lead + (HEAD_DIM, b, N_HEADS)).reshape(
        *lead, HEAD_DIM, b * N_HEADS)


def _from_chain_lanes(t, b):
    _, s, _, _ = t.shape
    t = t.reshape(2, s, HEAD_DIM, b, N_HEADS).transpose(0, 3, 1, 4, 2)
    return t.reshape(2, b, s, W_MIX)


def _merge_kernel(y_ref, bonus_ref, l_ref, gz0_ref, gz1_ref, gz2_ref, oa_ref, ob_ref, x_ref, gt_ref,
                  gnw_ref, gnb_ref, ones_ref, gc_ref, wb_ref, wo_ref, o_ref):
    ones_bd = ones_ref[...]
    inv_n = 1.0 / HEAD_DIM
    acc = bonus_ref[0]
    for d in range(2):
        y = y_ref[d, 0]
        mean = _seg_sum(y, ones_bd) * inv_n
        yc = y - mean
        var = _seg_sum(yc * yc, ones_bd) * inv_n
        acc = acc + (yc * lax.rsqrt(var + GN_EPS)) * gnw_ref[...] + gnb_ref[...]
    g = jnp.dot(_sigmoid(l_ref[0]).astype(BF16), gc_ref[...], preferred_element_type=F32)
    o_c = acc * g
    merged = (_sigmoid(gz0_ref[0])
              * jnp.dot(oa_ref[0], wb_ref[0], preferred_element_type=F32)
              + _sigmoid(gz1_ref[0])
              * jnp.dot(ob_ref[0], wb_ref[1], preferred_element_type=F32)
              + _sigmoid(gz2_ref[0])
              * jnp.dot(o_c.astype(BF16), wb_ref[2], preferred_element_type=F32))
    out = jnp.dot(merged.astype(BF16), wo_ref[...], preferred_element_type=F32)
    o_ref[0] = x_ref[0] + gt_ref[0] * out


def _merge(y, bonus, proj, o_a, o_b, x, gate, gn_w, gn_b, ones_bd, g_comb, w_branch, w_out, *, tm=512):
    b, s, d = x.shape
    gz_per = d // COLBLK
    tok = lambda width, cb: pl.BlockSpec((1, tm, width), lambda bi, i, cb=cb: (bi, i, cb))
    const = lambda shape: pl.BlockSpec(shape, lambda bi, i: (0,) * len(shape))
    return pl.pallas_call(
        _merge_kernel,
        grid=(b, s // tm),
        in_specs=[pl.BlockSpec((2, 1, tm, COLBLK), lambda bi, i: (0, bi, i, 0)),
                  tok(COLBLK, 0),
                  tok(COLBLK, CB_LORA),
                  tok(d, CB_GZ // gz_per), tok(d, CB_GZ // gz_per + 1), tok(d, CB_GZ // gz_per + 2),
                  tok(COLBLK, 0), tok(COLBLK, 0), tok(d, 0),
                  pl.BlockSpec((1, 1, d), lambda bi, i: (bi, 0, 0)),
                  const((1, COLBLK)), const((1, COLBLK)), const((COLBLK, COLBLK)),
                  const((COLBLK, COLBLK)), const((3, COLBLK, d)), const((d, d))],
        out_specs=pl.BlockSpec((1, tm, d), lambda bi, i: (bi, i, 0)),
        out_shape=jax.ShapeDtypeStruct((b, s, d), F32),
        compiler_params=_cparams(("parallel", "parallel")),
        name="merge_out",
    )(y, bonus, proj, proj, proj, proj, o_a, o_b, x, gate, gn_w, gn_b, ones_bd, g_comb, w_branch, w_out)


def _split_w_in(w_in):
    d = w_in.shape[0]
    n_lora = 4 * DECAY_LORA + GATE_LORA
    n_rest = N_ATTN + CB_LORA * COLBLK + n_lora
    rest, gates = w_in[:, N_ATTN:n_rest], w_in[:, n_rest:]
    zpad = jnp.zeros((d, COLBLK - n_lora), w_in.dtype)
    return w_in[:, :N_ATTN].astype(BF16), jnp.concatenate([rest, zpad, gates], axis=1).astype(BF16)


def _lora_params(mu_w, mu_a, w_up, a_up, g_up):
    r = DECAY_LORA
    mu_l = jnp.zeros((2, COLBLK), F32)
    w_comb = jnp.zeros((2, 2 * LANES, 2 * COLBLK), F32)
    for d in range(2):
        wcol, acol = d * r, 2 * r + d * r
        mu_l = mu_l.at[d, wcol:wcol + r].set(mu_w[d]).at[d, acol:acol + r].set(mu_a[d])
        w_comb = w_comb.at[d, wcol:wcol + r, :COLBLK].set(w_up[d])
        w_comb = w_comb.at[d, acol:acol + r, COLBLK:].set(a_up[d])
    g_comb = jnp.zeros((COLBLK, COLBLK), F32).at[4 * r:4 * r + GATE_LORA].set(g_up)
    return mu_l, w_comb.astype(BF16), g_comb.astype(BF16)


def _head_block_ones():
    seg = np.arange(COLBLK) // HEAD_DIM
    return jnp.asarray(seg[:, None] == seg[None, :], dtype=BF16)


def kernel(x, c, positions, ada_w, ada_b, norm_gains, ffn_wi, ffn_wo, w_in, rpb, mu_rkv, mu_w, mu_a,
           w0, w_up, a0, a_up, g_up, k_k, k_a, r_k, gn_w, gn_b, w_branch, w_out, final_norm):
    depth = ada_w.shape[0]
    b, s, d = x.shape
    assert d == 2 * COLBLK and s % 1024 == 0 and s % (GRID_W * NA_WIN_ROWS) == 0

    mod = _modulation(c, ada_w, ada_b)
    cos, s1, s2 = _rope_tables(positions)
    ones_bd = _head_block_ones()
    bias_tabs = _natten_bias_tables(rpb, s // GRID_W)
    dil_masks = _dilated_masks(s)
    ffn_wi_bf, ffn_wo_bf = ffn_wi.astype(BF16), ffn_wo.astype(BF16)
    w_branch_bf, w_out_bf = w_branch.astype(BF16), w_out.astype(BF16)
    row = lambda t: t.reshape(1, -1)

    for l in range(depth):
        m = mod[l].reshape(b, N_MOD, 1, d)
        sh1, sc1, gt1, sh2, sc2, gt2, sh3, sc3, gt3 = (m[:, i] for i in range(N_MOD))

        x = _ffn(x, norm_gains[l, 0], sh1, sc1, gt1, ffn_wi_bf[l, 0], ffn_wo_bf[l, 0])

        attn, proj = _in_proj(x, norm_gains[l, 1], sh2, sc2, *_split_w_in(w_in[l]))
        o_a = _dilated_attention(attn, cos, s1, s2, dil_masks)
        o_b = _neighborhood_attention(attn, bias_tabs[l])
        mu_l, w_comb, g_comb = _lora_params(mu_w[l], mu_a[l], w_up[l], a_up[l], g_up[l])
        decay, k2, a_vec, b_vec, bonus, pr, pv = _rwkv_pre(
            proj, mu_rkv[l], mu_l, w_comb, w0[l], a0[l], row(k_k[l]), row(k_a[l]), row(r_k[l]), ones_bd)
        y = _rwkv_scan(_to_chain_lanes(pr), _to_chain_lanes(decay), _to_chain_lanes(k2),
                       _to_chain_lanes(pv), _to_chain_lanes(a_vec), _to_chain_lanes(b_vec),
                       _param_to_chain_lanes(mu_rkv[l, :, 0], b), _param_to_chain_lanes(mu_rkv[l, :, 2], b))
        y = _from_chain_lanes(y, b)
        x = _merge(y, bonus, proj, o_a, o_b, x, gt2, row(gn_w[l]), row(gn_b[l]), ones_bd, g_comb,
                   w_branch_bf[l], w_out_bf[l])

        x = _ffn(x, norm_gains[l, 2], sh3, sc3, gt3, ffn_wi_bf[l, 1], ffn_wo_bf[l, 1],
                 final_norm if l == depth - 1 else None)
    return x
```

```python
import functools

import numpy as np
import jax
import jax.numpy as jnp
from jax import lax
from jax.experimental import pallas as pl
from jax.experimental.pallas import tpu as pltpu

F32 = jnp.float32
BF16 = jnp.bfloat16

HEAD_DIM = 64
N_HEADS = 8
W_MIX = N_HEADS * HEAD_DIM
DILATED_PAIRS = ((128, 1), (512, 4), (2048, 16))
QBLK = 128
ROT_DIM = HEAD_DIM // 4
ROPE_THETA = 500000.0
GRID_W = 64
NA_WIN_ROWS = 8
NA_WIN_COLS = 16
DECAY_LORA = 64
ICLR_LORA = 64
GATE_LORA = 128
DECAY_SCALE = 0.6065306597126334
N_MOD = 9
RMS_EPS = 1e-6
GN_EPS = 64e-5
NEG = -1e30

LANES = 128
SUBLANES = 8
VMEM_LIMIT = 48 * 1024 * 1024
DILATED_GROUP = 4
NATTEN_GROUP = 8

COLBLK = 512
CB_AQ, CB_AK, CB_AV = 0, 1, 2
CB_NQ, CB_NK, CB_NV = 3, 4, 5
N_ATTN = 6 * COLBLK
CB_PR, CB_PK, CB_PV = 0, 1, 2
CB_LORA = 3
CB_GZ = 4


def _cparams(sem):
    return pltpu.CompilerParams(dimension_semantics=sem, vmem_limit_bytes=VMEM_LIMIT)


def _sigmoid(x):
    return 0.5 * jnp.tanh(0.5 * x) + 0.5


def _mod_kernel(c_ref, w_ref, b_ref, o_ref):
    c = c_ref[...]
    cs = (c * _sigmoid(c)).astype(BF16)
    o_ref[0] = jnp.dot(cs, w_ref[0].astype(BF16), preferred_element_type=F32) + b_ref[0]


def _modulation(c, ada_w, ada_b):
    depth, d, nd = ada_w.shape
    b = c.shape[0]
    return pl.pallas_call(
        _mod_kernel,
        grid=(depth, nd // d),
        in_specs=[pl.BlockSpec((b, d), lambda l, j: (0, 0)),
                  pl.BlockSpec((1, d, d), lambda l, j: (l, 0, j)),
                  pl.BlockSpec((1, 1, d), lambda l, j: (l, 0, j))],
        out_specs=pl.BlockSpec((1, b, d), lambda l, j: (l, 0, j)),
        out_shape=jax.ShapeDtypeStruct((depth, b, nd), F32),
        compiler_params=_cparams(("parallel", "parallel")),
        name="adaln_mod",
    )(c, ada_w, ada_b.reshape(depth, 1, nd))


def _norm_mod(x, gain, shift, scale):
    ms = jnp.mean(x * x, axis=-1, keepdims=True)
    y = x * lax.rsqrt(ms + RMS_EPS) * gain
    return y * (1.0 + scale) + shift


def _ffn_kernel(*refs, final, tf):
    if final:
        x_ref, g_ref, sh_ref, sc_ref, gt_ref, wi_ref, wo_ref, fin_ref, o_ref = refs
    else:
        x_ref, g_ref, sh_ref, sc_ref, gt_ref, wi_ref, wo_ref, o_ref = refs
    x = x_ref[0]
    h = _norm_mod(x, g_ref[...], sh_ref[0], sc_ref[0]).astype(BF16)
    dff = wo_ref.shape[0]
    acc = None
    for lo in range(0, dff, tf):
        gate = jnp.dot(h, wi_ref[:, lo:lo + tf], preferred_element_type=F32)
        up = jnp.dot(h, wi_ref[:, dff + lo:dff + lo + tf], preferred_element_type=F32)
        act = ((gate * _sigmoid(gate)) * up).astype(BF16)
        part = jnp.dot(act, wo_ref[lo:lo + tf, :], preferred_element_type=F32)
        acc = part if acc is None else acc + part
    y = x + 0.5 * gt_ref[0] * acc
    if final:
        ms = jnp.mean(y * y, axis=-1, keepdims=True)
        y = y * lax.rsqrt(ms + RMS_EPS) * fin_ref[...]
    o_ref[0] = y


def _ffn(x, gain, shift, scale, gate, wi, wo, final_gain=None, *, tm=512, tf=256):
    b, s, d = x.shape
    dff = wo.shape[0]
    final = final_gain is not None
    vec = pl.BlockSpec((1, 1, d), lambda bi, i: (bi, 0, 0))
    resident = lambda shape: pl.BlockSpec(shape, lambda bi, i: (0, 0), pipeline_mode=pl.Buffered(1))
    in_specs = [pl.BlockSpec((1, tm, d), lambda bi, i: (bi, i, 0)),
                pl.BlockSpec((1, d), lambda bi, i: (0, 0)),
                vec, vec, vec,
                resident((d, 2 * dff)), resident((dff, d))]
    args = [x, gain.reshape(1, d), shift, scale, gate, wi, wo]
    if final:
        in_specs.append(pl.BlockSpec((1, d), lambda bi, i: (0, 0)))
        args.append(final_gain.reshape(1, d))
    return pl.pallas_call(
        functools.partial(_ffn_kernel, final=final, tf=tf),
        grid=(b, s // tm),
        in_specs=in_specs,
        out_specs=pl.BlockSpec((1, tm, d), lambda bi, i: (bi, i, 0)),
        out_shape=jax.ShapeDtypeStruct((b, s, d), F32),
        compiler_params=_cparams(("parallel", "parallel")),
        name="ffn_final" if final else "ffn",
    )(*args)


def _win_kernel(x_ref, g_ref, sh_ref, sc_ref, wa_ref, wr_ref, oa_ref, or_ref, h_scr):
    @pl.when(pl.program_id(2) == 0)
    def _():
        h = _norm_mod(x_ref[0], g_ref[...], sh_ref[0], sc_ref[0])
        h_scr[...] = h.astype(BF16)

    h = h_scr[...]
    oa_ref[0] = jnp.dot(h, wa_ref[...], preferred_element_type=F32).astype(BF16)
    or_ref[0] = jnp.dot(h, wr_ref[...], preferred_element_type=F32)


def _in_proj(x, gain, shift, scale, w_attn, w_rest, *, tm=1024, nj=4):
    b, s, d = x.shape
    ta, tr = w_attn.shape[1] // nj, w_rest.shape[1] // nj
    vec = pl.BlockSpec((1, 1, d), lambda bi, i, j: (bi, 0, 0))
    return pl.pallas_call(
        _win_kernel,
        grid=(b, s // tm, nj),
        in_specs=[pl.BlockSpec((1, tm, d), lambda bi, i, j: (bi, i, 0)),
                  pl.BlockSpec((1, d), lambda bi, i, j: (0, 0)),
                  vec, vec,
                  pl.BlockSpec((d, ta), lambda bi, i, j: (0, j)),
                  pl.BlockSpec((d, tr), lambda bi, i, j: (0, j))],
        out_specs=[pl.BlockSpec((1, tm, ta), lambda bi, i, j: (bi, i, j)),
                   pl.BlockSpec((1, tm, tr), lambda bi, i, j: (bi, i, j))],
        out_shape=[jax.ShapeDtypeStruct((b, s, w_attn.shape[1]), BF16),
                   jax.ShapeDtypeStruct((b, s, w_rest.shape[1]), F32)],
        scratch_shapes=[pltpu.VMEM((tm, d), BF16)],
        compiler_params=_cparams(("parallel", "parallel", "arbitrary")),
        name="in_proj",
    )(x, gain.reshape(1, d), shift, scale, w_attn, w_rest)


def _dilated_kernel(q_ref, k_ref, v_ref, cos_ref, s1_ref, s2_ref, *rest, seq, pad):
    n_pat = len(DILATED_PAIRS)
    mask_refs = rest[:n_pat]
    o_ref, q_scr, k_scr, v_scr, og_scr, lse_scr = rest[n_pat:]
    cos, s1, s2 = cos_ref[0], s1_ref[0], s2_ref[0]

    def rotary(t):
        return (t * cos + pltpu.roll(t, LANES - ROT_DIM // 2, axis=1) * s1
                + pltpu.roll(t, ROT_DIM // 2, axis=1) * s2)

    q_scr[...] = rotary(q_ref[0].astype(F32)) * (HEAD_DIM ** -0.5)
    zeros = jnp.zeros((pad, LANES), F32)
    k_scr[pl.ds(0, pad), :] = zeros
    k_scr[pl.ds(pad + seq, pad), :] = zeros
    v_scr[pl.ds(0, pad), :] = zeros
    v_scr[pl.ds(pad + seq, pad), :] = zeros
    k_scr[pl.ds(pad, seq), :] = rotary(k_ref[0].astype(F32))
    v_scr[pl.ds(pad, seq), :] = v_ref[0].astype(F32)

    lane = lax.broadcasted_iota(jnp.int32, (QBLK, LANES), 1)
    head0 = lane < HEAD_DIM

    for g, (window, dil) in enumerate(DILATED_PAIRS):
        nblk, lead, nkeys = _dilated_geometry(seq, window, dil)

        def group(gi, carry, g=g, dil=dil, lead=lead, nkeys=nkeys, nblk=nblk):
            units = []
            for u in range(DILATED_GROUP):
                idx = gi * DILATED_GROUP + u
                cls = idx // nblk
                n = idx % nblk
                q_start = cls + dil * QBLK * n
                k_start = pad + cls + dil * (QBLK * n - lead)
                if dil == 1:
                    q_rows = pl.ds(q_start, QBLK)
                    k_rows = pl.ds(k_start, nkeys)
                else:
                    q_rows = pl.ds(q_start, QBLK, stride=dil)
                    k_rows = pl.ds(k_start, nkeys, stride=dil)
                which = jnp.where(n == 0, 0, jnp.where(n == nblk - 1, 2, 1)) if nblk > 1 else 0
                units.append((q_rows, q_scr[q_rows, :], k_scr[k_rows, :].astype(BF16),
                              v_scr[k_rows, :].astype(BF16), which))
            scores = []
            for _, q, kw, _, which in units:
                for hmask in (head0, jnp.logical_not(head0)):
                    qh = jnp.where(hmask, q, 0.0).astype(BF16)
                    s = lax.dot_general(qh, kw, (((1,), (1,)), ((), ())),
                                        preferred_element_type=F32)
                    scores.append(s + mask_refs[g][which])
            probs, invs, lses = [], [], []
            for s in scores:
                m = jnp.max(s, axis=-1, keepdims=True)
                p = jnp.exp(s - m)
                den = jnp.sum(p, axis=-1, keepdims=True)
                probs.append(p.astype(BF16))
                invs.append(1.0 / den)
                lses.append(m + jnp.log(den))
            for u, (q_rows, _, _, vw, _) in enumerate(units):
                o0 = jnp.dot(probs[2 * u], vw, preferred_element_type=F32) * invs[2 * u]
                o1 = jnp.dot(probs[2 * u + 1], vw, preferred_element_type=F32) * invs[2 * u + 1]
                og_scr[g, q_rows, :] = jnp.where(head0, o0, o1)
                lse_scr[g, q_rows, :] = jnp.where(head0, lses[2 * u], lses[2 * u + 1])
            return carry

        lax.fori_loop(0, dil * nblk // DILATED_GROUP, group, 0)

    l0, l1, l2 = lse_scr[0], lse_scr[1], lse_scr[2]
    m = jnp.maximum(jnp.maximum(l0, l1), l2)
    e0, e1, e2 = jnp.exp(l0 - m), jnp.exp(l1 - m), jnp.exp(l2 - m)
    out = (e0 * og_scr[0] + e1 * og_scr[1] + e2 * og_scr[2]) * (1.0 / (e0 + e1 + e2))
    o_ref[0] = out.astype(o_ref.dtype)


def _dilated_geometry(seq, window, dil):
    radius = window // (2 * dil)
    nblk = seq // dil // QBLK
    lead = radius if nblk > 1 else 0
    return nblk, lead, QBLK + 2 * lead


def _dilated_masks(seq):
    out = []
    for window, dil in DILATED_PAIRS:
        radius = window // (2 * dil)
        nblk, lead, nkeys = _dilated_geometry(seq, window, dil)
        qi = np.arange(QBLK)[:, None]
        kj = np.arange(nkeys)[None, :]
        band = np.abs(kj - lead - qi) <= radius
        tabs = []
        for n in ((0, 1, nblk - 1) if nblk > 1 else (0,)):
            key_pos = QBLK * n + kj - lead
            tabs.append(band & (key_pos >= 0) & (key_pos < nblk * QBLK))
        out.append(jnp.asarray(np.where(np.stack(tabs), 0.0, NEG), F32))
    return out


def _dilated_attention(attn, cos, s1, s2, masks):
    b, s, _ = attn.shape
    pad = max((w // (2 * d)) * d for w, d in DILATED_PAIRS)
    hp = W_MIX // LANES
    per_cb = COLBLK // LANES
    col = lambda cb: pl.BlockSpec((1, s, LANES), lambda bi, h, cb=cb: (bi, 0, cb * per_cb + h))
    tab = pl.BlockSpec((1, s, LANES), lambda bi, h: (bi, 0, 0))
    const = lambda shape: pl.BlockSpec(shape, lambda bi, h: (0,) * len(shape))
    return pl.pallas_call(
        functools.partial(_dilated_kernel, seq=s, pad=pad),
        grid=(b, hp),
        in_specs=[col(CB_AQ), col(CB_AK), col(CB_AV), tab, tab, tab] + [const(m.shape) for m in masks],
        out_specs=pl.BlockSpec((1, s, LANES), lambda bi, h: (bi, 0, h)),
        out_shape=jax.ShapeDtypeStruct((b, s, W_MIX), BF16),
        scratch_shapes=[pltpu.VMEM((s, LANES), F32),
                        pltpu.VMEM((s + 2 * pad, LANES), F32),
                        pltpu.VMEM((s + 2 * pad, LANES), F32),
                        pltpu.VMEM((len(DILATED_PAIRS), s, LANES), F32),
                        pltpu.VMEM((len(DILATED_PAIRS), s, LANES), F32)],
        compiler_params=_cparams(("parallel", "parallel")),
        name="dilated_attn",
    )(attn, attn, attn, cos, s1, s2, *masks)


def _rope_tables(positions):
    half = ROT_DIM // 2
    lane = np.arange(LANES) % HEAD_DIM
    inv_freq = ROPE_THETA ** (-jnp.asarray(lane % half, F32) * 2.0 / ROT_DIM)
    ang = positions.astype(F32)[..., None] * inv_freq
    cos, sin = jnp.cos(ang), jnp.sin(ang)
    return (jnp.where(lane < ROT_DIM, cos, 1.0), jnp.where(lane < half, -sin, 0.0),
            jnp.where((lane >= half) & (lane < ROT_DIM), sin, 0.0))


def _natten_kernel(q_ref, k_ref, v_ref, bias_ref, o_ref, *, rows, win_rows):
    lane = lax.broadcasted_iota(jnp.int32, (GRID_W, LANES), 1)
    head0 = lane < HEAD_DIM
    scale = HEAD_DIM ** -0.5
    nkeys = win_rows * GRID_W

    def group(gi, carry):
        units = []
        for u in range(NATTEN_GROUP):
            r = gi * NATTEN_GROUP + u
            r_start = jnp.clip(r - win_rows // 2, 0, rows - win_rows)
            q_rows = pl.ds(pl.multiple_of(r * GRID_W, GRID_W), GRID_W)
            k_rows = pl.ds(pl.multiple_of(r_start * GRID_W, GRID_W), nkeys)
            units.append((q_rows, r - r_start, q_ref[0, q_rows, :],
                          k_ref[0, k_rows, :], v_ref[0, k_rows, :]))
        scores = []
        for _, delta, q, kw, _ in units:
            q = q * scale
            zero = jnp.zeros_like(q)
            q2 = jnp.concatenate([jnp.where(head0, q, zero), jnp.where(head0, zero, q)], axis=0)
            s2 = lax.dot_general(q2, kw, (((1,), (1,)), ((), ())), preferred_element_type=F32)
            scores += [s2[:GRID_W] + bias_ref[0, delta], s2[GRID_W:] + bias_ref[1, delta]]
        probs, invs = [], []
        for s in scores:
            m = jnp.max(s, axis=-1, keepdims=True)
            p = jnp.exp(s - m)
            probs.append(p.astype(BF16))
            invs.append(1.0 / jnp.sum(p, axis=-1, keepdims=True))
        for u, (q_rows, _, _, _, vw) in enumerate(units):
            o2 = jnp.dot(jnp.concatenate(probs[2 * u:2 * u + 2], axis=0), vw,
                         preferred_element_type=F32)
            o_ref[0, q_rows, :] = jnp.where(head0, o2[:GRID_W] * invs[2 * u],
                                            o2[GRID_W:] * invs[2 * u + 1]).astype(o_ref.dtype)
        return carry

    lax.fori_loop(0, rows // NATTEN_GROUP, group, 0)


def _natten_bias_tables(rpb, rows):
    wr = min(NA_WIN_ROWS, rows)
    wc = NA_WIN_COLS
    cols = np.arange(GRID_W)
    c_start = np.clip(cols - wc // 2, 0, GRID_W - wc)
    col_in = (cols[None, :] >= c_start[:, None]) & (cols[None, :] < c_start[:, None] + wc)
    coff = np.clip(cols[None, :] - cols[:, None], -(wc - 1), wc - 1) + wc - 1
    onehot = (coff[None] == np.arange(2 * wc - 1)[:, None, None]).astype(np.float32)
    by_col = jnp.einsum("lhrc,cqk->lhqrk", rpb.astype(F32), onehot, precision=lax.Precision.HIGHEST)
    by_col = jnp.where(col_in[:, None, :], by_col, NEG)
    top = NA_WIN_ROWS - 1
    tabs = jnp.stack([by_col[:, :, :, top - dl:top - dl + wr] for dl in range(wr)], axis=2)
    return tabs.reshape(rpb.shape[0], rpb.shape[1], wr, GRID_W, wr * GRID_W)


def _neighborhood_attention(attn, bias_tab):
    b, s, _ = attn.shape
    rows = s // GRID_W
    wr = min(NA_WIN_ROWS, rows)
    hp = W_MIX // LANES
    per_cb = COLBLK // LANES
    col = lambda cb: pl.BlockSpec((1, s, LANES), lambda bi, h, cb=cb: (bi, 0, cb * per_cb + h))
    return pl.pallas_call(
        functools.partial(_natten_kernel, rows=rows, win_rows=wr),
        grid=(b, hp),
        in_specs=[col(CB_NQ), col(CB_NK), col(CB_NV),
                  pl.BlockSpec((LANES // HEAD_DIM, wr, GRID_W, wr * GRID_W),
                               lambda bi, h: (h, 0, 0, 0))],
        out_specs=pl.BlockSpec((1, s, LANES), lambda bi, h: (bi, 0, h)),
        out_shape=jax.ShapeDtypeStruct((b, s, W_MIX), BF16),
        compiler_params=_cparams(("parallel", "parallel")),
        name="natten",
    )(attn, attn, attn, bias_tab)


def _seg_sum(x, ones_bd):
    return jnp.dot(x.astype(BF16), ones_bd, preferred_element_type=F32)


def _rwkv_pre_kernel(r_ref, k_ref, v_ref, l_ref,
                     rp_ref, kp_ref, vp_ref, lp_ref,
                     rn_ref, kn_ref, vn_ref, ln_ref,
                     mu_ref, mul_ref, wc_ref, w0_ref, a0_ref, kk_ref, ka_ref, rk_ref, ones_ref,
                     w_o, k_o, a_o, b_o, bonus_o, pr_o, pv_o, *, tm):
    i = pl.program_id(1)
    first = i == 0
    last = i == pl.num_programs(1) - 1
    row = lax.broadcasted_iota(jnp.int32, (tm, COLBLK), 0)
    ones_bd = ones_ref[...]

    def neighbours(x_ref, p_ref, n_ref):
        x = x_ref[0]
        prev_row = jnp.where(first, 0.0, p_ref[0, SUBLANES - 1:SUBLANES, :])
        next_row = jnp.where(last, 0.0, n_ref[0, 0:1, :])
        prev = jnp.where(row == 0, prev_row, pltpu.roll(x, 1, axis=0))
        nxt = jnp.where(row == tm - 1, next_row, pltpu.roll(x, tm - 1, axis=0))
        return x, (prev, nxt)

    pr, nb_r = neighbours(r_ref, rp_ref, rn_ref)
    pk, nb_k = neighbours(k_ref, kp_ref, kn_ref)
    pv, nb_v = neighbours(v_ref, vp_ref, vn_ref)
    pl_, nb_l = neighbours(l_ref, lp_ref, ln_ref)

    bonus = jnp.zeros((tm, COLBLK), F32)
    for d in range(2):
        r = pr + (nb_r[d] - pr) * mu_ref[d, 0:1, :]
        k = pk + (nb_k[d] - pk) * mu_ref[d, 1:2, :]
        v = pv + (nb_v[d] - pv) * mu_ref[d, 2:3, :]
        xl = pl_ + (nb_l[d] - pl_) * mul_ref[d:d + 1, :]
        zw = jnp.dot(jnp.tanh(xl[:, :LANES]).astype(BF16), wc_ref[d, :LANES, :],
                     preferred_element_type=F32)
        za = jnp.dot(xl[:, LANES:2 * LANES].astype(BF16), wc_ref[d, LANES:2 * LANES, :],
                     preferred_element_type=F32)
        z = zw + za
        wz = w0_ref[d:d + 1, :] + z[:, :COLBLK]
        az = a0_ref[d:d + 1, :] + z[:, COLBLK:]
        decay = jnp.exp(-DECAY_SCALE * _sigmoid(wz))
        a = _sigmoid(az)
        kk = k * kk_ref[...]
        kk = kk * lax.rsqrt(jnp.maximum(_seg_sum(kk * kk, ones_bd), 1e-24))
        k2 = k * (1.0 + (a - 1.0) * ka_ref[...])
        bonus = bonus + _seg_sum(r * k2 * rk_ref[...], ones_bd) * v
        w_o[d, 0] = decay
        k_o[d, 0] = k2.astype(k_o.dtype)
        a_o[d, 0] = (-kk).astype(a_o.dtype)
        b_o[d, 0] = (kk * a).astype(b_o.dtype)
    bonus_o[0] = bonus
    pr_o[0] = pr.astype(pr_o.dtype)
    pv_o[0] = pv


def _rwkv_pre(proj, mu_rkv, mu_lora, w_comb, w0, a0, k_k, k_a, r_k, ones_bd, *, tm=256):
    b, s, _ = proj.shape
    nsub = tm // SUBLANES
    last_sub = s // SUBLANES - 1
    main = lambda cb: pl.BlockSpec((1, tm, COLBLK), lambda bi, i, cb=cb: (bi, i, cb))
    prev = lambda cb: pl.BlockSpec(
        (1, SUBLANES, COLBLK), lambda bi, i, cb=cb: (bi, jnp.maximum(i * nsub - 1, 0), cb))
    nxt = lambda cb: pl.BlockSpec(
        (1, SUBLANES, COLBLK), lambda bi, i, cb=cb: (bi, jnp.minimum((i + 1) * nsub, last_sub), cb))
    cbs = (CB_PR, CB_PK, CB_PV, CB_LORA)
    const = lambda shape: pl.BlockSpec(shape, lambda bi, i: (0,) * len(shape))
    dir_out = pl.BlockSpec((2, 1, tm, COLBLK), lambda bi, i: (0, bi, i, 0))
    dir_shape = lambda dt: jax.ShapeDtypeStruct((2, b, s, COLBLK), dt)
    return pl.pallas_call(
        functools.partial(_rwkv_pre_kernel, tm=tm),
        grid=(b, s // tm),
        in_specs=([main(cb) for cb in cbs] + [prev(cb) for cb in cbs] + [nxt(cb) for cb in cbs]
                  + [const((2, 3, COLBLK)), const((2, COLBLK)), const((2, 2 * LANES, 2 * COLBLK)),
                     const((2, COLBLK)), const((2, COLBLK)), const((1, COLBLK)), const((1, COLBLK)),
                     const((1, COLBLK)), const((COLBLK, COLBLK))]),
        out_specs=[dir_out] * 4 + [pl.BlockSpec((1, tm, COLBLK), lambda bi, i: (bi, i, 0))] * 3,
        out_shape=([dir_shape(dt) for dt in (F32, BF16, BF16, BF16)]
                   + [jax.ShapeDtypeStruct((b, s, COLBLK), dt) for dt in (F32, BF16, F32)]),
        compiler_params=_cparams(("parallel", "parallel")),
        name="rwkv_pre",
    )(*([proj] * 12), mu_rkv, mu_lora, w_comb, w0, a0, k_k, k_a, r_k, ones_bd)


def _rwkv_scan_kernel(pr_ref, w_ref, k_ref, pv_ref, a_ref, b_ref, an_ref, mur_ref, muv_ref, y_ref,
                      st_scr, sa_scr, rows_scr, prevr_scr, prevv_scr, *, tt):
    d = pl.program_id(0)
    n = HEAD_DIM

    @pl.when(pl.program_id(1) == 0)
    def _():
        st_scr[...] = jnp.zeros_like(st_scr)
        sa_scr[...] = jnp.zeros_like(sa_scr)
        prevr_scr[...] = jnp.zeros_like(prevr_scr)
        prevv_scr[...] = jnp.zeros_like(prevv_scr)

    pack = rows_scr.shape[-2]
    mu_r, mu_v = mur_ref[0], muv_ref[0]

    def row(slot, q, k):
        return jnp.broadcast_to(rows_scr[slot, q, k // pack, pl.ds(k % pack, 1), :],
                                sa_scr.shape[1:])

    def step(j, t, a_next, sa, gam, pr_prev, pv_prev):
        slot = j % 2
        gam = gam * w_ref[0, t]
        inv = 1.0 / gam
        pr, pv = pr_ref[t].astype(F32), pv_ref[t]
        r = pr + (pr_prev - pr) * mu_r
        v = pv + (pv_prev - pv) * mu_v
        rows_scr[slot, 0] = b_ref[0, t].astype(F32) * inv
        rows_scr[slot, 1] = k_ref[0, t].astype(F32) * inv
        rows_scr[slot, 2] = r * gam
        rows_scr[slot, 3] = a_next.astype(F32) * gam
        y = jnp.zeros_like(v)
        sa_next = [jnp.zeros_like(v), jnp.zeros_like(v)]
        for k in range(n):
            new = st_scr[k] + (sa * row(slot, 0, k) + v * row(slot, 1, k))
            st_scr[k] = new
            y = y + new * row(slot, 2, k)
            sa_next[k % 2] = sa_next[k % 2] + new * row(slot, 3, k)
        y_ref[0, t] = y
        return sa_next[0] + sa_next[1], gam, pr, pv

    def body(j, carry):
        t = j + d * (tt - 1 - 2 * j)
        return step(j, t, a_ref[0, t + 1 - 2 * d], *carry)

    carry = (sa_scr[...], jnp.ones(rows_scr.shape[2:], F32), prevr_scr[...], prevv_scr[...])
    carry = lax.fori_loop(0, tt - 1, body, carry)
    sa, gam, pr, pv = step(tt - 1, (1 - d) * (tt - 1), an_ref[0, 0], *carry)
    sa_scr[...] = sa
    prevr_scr[...] = pr
    prevv_scr[...] = pv
    rows_scr[0, 0] = gam
    for k in range(n):
        st_scr[k] = st_scr[k] * row(0, 0, k)


def _rwkv_scan(pr, w, k, pv, a, b, mu_r, mu_v, *, tt=32):
    s, n, c = pv.shape
    nt = s // tt
    state_tile = (n // SUBLANES, SUBLANES, c)
    vec_tile = (n // (2 * SUBLANES), 2 * SUBLANES, c)
    chunk = lambda d, i: (d, i + d * (nt - 1 - 2 * i), 0, 0, 0)
    shared = lambda d, i: (i + d * (nt - 1 - 2 * i), 0, 0, 0)
    vec_spec = pl.BlockSpec((1, tt) + vec_tile, chunk)
    state_spec = pl.BlockSpec((1, tt) + state_tile, chunk)

    def next_first_step(d, i):
        fwd = jnp.minimum((i + 1) * tt, s - 1)
        bwd = jnp.maximum((nt - 1 - i) * tt - 1, 0)
        return (d, jnp.where(d == 0, fwd, bwd), 0, 0, 0)

    w, k, a, b = (t.reshape((2, s) + vec_tile) for t in (w, k, a, b))
    y = pl.pallas_call(
        functools.partial(_rwkv_scan_kernel, tt=tt),
        grid=(2, nt),
        in_specs=[pl.BlockSpec((tt,) + vec_tile, shared), vec_spec, vec_spec,
                  pl.BlockSpec((tt,) + state_tile, shared), vec_spec, vec_spec,
                  pl.BlockSpec((1, 1) + vec_tile, next_first_step),
                  pl.BlockSpec((1,) + vec_tile, lambda d, i: (d, 0, 0, 0)),
                  pl.BlockSpec((1,) + state_tile, lambda d, i: (d, 0, 0, 0))],
        out_specs=state_spec,
        out_shape=jax.ShapeDtypeStruct((2, s) + state_tile, F32),
        scratch_shapes=[pltpu.VMEM((n,) + state_tile, F32), pltpu.VMEM(state_tile, F32),
                        pltpu.VMEM((2, 4) + vec_tile, F32),
                        pltpu.VMEM(vec_tile, F32), pltpu.VMEM(state_tile, F32)],
        compiler_params=_cparams(("parallel", "arbitrary")),
        name="rwkv_scan",
    )(pr.reshape((s,) + vec_tile), w, k, pv.reshape((s,) + state_tile), a, b, a,
      mu_r.reshape((2,) + vec_tile), mu_v.reshape((2,) + state_tile))
    return y.reshape(2, s, n, c)


def _to_chain_lanes(t):
    *lead, b, s, _ = t.shape
    k = len(lead)
    t = t.reshape(*lead, b, s, N_HEADS, HEAD_DIM)
    t = t.transpose(*range(k), k + 1, k + 3, k, k + 2)
    return t.reshape(*lead, s, HEAD_DIM, b * N_HEADS)


def _param_to_chain_lanes(p, b):
    lead = p.shape[:-1]
    p = jnp.swapaxes(p.reshape(*lead, N_HEADS, HEAD_DIM), -1, -2)
    return jnp.broadcast_to(p[..., None, :], lead + (HEAD_DIM, b, N_HEADS)).reshape(
        *lead, HEAD_DIM, b * N_HEADS)


def _from_chain_lanes(t, b):
    _, s, _, _ = t.shape
    t = t.reshape(2, s, HEAD_DIM, b, N_HEADS).transpose(0, 3, 1, 4, 2)
    return t.reshape(2, b, s, W_MIX)


def _merge_kernel(y_ref, bonus_ref, l_ref, gz0_ref, gz1_ref, gz2_ref, oa_ref, ob_ref, x_ref, gt_ref,
                  gnw_ref, gnb_ref, ones_ref, gc_ref, wb_ref, wo_ref, o_ref):
    ones_bd = ones_ref[...]
    inv_n = 1.0 / HEAD_DIM
    acc = bonus_ref[0]
    for d in range(2):
        y = y_ref[d, 0]
        mean = _seg_sum(y, ones_bd) * inv_n
        yc = y - mean
        var = _seg_sum(yc * yc, ones_bd) * inv_n
        acc = acc + (yc * lax.rsqrt(var + GN_EPS)) * gnw_ref[...] + gnb_ref[...]
    g = jnp.dot(_sigmoid(l_ref[0]).astype(BF16), gc_ref[...], preferred_element_type=F32)
    o_c = acc * g
    merged = (_sigmoid(gz0_ref[0])
              * jnp.dot(oa_ref[0], wb_ref[0], preferred_element_type=F32)
              + _sigmoid(gz1_ref[0])
              * jnp.dot(ob_ref[0], wb_ref[1], preferred_element_type=F32)
              + _sigmoid(gz2_ref[0])
              * jnp.dot(o_c.astype(BF16), wb_ref[2], preferred_element_type=F32))
    out = jnp.dot(merged.astype(BF16), wo_ref[...], preferred_element_type=F32)
    o_ref[0] = x_ref[0] + gt_ref[0] * out


def _merge(y, bonus, proj, o_a, o_b, x, gate, gn_w, gn_b, ones_bd, g_comb, w_branch, w_out, *, tm=512):
    b, s, d = x.shape
    gz_per = d // COLBLK
    tok = lambda width, cb: pl.BlockSpec((1, tm, width), lambda bi, i, cb=cb: (bi, i, cb))
    const = lambda shape: pl.BlockSpec(shape, lambda bi, i: (0,) * len(shape))
    return pl.pallas_call(
        _merge_kernel,
        grid=(b, s // tm),
        in_specs=[pl.BlockSpec((2, 1, tm, COLBLK), lambda bi, i: (0, bi, i, 0)),
                  tok(COLBLK, 0),
                  tok(COLBLK, CB_LORA),
                  tok(d, CB_GZ // gz_per), tok(d, CB_GZ // gz_per + 1), tok(d, CB_GZ // gz_per + 2),
                  tok(COLBLK, 0), tok(COLBLK, 0), tok(d, 0),
                  pl.BlockSpec((1, 1, d), lambda bi, i: (bi, 0, 0)),
                  const((1, COLBLK)), const((1, COLBLK)), const((COLBLK, COLBLK)),
                  const((COLBLK, COLBLK)), const((3, COLBLK, d)), const((d, d))],
        out_specs=pl.BlockSpec((1, tm, d), lambda bi, i: (bi, i, 0)),
        out_shape=jax.ShapeDtypeStruct((b, s, d), F32),
        compiler_params=_cparams(("parallel", "parallel")),
        name="merge_out",
    )(y, bonus, proj, proj, proj, proj, o_a, o_b, x, gate, gn_w, gn_b, ones_bd, g_comb, w_branch, w_out)


def _split_w_in(w_in):
    n_lora = 4 * DECAY_LORA + GATE_LORA
    n_rest = N_ATTN + CB_LORA * COLBLK + n_lora
    rest, gates = w_in[..., N_ATTN:n_rest], w_in[..., n_rest:]
    zpad = jnp.zeros(w_in.shape[:-1] + (COLBLK - n_lora,), w_in.dtype)
    return (w_in[..., :N_ATTN].astype(BF16),
            jnp.concatenate([rest, zpad, gates], axis=-1).astype(BF16))


def _lora_params(mu_w, mu_a, w_up, a_up, g_up):
    r = DECAY_LORA
    depth = mu_w.shape[0]
    mu_l = jnp.zeros((depth, 2, COLBLK), F32)
    w_comb = jnp.zeros((depth, 2, 2 * LANES, 2 * COLBLK), F32)
    for d in range(2):
        wcol, acol = d * r, 2 * r + d * r
        mu_l = mu_l.at[:, d, wcol:wcol + r].set(mu_w[:, d]).at[:, d, acol:acol + r].set(mu_a[:, d])
        w_comb = w_comb.at[:, d, wcol:wcol + r, :COLBLK].set(w_up[:, d])
        w_comb = w_comb.at[:, d, acol:acol + r, COLBLK:].set(a_up[:, d])
    g_comb = jnp.zeros((depth, COLBLK, COLBLK), F32).at[:, 4 * r:4 * r + GATE_LORA].set(g_up)
    return mu_l, w_comb.astype(BF16), g_comb.astype(BF16)


def _head_block_ones():
    seg = np.arange(COLBLK) // HEAD_DIM
    return jnp.asarray(seg[:, None] == seg[None, :], dtype=BF16)


def kernel(x, c, positions, ada_w, ada_b, norm_gains, ffn_wi, ffn_wo, w_in, rpb, mu_rkv, mu_w, mu_a,
           w0, w_up, a0, a_up, g_up, k_k, k_a, r_k, gn_w, gn_b, w_branch, w_out, final_norm):
    depth = ada_w.shape[0]
    b, s, d = x.shape
    assert d == 2 * COLBLK and s % 1024 == 0 and s % (GRID_W * NA_WIN_ROWS) == 0

    mod = _modulation(c, ada_w, ada_b).reshape(depth, b, N_MOD, 1, d)
    cos, s1, s2 = _rope_tables(positions)
    ones_bd = _head_block_ones()
    bias_tabs = _natten_bias_tables(rpb, s // GRID_W)
    dil_masks = _dilated_masks(s)
    ffn_wi_bf, ffn_wo_bf = ffn_wi.astype(BF16), ffn_wo.astype(BF16)
    w_branch_bf, w_out_bf = w_branch.astype(BF16), w_out.astype(BF16)
    w_attn, w_rest = _split_w_in(w_in)
    mu_l, w_comb, g_comb = _lora_params(mu_w, mu_a, w_up, a_up, g_up)
    mu_r, mu_v = _param_to_chain_lanes(mu_rkv[:, :, 0], b), _param_to_chain_lanes(mu_rkv[:, :, 2], b)
    row = lambda t: t.reshape(1, -1)

    for l in range(depth):
        sh1, sc1, gt1, sh2, sc2, gt2, sh3, sc3, gt3 = (mod[l, :, i] for i in range(N_MOD))

        x = _ffn(x, norm_gains[l, 0], sh1, sc1, gt1, ffn_wi_bf[l, 0], ffn_wo_bf[l, 0])

        attn, proj = _in_proj(x, norm_gains[l, 1], sh2, sc2, w_attn[l], w_rest[l])
        o_a = _dilated_attention(attn, cos, s1, s2, dil_masks)
        o_b = _neighborhood_attention(attn, bias_tabs[l])
        decay, k2, a_vec, b_vec, bonus, pr, pv = _rwkv_pre(
            proj, mu_rkv[l], mu_l[l], w_comb[l], w0[l], a0[l], row(k_k[l]), row(k_a[l]), row(r_k[l]),
            ones_bd)
        y = _rwkv_scan(_to_chain_lanes(pr), _to_chain_lanes(decay), _to_chain_lanes(k2),
                       _to_chain_lanes(pv), _to_chain_lanes(a_vec), _to_chain_lanes(b_vec),
                       mu_r[l], mu_v[l])
        y = _from_chain_lanes(y, b)
        x = _merge(y, bonus, proj, o_a, o_b, x, gt2, row(gn_w[l]), row(gn_b[l]), ones_bd, g_comb[l],
                   w_branch_bf[l], w_out_bf[l])

        x = _ffn(x, norm_gains[l, 2], sh3, sc3, gt3, ffn_wi_bf[l, 1], ffn_wo_bf[l, 1],
                 final_norm if l == depth - 1 else None)
    return x
```

```python
import functools

import numpy as np
import jax
import jax.numpy as jnp
from jax import lax
from jax.experimental import pallas as pl
from jax.experimental.pallas import tpu as pltpu

F32 = jnp.float32
BF16 = jnp.bfloat16

HEAD_DIM = 64
N_HEADS = 8
W_MIX = N_HEADS * HEAD_DIM
DILATED_PAIRS = ((128, 1), (512, 4), (2048, 16))
QBLK = 128
ROT_DIM = HEAD_DIM // 4
ROPE_THETA = 500000.0
GRID_W = 64
NA_WIN_ROWS = 8
NA_WIN_COLS = 16
DECAY_LORA = 64
ICLR_LORA = 64
GATE_LORA = 128
DECAY_SCALE = 0.6065306597126334
N_MOD = 9
RMS_EPS = 1e-6
GN_EPS = 64e-5
NEG = -1e30

LANES = 128
SUBLANES = 8
VMEM_LIMIT = 48 * 1024 * 1024
DILATED_GROUP = 4
NATTEN_GROUP = 8
PACK_TILE = 256

COLBLK = 512
CB_AQ, CB_AK, CB_AV = 0, 1, 2
CB_NQ, CB_NK, CB_NV = 3, 4, 5
N_ATTN = 6 * COLBLK
CB_PR, CB_PK, CB_PV = 0, 1, 2
CB_LORA = 3
CB_GZ = 4


def _cparams(sem):
    return pltpu.CompilerParams(dimension_semantics=sem, vmem_limit_bytes=VMEM_LIMIT)


def _sigmoid(x):
    return 0.5 * jnp.tanh(0.5 * x) + 0.5


def _mod_kernel(c_ref, w_ref, b_ref, o_ref):
    c = c_ref[...]
    cs = (c * _sigmoid(c)).astype(BF16)
    o_ref[0] = jnp.dot(cs, w_ref[0].astype(BF16), preferred_element_type=F32) + b_ref[0]


def _modulation(c, ada_w, ada_b):
    depth, d, nd = ada_w.shape
    b = c.shape[0]
    return pl.pallas_call(
        _mod_kernel,
        grid=(depth, nd // d),
        in_specs=[pl.BlockSpec((b, d), lambda l, j: (0, 0)),
                  pl.BlockSpec((1, d, d), lambda l, j: (l, 0, j)),
                  pl.BlockSpec((1, 1, d), lambda l, j: (l, 0, j))],
        out_specs=pl.BlockSpec((1, b, d), lambda l, j: (l, 0, j)),
        out_shape=jax.ShapeDtypeStruct((depth, b, nd), F32),
        compiler_params=_cparams(("parallel", "parallel")),
        name="adaln_mod",
    )(c, ada_w, ada_b.reshape(depth, 1, nd))


def _norm_mod(x, gain, shift, scale):
    ms = jnp.mean(x * x, axis=-1, keepdims=True)
    y = x * lax.rsqrt(ms + RMS_EPS) * gain
    return y * (1.0 + scale) + shift


def _ffn_kernel(*refs, final, tf):
    if final:
        x_ref, g_ref, sh_ref, sc_ref, gt_ref, wi_ref, wo_ref, fin_ref, o_ref = refs
    else:
        x_ref, g_ref, sh_ref, sc_ref, gt_ref, wi_ref, wo_ref, o_ref = refs
    x = x_ref[0]
    h = _norm_mod(x, g_ref[...], sh_ref[0], sc_ref[0]).astype(BF16)
    dff = wo_ref.shape[0]
    acc = None
    for lo in range(0, dff, tf):
        gate = jnp.dot(h, wi_ref[:, lo:lo + tf], preferred_element_type=F32)
        up = jnp.dot(h, wi_ref[:, dff + lo:dff + lo + tf], preferred_element_type=F32)
        act = ((gate * _sigmoid(gate)) * up).astype(BF16)
        part = jnp.dot(act, wo_ref[lo:lo + tf, :], preferred_element_type=F32)
        acc = part if acc is None else acc + part
    y = x + 0.5 * gt_ref[0] * acc
    if final:
        ms = jnp.mean(y * y, axis=-1, keepdims=True)
        y = y * lax.rsqrt(ms + RMS_EPS) * fin_ref[...]
    o_ref[0] = y


def _ffn(x, gain, shift, scale, gate, wi, wo, final_gain=None, *, tm=512, tf=256):
    b, s, d = x.shape
    dff = wo.shape[0]
    final = final_gain is not None
    vec = pl.BlockSpec((1, 1, d), lambda bi, i: (bi, 0, 0))
    resident = lambda shape: pl.BlockSpec(shape, lambda bi, i: (0, 0), pipeline_mode=pl.Buffered(1))
    in_specs = [pl.BlockSpec((1, tm, d), lambda bi, i: (bi, i, 0)),
                pl.BlockSpec((1, d), lambda bi, i: (0, 0)),
                vec, vec, vec,
                resident((d, 2 * dff)), resident((dff, d))]
    args = [x, gain.reshape(1, d), shift, scale, gate, wi, wo]
    if final:
        in_specs.append(pl.BlockSpec((1, d), lambda bi, i: (0, 0)))
        args.append(final_gain.reshape(1, d))
    return pl.pallas_call(
        functools.partial(_ffn_kernel, final=final, tf=tf),
        grid=(b, s // tm),
        in_specs=in_specs,
        out_specs=pl.BlockSpec((1, tm, d), lambda bi, i: (bi, i, 0)),
        out_shape=jax.ShapeDtypeStruct((b, s, d), F32),
        compiler_params=_cparams(("parallel", "parallel")),
        name="ffn_final" if final else "ffn",
    )(*args)


def _win_kernel(x_ref, g_ref, sh_ref, sc_ref, wa_ref, wr_ref, oa_ref, or_ref, h_scr):
    @pl.when(pl.program_id(2) == 0)
    def _():
        h = _norm_mod(x_ref[0], g_ref[...], sh_ref[0], sc_ref[0])
        h_scr[...] = h.astype(BF16)

    h = h_scr[...]
    oa_ref[0] = jnp.dot(h, wa_ref[...], preferred_element_type=F32).astype(BF16)
    or_ref[0] = jnp.dot(h, wr_ref[...], preferred_element_type=F32)


def _in_proj(x, gain, shift, scale, w_attn, w_rest, *, tm=1024, nj=4):
    b, s, d = x.shape
    ta, tr = w_attn.shape[1] // nj, w_rest.shape[1] // nj
    vec = pl.BlockSpec((1, 1, d), lambda bi, i, j: (bi, 0, 0))
    return pl.pallas_call(
        _win_kernel,
        grid=(b, s // tm, nj),
        in_specs=[pl.BlockSpec((1, tm, d), lambda bi, i, j: (bi, i, 0)),
                  pl.BlockSpec((1, d), lambda bi, i, j: (0, 0)),
                  vec, vec,
                  pl.BlockSpec((d, ta), lambda bi, i, j: (0, j)),
                  pl.BlockSpec((d, tr), lambda bi, i, j: (0, j))],
        out_specs=[pl.BlockSpec((1, tm, ta), lambda bi, i, j: (bi, i, j)),
                   pl.BlockSpec((1, tm, tr), lambda bi, i, j: (bi, i, j))],
        out_shape=[jax.ShapeDtypeStruct((b, s, w_attn.shape[1]), BF16),
                   jax.ShapeDtypeStruct((b, s, w_rest.shape[1]), F32)],
        scratch_shapes=[pltpu.VMEM((tm, d), BF16)],
        compiler_params=_cparams(("parallel", "parallel", "arbitrary")),
        name="in_proj",
    )(x, gain.reshape(1, d), shift, scale, w_attn, w_rest)


def _dilated_kernel(q_ref, k_ref, v_ref, cos_ref, s1_ref, s2_ref, *rest, seq, pad):
    n_pat = len(DILATED_PAIRS)
    mask_refs = rest[:n_pat]
    o_ref, q_scr, k_scr, v_scr, og_scr, lse_scr = rest[n_pat:]
    cos, s1, s2 = cos_ref[0], s1_ref[0], s2_ref[0]

    def rotary(t):
        return (t * cos + pltpu.roll(t, LANES - ROT_DIM // 2, axis=1) * s1
                + pltpu.roll(t, ROT_DIM // 2, axis=1) * s2)

    q_scr[...] = rotary(q_ref[0].astype(F32)) * (HEAD_DIM ** -0.5)
    zeros = jnp.zeros((pad, LANES), F32)
    k_scr[pl.ds(0, pad), :] = zeros
    k_scr[pl.ds(pad + seq, pad), :] = zeros
    v_scr[pl.ds(0, pad), :] = zeros
    v_scr[pl.ds(pad + seq, pad), :] = zeros
    k_scr[pl.ds(pad, seq), :] = rotary(k_ref[0].astype(F32))
    v_scr[pl.ds(pad, seq), :] = v_ref[0].astype(F32)

    lane = lax.broadcasted_iota(jnp.int32, (QBLK, LANES), 1)
    head0 = lane < HEAD_DIM

    for g, (window, dil) in enumerate(DILATED_PAIRS):
        nblk, lead, nkeys = _dilated_geometry(seq, window, dil)

        def group(gi, carry, g=g, dil=dil, lead=lead, nkeys=nkeys, nblk=nblk):
            units = []
            for u in range(DILATED_GROUP):
                idx = gi * DILATED_GROUP + u
                cls = idx // nblk
                n = idx % nblk
                q_start = cls + dil * QBLK * n
                k_start = pad + cls + dil * (QBLK * n - lead)
                if dil == 1:
                    q_rows = pl.ds(q_start, QBLK)
                    k_rows = pl.ds(k_start, nkeys)
                else:
                    q_rows = pl.ds(q_start, QBLK, stride=dil)
                    k_rows = pl.ds(k_start, nkeys, stride=dil)
                which = jnp.where(n == 0, 0, jnp.where(n == nblk - 1, 2, 1)) if nblk > 1 else 0
                units.append((q_rows, q_scr[q_rows, :], k_scr[k_rows, :].astype(BF16),
                              v_scr[k_rows, :].astype(BF16), which))
            scores = []
            for _, q, kw, _, which in units:
                for hmask in (head0, jnp.logical_not(head0)):
                    qh = jnp.where(hmask, q, 0.0).astype(BF16)
                    s = lax.dot_general(qh, kw, (((1,), (1,)), ((), ())),
                                        preferred_element_type=F32)
                    scores.append(s + mask_refs[g][which])
            probs, invs, lses = [], [], []
            for s in scores:
                m = jnp.max(s, axis=-1, keepdims=True)
                p = jnp.exp(s - m)
                den = jnp.sum(p, axis=-1, keepdims=True)
                probs.append(p.astype(BF16))
                invs.append(1.0 / den)
                lses.append(m + jnp.log(den))
            for u, (q_rows, _, _, vw, _) in enumerate(units):
                o0 = jnp.dot(probs[2 * u], vw, preferred_element_type=F32) * invs[2 * u]
                o1 = jnp.dot(probs[2 * u + 1], vw, preferred_element_type=F32) * invs[2 * u + 1]
                og_scr[g, q_rows, :] = jnp.where(head0, o0, o1)
                lse_scr[g, q_rows, :] = jnp.where(head0, lses[2 * u], lses[2 * u + 1])
            return carry

        lax.fori_loop(0, dil * nblk // DILATED_GROUP, group, 0)

    l0, l1, l2 = lse_scr[0], lse_scr[1], lse_scr[2]
    m = jnp.maximum(jnp.maximum(l0, l1), l2)
    e0, e1, e2 = jnp.exp(l0 - m), jnp.exp(l1 - m), jnp.exp(l2 - m)
    out = (e0 * og_scr[0] + e1 * og_scr[1] + e2 * og_scr[2]) * (1.0 / (e0 + e1 + e2))
    o_ref[0] = out.astype(o_ref.dtype)


def _dilated_geometry(seq, window, dil):
    radius = window // (2 * dil)
    nblk = seq // dil // QBLK
    lead = radius if nblk > 1 else 0
    return nblk, lead, QBLK + 2 * lead


def _dilated_masks(seq):
    out = []
    for window, dil in DILATED_PAIRS:
        radius = window // (2 * dil)
        nblk, lead, nkeys = _dilated_geometry(seq, window, dil)
        qi = np.arange(QBLK)[:, None]
        kj = np.arange(nkeys)[None, :]
        band = np.abs(kj - lead - qi) <= radius
        tabs = []
        for n in ((0, 1, nblk - 1) if nblk > 1 else (0,)):
            key_pos = QBLK * n + kj - lead
            tabs.append(band & (key_pos >= 0) & (key_pos < nblk * QBLK))
        out.append(jnp.asarray(np.where(np.stack(tabs), 0.0, NEG), F32))
    return out


def _dilated_attention(attn, cos, s1, s2, masks):
    b, s, _ = attn.shape
    pad = max((w // (2 * d)) * d for w, d in DILATED_PAIRS)
    hp = W_MIX // LANES
    per_cb = COLBLK // LANES
    col = lambda cb: pl.BlockSpec((1, s, LANES), lambda bi, h, cb=cb: (bi, 0, cb * per_cb + h))
    tab = pl.BlockSpec((1, s, LANES), lambda bi, h: (bi, 0, 0))
    const = lambda shape: pl.BlockSpec(shape, lambda bi, h: (0,) * len(shape))
    return pl.pallas_call(
        functools.partial(_dilated_kernel, seq=s, pad=pad),
        grid=(b, hp),
        in_specs=[col(CB_AQ), col(CB_AK), col(CB_AV), tab, tab, tab] + [const(m.shape) for m in masks],
        out_specs=pl.BlockSpec((1, s, LANES), lambda bi, h: (bi, 0, h)),
        out_shape=jax.ShapeDtypeStruct((b, s, W_MIX), BF16),
        scratch_shapes=[pltpu.VMEM((s, LANES), F32),
                        pltpu.VMEM((s + 2 * pad, LANES), F32),
                        pltpu.VMEM((s + 2 * pad, LANES), F32),
                        pltpu.VMEM((len(DILATED_PAIRS), s, LANES), F32),
                        pltpu.VMEM((len(DILATED_PAIRS), s, LANES), F32)],
        compiler_params=_cparams(("parallel", "parallel")),
        name="dilated_attn",
    )(attn, attn, attn, cos, s1, s2, *masks)


def _rope_tables(positions):
    half = ROT_DIM // 2
    lane = np.arange(LANES) % HEAD_DIM
    inv_freq = ROPE_THETA ** (-jnp.asarray(lane % half, F32) * 2.0 / ROT_DIM)
    ang = positions.astype(F32)[..., None] * inv_freq
    cos, sin = jnp.cos(ang), jnp.sin(ang)
    return (jnp.where(lane < ROT_DIM, cos, 1.0), jnp.where(lane < half, -sin, 0.0),
            jnp.where((lane >= half) & (lane < ROT_DIM), sin, 0.0))


def _natten_kernel(q_ref, k_ref, v_ref, bias_ref, o_ref, *, rows, win_rows):
    lane = lax.broadcasted_iota(jnp.int32, (GRID_W, LANES), 1)
    head0 = lane < HEAD_DIM
    scale = HEAD_DIM ** -0.5
    nkeys = win_rows * GRID_W

    def group(gi, carry):
        units = []
        for u in range(NATTEN_GROUP):
            r = gi * NATTEN_GROUP + u
            r_start = jnp.clip(r - win_rows // 2, 0, rows - win_rows)
            q_rows = pl.ds(pl.multiple_of(r * GRID_W, GRID_W), GRID_W)
            k_rows = pl.ds(pl.multiple_of(r_start * GRID_W, GRID_W), nkeys)
            units.append((q_rows, r - r_start, q_ref[0, q_rows, :],
                          k_ref[0, k_rows, :], v_ref[0, k_rows, :]))
        scores = []
        for _, delta, q, kw, _ in units:
            q = q * scale
            zero = jnp.zeros_like(q)
            q2 = jnp.concatenate([jnp.where(head0, q, zero), jnp.where(head0, zero, q)], axis=0)
            s2 = lax.dot_general(q2, kw, (((1,), (1,)), ((), ())), preferred_element_type=F32)
            scores += [s2[:GRID_W] + bias_ref[0, delta], s2[GRID_W:] + bias_ref[1, delta]]
        probs, invs = [], []
        for s in scores:
            m = jnp.max(s, axis=-1, keepdims=True)
            p = jnp.exp(s - m)
            probs.append(p.astype(BF16))
            invs.append(1.0 / jnp.sum(p, axis=-1, keepdims=True))
        for u, (q_rows, _, _, _, vw) in enumerate(units):
            o2 = jnp.dot(jnp.concatenate(probs[2 * u:2 * u + 2], axis=0), vw,
                         preferred_element_type=F32)
            o_ref[0, q_rows, :] = jnp.where(head0, o2[:GRID_W] * invs[2 * u],
                                            o2[GRID_W:] * invs[2 * u + 1]).astype(o_ref.dtype)
        return carry

    lax.fori_loop(0, rows // NATTEN_GROUP, group, 0)


def _natten_bias_tables(rpb, rows):
    wr = min(NA_WIN_ROWS, rows)
    wc = NA_WIN_COLS
    cols = np.arange(GRID_W)
    c_start = np.clip(cols - wc // 2, 0, GRID_W - wc)
    col_in = (cols[None, :] >= c_start[:, None]) & (cols[None, :] < c_start[:, None] + wc)
    coff = np.clip(cols[None, :] - cols[:, None], -(wc - 1), wc - 1) + wc - 1
    onehot = (coff[None] == np.arange(2 * wc - 1)[:, None, None]).astype(np.float32)
    by_col = jnp.einsum("lhrc,cqk->lhqrk", rpb.astype(F32), onehot, precision=lax.Precision.HIGHEST)
    by_col = jnp.where(col_in[:, None, :], by_col, NEG)
    top = NA_WIN_ROWS - 1
    tabs = jnp.stack([by_col[:, :, :, top - dl:top - dl + wr] for dl in range(wr)], axis=2)
    return tabs.reshape(rpb.shape[0], rpb.shape[1], wr, GRID_W, wr * GRID_W)


def _neighborhood_attention(attn, bias_tab):
    b, s, _ = attn.shape
    rows = s // GRID_W
    wr = min(NA_WIN_ROWS, rows)
    hp = W_MIX // LANES
    per_cb = COLBLK // LANES
    col = lambda cb: pl.BlockSpec((1, s, LANES), lambda bi, h, cb=cb: (bi, 0, cb * per_cb + h))
    return pl.pallas_call(
        functools.partial(_natten_kernel, rows=rows, win_rows=wr),
        grid=(b, hp),
        in_specs=[col(CB_NQ), col(CB_NK), col(CB_NV),
                  pl.BlockSpec((LANES // HEAD_DIM, wr, GRID_W, wr * GRID_W),
                               lambda bi, h: (h, 0, 0, 0))],
        out_specs=pl.BlockSpec((1, s, LANES), lambda bi, h: (bi, 0, h)),
        out_shape=jax.ShapeDtypeStruct((b, s, W_MIX), BF16),
        compiler_params=_cparams(("parallel", "parallel")),
        name="natten",
    )(attn, attn, attn, bias_tab)


def _seg_sum(x, ones_bd):
    return jnp.dot(x.astype(BF16), ones_bd, preferred_element_type=F32)


def _pack_rows(x):
    half = x.shape[0] // 2
    bits = lax.bitcast_convert_type(x.astype(BF16).astype(F32), jnp.uint32)
    return (bits[:half] >> 16) | (bits[half:] & jnp.uint32(0xFFFF0000))


def _unpack_rows(u, high):
    bits = jnp.where(high, u & jnp.uint32(0xFFFF0000), u << 16)
    return lax.bitcast_convert_type(bits, F32)


def _rwkv_pre_kernel(r_ref, k_ref, v_ref, l_ref,
                     rp_ref, kp_ref, vp_ref, lp_ref,
                     rn_ref, kn_ref, vn_ref, ln_ref,
                     mu_ref, mul_ref, wc_ref, w0_ref, a0_ref, kk_ref, ka_ref, rk_ref, ones_ref,
                     w_o, k_o, a_o, b_o, bonus_o, pr_o, pv_o, *, tm):
    i = pl.program_id(1)
    first = i == 0
    last = i == pl.num_programs(1) - 1
    row = lax.broadcasted_iota(jnp.int32, (tm, COLBLK), 0)
    ones_bd = ones_ref[...]

    def neighbours(x_ref, p_ref, n_ref):
        x = x_ref[0]
        prev_row = jnp.where(first, 0.0, p_ref[0, SUBLANES - 1:SUBLANES, :])
        next_row = jnp.where(last, 0.0, n_ref[0, 0:1, :])
        prev = jnp.where(row == 0, prev_row, pltpu.roll(x, 1, axis=0))
        nxt = jnp.where(row == tm - 1, next_row, pltpu.roll(x, tm - 1, axis=0))
        return x, (prev, nxt)

    pr, nb_r = neighbours(r_ref, rp_ref, rn_ref)
    pk, nb_k = neighbours(k_ref, kp_ref, kn_ref)
    pv, nb_v = neighbours(v_ref, vp_ref, vn_ref)
    pl_, nb_l = neighbours(l_ref, lp_ref, ln_ref)

    bonus = jnp.zeros((tm, COLBLK), F32)
    for d in range(2):
        r = pr + (nb_r[d] - pr) * mu_ref[d, 0:1, :]
        k = pk + (nb_k[d] - pk) * mu_ref[d, 1:2, :]
        v = pv + (nb_v[d] - pv) * mu_ref[d, 2:3, :]
        xl = pl_ + (nb_l[d] - pl_) * mul_ref[d:d + 1, :]
        zw = jnp.dot(jnp.tanh(xl[:, :LANES]).astype(BF16), wc_ref[d, :LANES, :],
                     preferred_element_type=F32)
        za = jnp.dot(xl[:, LANES:2 * LANES].astype(BF16), wc_ref[d, LANES:2 * LANES, :],
                     preferred_element_type=F32)
        z = zw + za
        wz = w0_ref[d:d + 1, :] + z[:, :COLBLK]
        az = a0_ref[d:d + 1, :] + z[:, COLBLK:]
        decay = jnp.exp(-DECAY_SCALE * _sigmoid(wz))
        a = _sigmoid(az)
        kk = k * kk_ref[...]
        kk = kk * lax.rsqrt(jnp.maximum(_seg_sum(kk * kk, ones_bd), 1e-24))
        k2 = k * (1.0 + (a - 1.0) * ka_ref[...])
        bonus = bonus + _seg_sum(r * k2 * rk_ref[...], ones_bd) * v
        w_o[d, 0] = decay
        k_o[d, 0] = _pack_rows(k2)
        a_o[d, 0] = _pack_rows(-kk)
        b_o[d, 0] = _pack_rows(kk * a)
    bonus_o[0] = bonus
    pr_o[0] = _pack_rows(pr)
    pv_o[0] = pv


def _rwkv_pre(proj, mu_rkv, mu_lora, w_comb, w0, a0, k_k, k_a, r_k, ones_bd, *, tm=PACK_TILE):
    b, s, _ = proj.shape
    nsub = tm // SUBLANES
    last_sub = s // SUBLANES - 1
    main = lambda cb: pl.BlockSpec((1, tm, COLBLK), lambda bi, i, cb=cb: (bi, i, cb))
    prev = lambda cb: pl.BlockSpec(
        (1, SUBLANES, COLBLK), lambda bi, i, cb=cb: (bi, jnp.maximum(i * nsub - 1, 0), cb))
    nxt = lambda cb: pl.BlockSpec(
        (1, SUBLANES, COLBLK), lambda bi, i, cb=cb: (bi, jnp.minimum((i + 1) * nsub, last_sub), cb))
    cbs = (CB_PR, CB_PK, CB_PV, CB_LORA)
    const = lambda shape: pl.BlockSpec(shape, lambda bi, i: (0,) * len(shape))
    tok = lambda rows: pl.BlockSpec((1, rows, COLBLK), lambda bi, i: (bi, i, 0))
    dir_out = lambda rows: pl.BlockSpec((2, 1, rows, COLBLK), lambda bi, i: (0, bi, i, 0))
    u32 = jnp.uint32
    return pl.pallas_call(
        functools.partial(_rwkv_pre_kernel, tm=tm),
        grid=(b, s // tm),
        in_specs=([main(cb) for cb in cbs] + [prev(cb) for cb in cbs] + [nxt(cb) for cb in cbs]
                  + [const((2, 3, COLBLK)), const((2, COLBLK)), const((2, 2 * LANES, 2 * COLBLK)),
                     const((2, COLBLK)), const((2, COLBLK)), const((1, COLBLK)), const((1, COLBLK)),
                     const((1, COLBLK)), const((COLBLK, COLBLK))]),
        out_specs=[dir_out(tm), dir_out(tm // 2), dir_out(tm // 2), dir_out(tm // 2),
                   tok(tm), tok(tm // 2), tok(tm)],
        out_shape=[jax.ShapeDtypeStruct((2, b, s, COLBLK), F32)]
                  + [jax.ShapeDtypeStruct((2, b, s // 2, COLBLK), u32)] * 3
                  + [jax.ShapeDtypeStruct((b, s, COLBLK), F32),
                     jax.ShapeDtypeStruct((b, s // 2, COLBLK), u32),
                     jax.ShapeDtypeStruct((b, s, COLBLK), F32)],
        compiler_params=_cparams(("parallel", "parallel")),
        name="rwkv_pre",
    )(*([proj] * 12), mu_rkv, mu_lora, w_comb, w0, a0, k_k, k_a, r_k, ones_bd)


def _rwkv_scan_kernel(pr_ref, w_ref, k_ref, pv_ref, a_ref, b_ref, an_ref, mur_ref, muv_ref, y_ref,
                      st_scr, sa_scr, rows_scr, prevr_scr, prevv_scr, *, tt, seq):
    d = pl.program_id(0)
    i = pl.program_id(1)
    n = HEAD_DIM
    nt = seq // tt
    chunks_per_tile = PACK_TILE // tt
    chunk = i + d * (nt - 1 - 2 * i)
    high = chunk % chunks_per_tile >= chunks_per_tile // 2
    t_after = jnp.where(d == 0, (i + 1) * tt, (nt - 1 - i) * tt - 1)
    high_after = t_after % PACK_TILE >= PACK_TILE // 2

    @pl.when(i == 0)
    def _():
        st_scr[...] = jnp.zeros_like(st_scr)
        sa_scr[...] = jnp.zeros_like(sa_scr)
        prevr_scr[...] = jnp.zeros_like(prevr_scr)
        prevv_scr[...] = jnp.zeros_like(prevv_scr)

    mu_r, mu_v = mur_ref[0], muv_ref[0]

    def row(slot, q, k):
        return jnp.broadcast_to(rows_scr[slot, q, k // SUBLANES, pl.ds(k % SUBLANES, 1), :],
                                sa_scr.shape[1:])

    def step(j, t, a_next, sa, gam, pr_prev, pv_prev):
        slot = j % 2
        gam = gam * w_ref[0, t]
        inv = 1.0 / gam
        pr, pv = _unpack_rows(pr_ref[t], high), pv_ref[t]
        r = pr + (pr_prev - pr) * mu_r
        v = pv + (pv_prev - pv) * mu_v
        rows_scr[slot, 0] = _unpack_rows(b_ref[0, t], high) * inv
        rows_scr[slot, 1] = _unpack_rows(k_ref[0, t], high) * inv
        rows_scr[slot, 2] = r * gam
        rows_scr[slot, 3] = a_next * gam
        y = jnp.zeros_like(v)
        sa_next = [jnp.zeros_like(v), jnp.zeros_like(v)]
        for k in range(n):
            new = st_scr[k] + (sa * row(slot, 0, k) + v * row(slot, 1, k))
            st_scr[k] = new
            y = y + new * row(slot, 2, k)
            sa_next[k % 2] = sa_next[k % 2] + new * row(slot, 3, k)
        y_ref[0, t] = y
        return sa_next[0] + sa_next[1], gam, pr, pv

    def body(j, carry):
        t = j + d * (tt - 1 - 2 * j)
        return step(j, t, _unpack_rows(a_ref[0, t + 1 - 2 * d], high), *carry)

    carry = (sa_scr[...], jnp.ones(sa_scr.shape, F32), prevr_scr[...], prevv_scr[...])
    carry = lax.fori_loop(0, tt - 1, body, carry)
    sa, gam, pr, pv = step(tt - 1, (1 - d) * (tt - 1), _unpack_rows(an_ref[0, 0], high_after), *carry)
    sa_scr[...] = sa
    prevr_scr[...] = pr
    prevv_scr[...] = pv
    rows_scr[0, 0] = gam
    for k in range(n):
        st_scr[k] = st_scr[k] * row(0, 0, k)


def _rwkv_scan(pr, w, k, pv, a, b, mu_r, mu_v, *, tt=32):
    s, n, c = pv.shape
    nt = s // tt
    cpt = PACK_TILE // tt
    tile = (n // SUBLANES, SUBLANES, c)
    order = lambda d, i: i + d * (nt - 1 - 2 * i)
    packed = lambda ci: (ci // cpt) * (cpt // 2) + ci % (cpt // 2)
    dir_spec = pl.BlockSpec((1, tt) + tile, lambda d, i: (d, order(d, i), 0, 0, 0))
    shared_spec = pl.BlockSpec((tt,) + tile, lambda d, i: (order(d, i), 0, 0, 0))
    dir_packed = pl.BlockSpec((1, tt) + tile, lambda d, i: (d, packed(order(d, i)), 0, 0, 0))
    shared_packed = pl.BlockSpec((tt,) + tile, lambda d, i: (packed(order(d, i)), 0, 0, 0))
    per_dir = pl.BlockSpec((1,) + tile, lambda d, i: (d, 0, 0, 0))

    def token_after(d, i):
        t = jnp.where(d == 0, jnp.minimum((i + 1) * tt, s - 1), jnp.maximum((nt - 1 - i) * tt - 1, 0))
        return (d, (t // PACK_TILE) * (PACK_TILE // 2) + t % (PACK_TILE // 2), 0, 0, 0)

    k, a, b = (t.reshape((2, s // 2) + tile) for t in (k, a, b))
    y = pl.pallas_call(
        functools.partial(_rwkv_scan_kernel, tt=tt, seq=s),
        grid=(2, nt),
        in_specs=[shared_packed, dir_spec, dir_packed, shared_spec, dir_packed, dir_packed,
                  pl.BlockSpec((1, 1) + tile, token_after), per_dir, per_dir],
        out_specs=dir_spec,
        out_shape=jax.ShapeDtypeStruct((2, s) + tile, F32),
        scratch_shapes=[pltpu.VMEM((n,) + tile, F32), pltpu.VMEM(tile, F32),
                        pltpu.VMEM((2, 4) + tile, F32), pltpu.VMEM(tile, F32), pltpu.VMEM(tile, F32)],
        compiler_params=_cparams(("parallel", "arbitrary")),
        name="rwkv_scan",
    )(pr.reshape((s // 2,) + tile), w.reshape((2, s) + tile), k, pv.reshape((s,) + tile), a, b, a,
      mu_r.reshape((2,) + tile), mu_v.reshape((2,) + tile))
    return y.reshape(2, s, n, c)


def _to_chain_lanes(t):
    *lead, b, s, _ = t.shape
    k = len(lead)
    t = t.reshape(*lead, b, s, N_HEADS, HEAD_DIM)
    t = t.transpose(*range(k), k + 1, k + 3, k, k + 2)
    return t.reshape(*lead, s, HEAD_DIM, b * N_HEADS)


def _param_to_chain_lanes(p, b):
    lead = p.shape[:-1]
    p = jnp.swapaxes(p.reshape(*lead, N_HEADS, HEAD_DIM), -1, -2)
    return jnp.broadcast_to(p[..., None, :], lead + (HEAD_DIM, b, N_HEADS)).reshape(
        *lead, HEAD_DIM, b * N_HEADS)


def _from_chain_lanes(t, b):
    _, s, _, _ = t.shape
    t = t.reshape(2, s, HEAD_DIM, b, N_HEADS).transpose(0, 3, 1, 4, 2)
    return t.reshape(2, b, s, W_MIX)


def _merge_kernel(y_ref, bonus_ref, l_ref, gz0_ref, gz1_ref, gz2_ref, oa_ref, ob_ref, x_ref, gt_ref,
                  gnw_ref, gnb_ref, ones_ref, gc_ref, wb_ref, wo_ref, o_ref):
    ones_bd = ones_ref[...]
    inv_n = 1.0 / HEAD_DIM
    acc = bonus_ref[0]
    for d in range(2):
        y = y_ref[d, 0]
        mean = _seg_sum(y, ones_bd) * inv_n
        yc = y - mean
        var = _seg_sum(yc * yc, ones_bd) * inv_n
        acc = acc + (yc * lax.rsqrt(var + GN_EPS)) * gnw_ref[...] + gnb_ref[...]
    g = jnp.dot(_sigmoid(l_ref[0]).astype(BF16), gc_ref[...], preferred_element_type=F32)
    o_c = acc * g
    merged = (_sigmoid(gz0_ref[0])
              * jnp.dot(oa_ref[0], wb_ref[0], preferred_element_type=F32)
              + _sigmoid(gz1_ref[0])
              * jnp.dot(ob_ref[0], wb_ref[1], preferred_element_type=F32)
              + _sigmoid(gz2_ref[0])
              * jnp.dot(o_c.astype(BF16), wb_ref[2], preferred_element_type=F32))
    out = jnp.dot(merged.astype(BF16), wo_ref[...], preferred_element_type=F32)
    o_ref[0] = x_ref[0] + gt_ref[0] * out


def _merge(y, bonus, proj, o_a, o_b, x, gate, gn_w, gn_b, ones_bd, g_comb, w_branch, w_out, *, tm=512):
    b, s, d = x.shape
    gz_per = d // COLBLK
    tok = lambda width, cb: pl.BlockSpec((1, tm, width), lambda bi, i, cb=cb: (bi, i, cb))
    const = lambda shape: pl.BlockSpec(shape, lambda bi, i: (0,) * len(shape))
    return pl.pallas_call(
        _merge_kernel,
        grid=(b, s // tm),
        in_specs=[pl.BlockSpec((2, 1, tm, COLBLK), lambda bi, i: (0, bi, i, 0)),
                  tok(COLBLK, 0),
                  tok(COLBLK, CB_LORA),
                  tok(d, CB_GZ // gz_per), tok(d, CB_GZ // gz_per + 1), tok(d, CB_GZ // gz_per + 2),
                  tok(COLBLK, 0), tok(COLBLK, 0), tok(d, 0),
                  pl.BlockSpec((1, 1, d), lambda bi, i: (bi, 0, 0)),
                  const((1, COLBLK)), const((1, COLBLK)), const((COLBLK, COLBLK)),
                  const((COLBLK, COLBLK)), const((3, COLBLK, d)), const((d, d))],
        out_specs=pl.BlockSpec((1, tm, d), lambda bi, i: (bi, i, 0)),
        out_shape=jax.ShapeDtypeStruct((b, s, d), F32),
        compiler_params=_cparams(("parallel", "parallel")),
        name="merge_out",
    )(y, bonus, proj, proj, proj, proj, o_a, o_b, x, gate, gn_w, gn_b, ones_bd, g_comb, w_branch, w_out)


def _split_w_in(w_in):
    n_lora = 4 * DECAY_LORA + GATE_LORA
    n_rest = N_ATTN + CB_LORA * COLBLK + n_lora
    rest, gates = w_in[..., N_ATTN:n_rest], w_in[..., n_rest:]
    zpad = jnp.zeros(w_in.shape[:-1] + (COLBLK - n_lora,), w_in.dtype)
    return (w_in[..., :N_ATTN].astype(BF16),
            jnp.concatenate([rest, zpad, gates], axis=-1).astype(BF16))


def _lora_params(mu_w, mu_a, w_up, a_up, g_up):
    r = DECAY_LORA
    depth = mu_w.shape[0]
    mu_l = jnp.zeros((depth, 2, COLBLK), F32)
    w_comb = jnp.zeros((depth, 2, 2 * LANES, 2 * COLBLK), F32)
    for d in range(2):
        wcol, acol = d * r, 2 * r + d * r
        mu_l = mu_l.at[:, d, wcol:wcol + r].set(mu_w[:, d]).at[:, d, acol:acol + r].set(mu_a[:, d])
        w_comb = w_comb.at[:, d, wcol:wcol + r, :COLBLK].set(w_up[:, d])
        w_comb = w_comb.at[:, d, acol:acol + r, COLBLK:].set(a_up[:, d])
    g_comb = jnp.zeros((depth, COLBLK, COLBLK), F32).at[:, 4 * r:4 * r + GATE_LORA].set(g_up)
    return mu_l, w_comb.astype(BF16), g_comb.astype(BF16)


def _head_block_ones():
    seg = np.arange(COLBLK) // HEAD_DIM
    return jnp.asarray(seg[:, None] == seg[None, :], dtype=BF16)


def kernel(x, c, positions, ada_w, ada_b, norm_gains, ffn_wi, ffn_wo, w_in, rpb, mu_rkv, mu_w, mu_a,
           w0, w_up, a0, a_up, g_up, k_k, k_a, r_k, gn_w, gn_b, w_branch, w_out, final_norm):
    depth = ada_w.shape[0]
    b, s, d = x.shape
    assert d == 2 * COLBLK and s % 1024 == 0 and s % (GRID_W * NA_WIN_ROWS) == 0

    mod = _modulation(c, ada_w, ada_b).reshape(depth, b, N_MOD, 1, d)
    cos, s1, s2 = _rope_tables(positions)
    ones_bd = _head_block_ones()
    bias_tabs = _natten_bias_tables(rpb, s // GRID_W)
    dil_masks = _dilated_masks(s)
    ffn_wi_bf, ffn_wo_bf = ffn_wi.astype(BF16), ffn_wo.astype(BF16)
    w_branch_bf, w_out_bf = w_branch.astype(BF16), w_out.astype(BF16)
    w_attn, w_rest = _split_w_in(w_in)
    mu_l, w_comb, g_comb = _lora_params(mu_w, mu_a, w_up, a_up, g_up)
    mu_r, mu_v = _param_to_chain_lanes(mu_rkv[:, :, 0], b), _param_to_chain_lanes(mu_rkv[:, :, 2], b)
    row = lambda t: t.reshape(1, -1)

    for l in range(depth):
        sh1, sc1, gt1, sh2, sc2, gt2, sh3, sc3, gt3 = (mod[l, :, i] for i in range(N_MOD))

        x = _ffn(x, norm_gains[l, 0], sh1, sc1, gt1, ffn_wi_bf[l, 0], ffn_wo_bf[l, 0])

        attn, proj = _in_proj(x, norm_gains[l, 1], sh2, sc2, w_attn[l], w_rest[l])
        o_a = _dilated_attention(attn, cos, s1, s2, dil_masks)
        o_b = _neighborhood_attention(attn, bias_tabs[l])
        decay, k2, a_vec, b_vec, bonus, pr, pv = _rwkv_pre(
            proj, mu_rkv[l], mu_l[l], w_comb[l], w0[l], a0[l], row(k_k[l]), row(k_a[l]), row(r_k[l]),
            ones_bd)
        y = _rwkv_scan(_to_chain_lanes(pr), _to_chain_lanes(decay), _to_chain_lanes(k2),
                       _to_chain_lanes(pv), _to_chain_lanes(a_vec), _to_chain_lanes(b_vec),
                       mu_r[l], mu_v[l])
        y = _from_chain_lanes(y, b)
        x = _merge(y, bonus, proj, o_a, o_b, x, gt2, row(gn_w[l]), row(gn_b[l]), ones_bd, g_comb[l],
                   w_branch_bf[l], w_out_bf[l])

        x = _ffn(x, norm_gains[l, 2], sh3, sc3, gt3, ffn_wi_bf[l, 1], ffn_wo_bf[l, 1],
                 final_norm if l == depth - 1 else None)
    return x
```

```python
import functools

import numpy as np
import jax
import jax.numpy as jnp
from jax import lax
from jax.experimental import pallas as pl
from jax.experimental.pallas import tpu as pltpu

F32 = jnp.float32
BF16 = jnp.bfloat16

HEAD_DIM = 64
N_HEADS = 8
W_MIX = N_HEADS * HEAD_DIM
DILATED_PAIRS = ((128, 1), (512, 4), (2048, 16))
QBLK = 128
ROT_DIM = HEAD_DIM // 4
ROPE_THETA = 500000.0
GRID_W = 64
NA_WIN_ROWS = 8
NA_WIN_COLS = 16
DECAY_LORA = 64
ICLR_LORA = 64
GATE_LORA = 128
DECAY_SCALE = 0.6065306597126334
N_MOD = 9
RMS_EPS = 1e-6
GN_EPS = 64e-5
NEG = -1e30

LANES = 128
SUBLANES = 8
VMEM_LIMIT = 48 * 1024 * 1024
DILATED_GROUP = 4
NATTEN_GROUP = 16
PACK_TILE = 256

COLBLK = 512
CB_AQ, CB_AK, CB_AV = 0, 1, 2
CB_NQ, CB_NK, CB_NV = 3, 4, 5
N_ATTN = 6 * COLBLK
CB_PR, CB_PK, CB_PV = 0, 1, 2
CB_LORA = 3
CB_GZ = 4


def _cparams(sem):
    return pltpu.CompilerParams(dimension_semantics=sem, vmem_limit_bytes=VMEM_LIMIT)


def _sigmoid(x):
    return 0.5 * jnp.tanh(0.5 * x) + 0.5


def _mod_kernel(c_ref, w_ref, b_ref, o_ref):
    c = c_ref[...]
    cs = (c * _sigmoid(c)).astype(BF16)
    o_ref[0] = jnp.dot(cs, w_ref[0].astype(BF16), preferred_element_type=F32) + b_ref[0]


def _modulation(c, ada_w, ada_b):
    depth, d, nd = ada_w.shape
    b = c.shape[0]
    return pl.pallas_call(
        _mod_kernel,
        grid=(depth, nd // d),
        in_specs=[pl.BlockSpec((b, d), lambda l, j: (0, 0)),
                  pl.BlockSpec((1, d, d), lambda l, j: (l, 0, j)),
                  pl.BlockSpec((1, 1, d), lambda l, j: (l, 0, j))],
        out_specs=pl.BlockSpec((1, b, d), lambda l, j: (l, 0, j)),
        out_shape=jax.ShapeDtypeStruct((depth, b, nd), F32),
        compiler_params=_cparams(("parallel", "parallel")),
        name="adaln_mod",
    )(c, ada_w, ada_b.reshape(depth, 1, nd))


def _norm_mod(x, gain, shift, scale):
    ms = jnp.mean(x * x, axis=-1, keepdims=True)
    y = x * lax.rsqrt(ms + RMS_EPS) * gain
    return y * (1.0 + scale) + shift


def _ffn_kernel(*refs, final, tf):
    if final:
        x_ref, g_ref, sh_ref, sc_ref, gt_ref, wi_ref, wo_ref, fin_ref, o_ref = refs
    else:
        x_ref, g_ref, sh_ref, sc_ref, gt_ref, wi_ref, wo_ref, o_ref = refs
    x = x_ref[0]
    h = _norm_mod(x, g_ref[...], sh_ref[0], sc_ref[0]).astype(BF16)
    dff = wo_ref.shape[0]
    acc = None
    for lo in range(0, dff, tf):
        gate = jnp.dot(h, wi_ref[:, lo:lo + tf], preferred_element_type=F32)
        up = jnp.dot(h, wi_ref[:, dff + lo:dff + lo + tf], preferred_element_type=F32)
        act = ((gate * _sigmoid(gate)) * up).astype(BF16)
        part = jnp.dot(act, wo_ref[lo:lo + tf, :], preferred_element_type=F32)
        acc = part if acc is None else acc + part
    y = x + 0.5 * gt_ref[0] * acc
    if final:
        ms = jnp.mean(y * y, axis=-1, keepdims=True)
        y = y * lax.rsqrt(ms + RMS_EPS) * fin_ref[...]
    o_ref[0] = y


def _ffn(x, gain, shift, scale, gate, wi, wo, final_gain=None, *, tm=512, tf=256):
    b, s, d = x.shape
    dff = wo.shape[0]
    final = final_gain is not None
    vec = pl.BlockSpec((1, 1, d), lambda bi, i: (bi, 0, 0))
    resident = lambda shape: pl.BlockSpec(shape, lambda bi, i: (0, 0), pipeline_mode=pl.Buffered(1))
    in_specs = [pl.BlockSpec((1, tm, d), lambda bi, i: (bi, i, 0)),
                pl.BlockSpec((1, d), lambda bi, i: (0, 0)),
                vec, vec, vec,
                resident((d, 2 * dff)), resident((dff, d))]
    args = [x, gain.reshape(1, d), shift, scale, gate, wi, wo]
    if final:
        in_specs.append(pl.BlockSpec((1, d), lambda bi, i: (0, 0)))
        args.append(final_gain.reshape(1, d))
    return pl.pallas_call(
        functools.partial(_ffn_kernel, final=final, tf=tf),
        grid=(b, s // tm),
        in_specs=in_specs,
        out_specs=pl.BlockSpec((1, tm, d), lambda bi, i: (bi, i, 0)),
        out_shape=jax.ShapeDtypeStruct((b, s, d), F32),
        compiler_params=_cparams(("parallel", "parallel")),
        name="ffn_final" if final else "ffn",
    )(*args)


def _win_kernel(x_ref, g_ref, sh_ref, sc_ref, wa_ref, wr_ref, oa_ref, or_ref, h_scr):
    @pl.when(pl.program_id(2) == 0)
    def _():
        h = _norm_mod(x_ref[0], g_ref[...], sh_ref[0], sc_ref[0])
        h_scr[...] = h.astype(BF16)

    h = h_scr[...]
    oa_ref[0] = jnp.dot(h, wa_ref[...], preferred_element_type=F32).astype(BF16)
    or_ref[0] = jnp.dot(h, wr_ref[...], preferred_element_type=F32)


def _in_proj(x, gain, shift, scale, w_attn, w_rest, *, tm=1024, nj=4):
    b, s, d = x.shape
    ta, tr = w_attn.shape[1] // nj, w_rest.shape[1] // nj
    vec = pl.BlockSpec((1, 1, d), lambda bi, i, j: (bi, 0, 0))
    return pl.pallas_call(
        _win_kernel,
        grid=(b, s // tm, nj),
        in_specs=[pl.BlockSpec((1, tm, d), lambda bi, i, j: (bi, i, 0)),
                  pl.BlockSpec((1, d), lambda bi, i, j: (0, 0)),
                  vec, vec,
                  pl.BlockSpec((d, ta), lambda bi, i, j: (0, j)),
                  pl.BlockSpec((d, tr), lambda bi, i, j: (0, j))],
        out_specs=[pl.BlockSpec((1, tm, ta), lambda bi, i, j: (bi, i, j)),
                   pl.BlockSpec((1, tm, tr), lambda bi, i, j: (bi, i, j))],
        out_shape=[jax.ShapeDtypeStruct((b, s, w_attn.shape[1]), BF16),
                   jax.ShapeDtypeStruct((b, s, w_rest.shape[1]), F32)],
        scratch_shapes=[pltpu.VMEM((tm, d), BF16)],
        compiler_params=_cparams(("parallel", "parallel", "arbitrary")),
        name="in_proj",
    )(x, gain.reshape(1, d), shift, scale, w_attn, w_rest)


def _dilated_kernel(q_ref, k_ref, v_ref, cos_ref, s1_ref, s2_ref, *rest, seq, pad):
    n_pat = len(DILATED_PAIRS)
    mask_refs = rest[:n_pat]
    o_ref, q_scr, k_scr, v_scr, og_scr, lse_scr = rest[n_pat:]
    cos, s1, s2 = cos_ref[0], s1_ref[0], s2_ref[0]

    def rotary(t):
        return (t * cos + pltpu.roll(t, LANES - ROT_DIM // 2, axis=1) * s1
                + pltpu.roll(t, ROT_DIM // 2, axis=1) * s2)

    q_scr[...] = rotary(q_ref[0].astype(F32)) * (HEAD_DIM ** -0.5)
    zeros = jnp.zeros((pad, LANES), F32)
    k_scr[pl.ds(0, pad), :] = zeros
    k_scr[pl.ds(pad + seq, pad), :] = zeros
    v_scr[pl.ds(0, pad), :] = zeros
    v_scr[pl.ds(pad + seq, pad), :] = zeros
    k_scr[pl.ds(pad, seq), :] = rotary(k_ref[0].astype(F32))
    v_scr[pl.ds(pad, seq), :] = v_ref[0].astype(F32)

    lane = lax.broadcasted_iota(jnp.int32, (QBLK, LANES), 1)
    head0 = lane < HEAD_DIM

    for g, (window, dil) in enumerate(DILATED_PAIRS):
        nblk, lead, nkeys = _dilated_geometry(seq, window, dil)

        def group(gi, carry, g=g, dil=dil, lead=lead, nkeys=nkeys, nblk=nblk):
            units = []
            for u in range(DILATED_GROUP):
                idx = gi * DILATED_GROUP + u
                cls = idx // nblk
                n = idx % nblk
                q_start = cls + dil * QBLK * n
                k_start = pad + cls + dil * (QBLK * n - lead)
                if dil == 1:
                    q_rows = pl.ds(q_start, QBLK)
                    k_rows = pl.ds(k_start, nkeys)
                else:
                    q_rows = pl.ds(q_start, QBLK, stride=dil)
                    k_rows = pl.ds(k_start, nkeys, stride=dil)
                which = jnp.where(n == 0, 0, jnp.where(n == nblk - 1, 2, 1)) if nblk > 1 else 0
                units.append((q_rows, q_scr[q_rows, :], k_scr[k_rows, :].astype(BF16),
                              v_scr[k_rows, :].astype(BF16), which))
            scores = []
            for _, q, kw, _, which in units:
                for hmask in (head0, jnp.logical_not(head0)):
                    qh = jnp.where(hmask, q, 0.0).astype(BF16)
                    s = lax.dot_general(qh, kw, (((1,), (1,)), ((), ())),
                                        preferred_element_type=F32)
                    scores.append(s + mask_refs[g][which])
            probs, invs, lses = [], [], []
            for s in scores:
                m = jnp.max(s, axis=-1, keepdims=True)
                p = jnp.exp(s - m)
                den = jnp.sum(p, axis=-1, keepdims=True)
                probs.append(p.astype(BF16))
                invs.append(1.0 / den)
                lses.append(m + jnp.log(den))
            for u, (q_rows, _, _, vw, _) in enumerate(units):
                o0 = jnp.dot(probs[2 * u], vw, preferred_element_type=F32) * invs[2 * u]
                o1 = jnp.dot(probs[2 * u + 1], vw, preferred_element_type=F32) * invs[2 * u + 1]
                og_scr[g, q_rows, :] = jnp.where(head0, o0, o1)
                lse_scr[g, q_rows, :] = jnp.where(head0, lses[2 * u], lses[2 * u + 1])
            return carry

        lax.fori_loop(0, dil * nblk // DILATED_GROUP, group, 0)

    l0, l1, l2 = lse_scr[0], lse_scr[1], lse_scr[2]
    m = jnp.maximum(jnp.maximum(l0, l1), l2)
    e0, e1, e2 = jnp.exp(l0 - m), jnp.exp(l1 - m), jnp.exp(l2 - m)
    out = (e0 * og_scr[0] + e1 * og_scr[1] + e2 * og_scr[2]) * (1.0 / (e0 + e1 + e2))
    o_ref[0] = out.astype(o_ref.dtype)


def _dilated_geometry(seq, window, dil):
    radius = window // (2 * dil)
    nblk = seq // dil // QBLK
    lead = radius if nblk > 1 else 0
    return nblk, lead, QBLK + 2 * lead


def _dilated_masks(seq):
    out = []
    for window, dil in DILATED_PAIRS:
        radius = window // (2 * dil)
        nblk, lead, nkeys = _dilated_geometry(seq, window, dil)
        qi = np.arange(QBLK)[:, None]
        kj = np.arange(nkeys)[None, :]
        band = np.abs(kj - lead - qi) <= radius
        tabs = []
        for n in ((0, 1, nblk - 1) if nblk > 1 else (0,)):
            key_pos = QBLK * n + kj - lead
            tabs.append(band & (key_pos >= 0) & (key_pos < nblk * QBLK))
        out.append(jnp.asarray(np.where(np.stack(tabs), 0.0, NEG), F32))
    return out


def _dilated_attention(attn, cos, s1, s2, masks):
    b, s, _ = attn.shape
    pad = max((w // (2 * d)) * d for w, d in DILATED_PAIRS)
    hp = W_MIX // LANES
    per_cb = COLBLK // LANES
    col = lambda cb: pl.BlockSpec((1, s, LANES), lambda bi, h, cb=cb: (bi, 0, cb * per_cb + h))
    tab = pl.BlockSpec((1, s, LANES), lambda bi, h: (bi, 0, 0))
    const = lambda shape: pl.BlockSpec(shape, lambda bi, h: (0,) * len(shape))
    return pl.pallas_call(
        functools.partial(_dilated_kernel, seq=s, pad=pad),
        grid=(b, hp),
        in_specs=[col(CB_AQ), col(CB_AK), col(CB_AV), tab, tab, tab] + [const(m.shape) for m in masks],
        out_specs=pl.BlockSpec((1, s, LANES), lambda bi, h: (bi, 0, h)),
        out_shape=jax.ShapeDtypeStruct((b, s, W_MIX), BF16),
        scratch_shapes=[pltpu.VMEM((s, LANES), F32),
                        pltpu.VMEM((s + 2 * pad, LANES), F32),
                        pltpu.VMEM((s + 2 * pad, LANES), F32),
                        pltpu.VMEM((len(DILATED_PAIRS), s, LANES), F32),
                        pltpu.VMEM((len(DILATED_PAIRS), s, LANES), F32)],
        compiler_params=_cparams(("parallel", "parallel")),
        name="dilated_attn",
    )(attn, attn, attn, cos, s1, s2, *masks)


def _rope_tables(positions):
    half = ROT_DIM // 2
    lane = np.arange(LANES) % HEAD_DIM
    inv_freq = ROPE_THETA ** (-jnp.asarray(lane % half, F32) * 2.0 / ROT_DIM)
    ang = positions.astype(F32)[..., None] * inv_freq
    cos, sin = jnp.cos(ang), jnp.sin(ang)
    return (jnp.where(lane < ROT_DIM, cos, 1.0), jnp.where(lane < half, -sin, 0.0),
            jnp.where((lane >= half) & (lane < ROT_DIM), sin, 0.0))


def _natten_kernel(q_ref, k_ref, v_ref, bias_ref, o_ref, *, rows, win_rows):
    lane = lax.broadcasted_iota(jnp.int32, (GRID_W, LANES), 1)
    head0 = lane < HEAD_DIM
    scale = HEAD_DIM ** -0.5
    nkeys = win_rows * GRID_W

    def group(gi, carry):
        units = []
        for u in range(NATTEN_GROUP):
            r = gi * NATTEN_GROUP + u
            r_start = jnp.clip(r - win_rows // 2, 0, rows - win_rows)
            q_rows = pl.ds(pl.multiple_of(r * GRID_W, GRID_W), GRID_W)
            k_rows = pl.ds(pl.multiple_of(r_start * GRID_W, GRID_W), nkeys)
            units.append((q_rows, r - r_start, q_ref[0, q_rows, :],
                          k_ref[0, k_rows, :], v_ref[0, k_rows, :]))
        scores = []
        for _, delta, q, kw, _ in units:
            q = q * scale
            zero = jnp.zeros_like(q)
            q2 = jnp.concatenate([jnp.where(head0, q, zero), jnp.where(head0, zero, q)], axis=0)
            s2 = lax.dot_general(q2, kw, (((1,), (1,)), ((), ())), preferred_element_type=F32)
            scores += [s2[:GRID_W] + bias_ref[0, delta], s2[GRID_W:] + bias_ref[1, delta]]
        probs, invs = [], []
        for s in scores:
            m = jnp.max(s, axis=-1, keepdims=True)
            p = jnp.exp(s - m)
            probs.append(p.astype(BF16))
            invs.append(1.0 / jnp.sum(p, axis=-1, keepdims=True))
        for u, (q_rows, _, _, _, vw) in enumerate(units):
            o2 = jnp.dot(jnp.concatenate(probs[2 * u:2 * u + 2], axis=0), vw,
                         preferred_element_type=F32)
            o_ref[0, q_rows, :] = jnp.where(head0, o2[:GRID_W] * invs[2 * u],
                                            o2[GRID_W:] * invs[2 * u + 1]).astype(o_ref.dtype)
        return carry

    lax.fori_loop(0, rows // NATTEN_GROUP, group, 0)


def _natten_bias_tables(rpb, rows):
    wr = min(NA_WIN_ROWS, rows)
    wc = NA_WIN_COLS
    cols = np.arange(GRID_W)
    c_start = np.clip(cols - wc // 2, 0, GRID_W - wc)
    col_in = (cols[None, :] >= c_start[:, None]) & (cols[None, :] < c_start[:, None] + wc)
    coff = np.clip(cols[None, :] - cols[:, None], -(wc - 1), wc - 1) + wc - 1
    onehot = (coff[None] == np.arange(2 * wc - 1)[:, None, None]).astype(np.float32)
    by_col = jnp.einsum("lhrc,cqk->lhqrk", rpb.astype(F32), onehot, precision=lax.Precision.HIGHEST)
    by_col = jnp.where(col_in[:, None, :], by_col, NEG)
    top = NA_WIN_ROWS - 1
    tabs = jnp.stack([by_col[:, :, :, top - dl:top - dl + wr] for dl in range(wr)], axis=2)
    return tabs.reshape(rpb.shape[0], rpb.shape[1], wr, GRID_W, wr * GRID_W)


def _neighborhood_attention(attn, bias_tab):
    b, s, _ = attn.shape
    rows = s // GRID_W
    wr = min(NA_WIN_ROWS, rows)
    hp = W_MIX // LANES
    per_cb = COLBLK // LANES
    col = lambda cb: pl.BlockSpec((1, s, LANES), lambda bi, h, cb=cb: (bi, 0, cb * per_cb + h))
    return pl.pallas_call(
        functools.partial(_natten_kernel, rows=rows, win_rows=wr),
        grid=(b, hp),
        in_specs=[col(CB_NQ), col(CB_NK), col(CB_NV),
                  pl.BlockSpec((LANES // HEAD_DIM, wr, GRID_W, wr * GRID_W),
                               lambda bi, h: (h, 0, 0, 0))],
        out_specs=pl.BlockSpec((1, s, LANES), lambda bi, h: (bi, 0, h)),
        out_shape=jax.ShapeDtypeStruct((b, s, W_MIX), BF16),
        compiler_params=_cparams(("parallel", "parallel")),
        name="natten",
    )(attn, attn, attn, bias_tab)


def _seg_sum(x, ones_bd):
    return jnp.dot(x.astype(BF16), ones_bd, preferred_element_type=F32)


def _pack_rows(x):
    half = x.shape[0] // 2
    bits = lax.bitcast_convert_type(x.astype(BF16).astype(F32), jnp.uint32)
    return (bits[:half] >> 16) | (bits[half:] & jnp.uint32(0xFFFF0000))


def _unpack_rows(u, high):
    bits = jnp.where(high, u & jnp.uint32(0xFFFF0000), u << 16)
    return lax.bitcast_convert_type(bits, F32)


def _rwkv_pre_kernel(r_ref, k_ref, v_ref, l_ref,
                     rp_ref, kp_ref, vp_ref, lp_ref,
                     rn_ref, kn_ref, vn_ref, ln_ref,
                     mu_ref, mul_ref, wc_ref, w0_ref, a0_ref, kk_ref, ka_ref, rk_ref, ones_ref,
                     w_o, k_o, a_o, b_o, bonus_o, pr_o, pv_o, *, tm):
    i = pl.program_id(1)
    first = i == 0
    last = i == pl.num_programs(1) - 1
    row = lax.broadcasted_iota(jnp.int32, (tm, COLBLK), 0)
    ones_bd = ones_ref[...]

    def neighbours(x_ref, p_ref, n_ref):
        x = x_ref[0]
        prev_row = jnp.where(first, 0.0, p_ref[0, SUBLANES - 1:SUBLANES, :])
        next_row = jnp.where(last, 0.0, n_ref[0, 0:1, :])
        prev = jnp.where(row == 0, prev_row, pltpu.roll(x, 1, axis=0))
        nxt = jnp.where(row == tm - 1, next_row, pltpu.roll(x, tm - 1, axis=0))
        return x, (prev, nxt)

    pr, nb_r = neighbours(r_ref, rp_ref, rn_ref)
    pk, nb_k = neighbours(k_ref, kp_ref, kn_ref)
    pv, nb_v = neighbours(v_ref, vp_ref, vn_ref)
    pl_, nb_l = neighbours(l_ref, lp_ref, ln_ref)

    bonus = jnp.zeros((tm, COLBLK), F32)
    for d in range(2):
        r = pr + (nb_r[d] - pr) * mu_ref[d, 0:1, :]
        k = pk + (nb_k[d] - pk) * mu_ref[d, 1:2, :]
        v = pv + (nb_v[d] - pv) * mu_ref[d, 2:3, :]
        xl = pl_ + (nb_l[d] - pl_) * mul_ref[d:d + 1, :]
        zw = jnp.dot(jnp.tanh(xl[:, :LANES]).astype(BF16), wc_ref[d, :LANES, :],
                     preferred_element_type=F32)
        za = jnp.dot(xl[:, LANES:2 * LANES].astype(BF16), wc_ref[d, LANES:2 * LANES, :],
                     preferred_element_type=F32)
        z = zw + za
        wz = w0_ref[d:d + 1, :] + z[:, :COLBLK]
        az = a0_ref[d:d + 1, :] + z[:, COLBLK:]
        decay = jnp.exp(-DECAY_SCALE * _sigmoid(wz))
        a = _sigmoid(az)
        kk = k * kk_ref[...]
        kk = kk * lax.rsqrt(jnp.maximum(_seg_sum(kk * kk, ones_bd), 1e-24))
        k2 = k * (1.0 + (a - 1.0) * ka_ref[...])
        bonus = bonus + _seg_sum(r * k2 * rk_ref[...], ones_bd) * v
        w_o[d, 0] = decay
        k_o[d, 0] = _pack_rows(k2)
        a_o[d, 0] = _pack_rows(-kk)
        b_o[d, 0] = _pack_rows(kk * a)
    bonus_o[0] = bonus
    pr_o[0] = _pack_rows(pr)
    pv_o[0] = pv


def _rwkv_pre(proj, mu_rkv, mu_lora, w_comb, w0, a0, k_k, k_a, r_k, ones_bd, *, tm=PACK_TILE):
    b, s, _ = proj.shape
    nsub = tm // SUBLANES
    last_sub = s // SUBLANES - 1
    main = lambda cb: pl.BlockSpec((1, tm, COLBLK), lambda bi, i, cb=cb: (bi, i, cb))
    prev = lambda cb: pl.BlockSpec(
        (1, SUBLANES, COLBLK), lambda bi, i, cb=cb: (bi, jnp.maximum(i * nsub - 1, 0), cb))
    nxt = lambda cb: pl.BlockSpec(
        (1, SUBLANES, COLBLK), lambda bi, i, cb=cb: (bi, jnp.minimum((i + 1) * nsub, last_sub), cb))
    cbs = (CB_PR, CB_PK, CB_PV, CB_LORA)
    const = lambda shape: pl.BlockSpec(shape, lambda bi, i: (0,) * len(shape))
    tok = lambda rows: pl.BlockSpec((1, rows, COLBLK), lambda bi, i: (bi, i, 0))
    dir_out = lambda rows: pl.BlockSpec((2, 1, rows, COLBLK), lambda bi, i: (0, bi, i, 0))
    u32 = jnp.uint32
    return pl.pallas_call(
        functools.partial(_rwkv_pre_kernel, tm=tm),
        grid=(b, s // tm),
        in_specs=([main(cb) for cb in cbs] + [prev(cb) for cb in cbs] + [nxt(cb) for cb in cbs]
                  + [const((2, 3, COLBLK)), const((2, COLBLK)), const((2, 2 * LANES, 2 * COLBLK)),
                     const((2, COLBLK)), const((2, COLBLK)), const((1, COLBLK)), const((1, COLBLK)),
                     const((1, COLBLK)), const((COLBLK, COLBLK))]),
        out_specs=[dir_out(tm), dir_out(tm // 2), dir_out(tm // 2), dir_out(tm // 2),
                   tok(tm), tok(tm // 2), tok(tm)],
        out_shape=[jax.ShapeDtypeStruct((2, b, s, COLBLK), F32)]
                  + [jax.ShapeDtypeStruct((2, b, s // 2, COLBLK), u32)] * 3
                  + [jax.ShapeDtypeStruct((b, s, COLBLK), F32),
                     jax.ShapeDtypeStruct((b, s // 2, COLBLK), u32),
                     jax.ShapeDtypeStruct((b, s, COLBLK), F32)],
        compiler_params=_cparams(("parallel", "parallel")),
        name="rwkv_pre",
    )(*([proj] * 12), mu_rkv, mu_lora, w_comb, w0, a0, k_k, k_a, r_k, ones_bd)


def _rwkv_scan_kernel(pr_ref, w_ref, k_ref, pv_ref, a_ref, b_ref, an_ref, mur_ref, muv_ref, y_ref,
                      st_scr, sa_scr, rows_scr, prevr_scr, prevv_scr, *, tt, seq):
    d = pl.program_id(0)
    i = pl.program_id(1)
    n = HEAD_DIM
    nt = seq // tt
    chunks_per_tile = PACK_TILE // tt
    chunk = i + d * (nt - 1 - 2 * i)
    high = chunk % chunks_per_tile >= chunks_per_tile // 2
    t_after = jnp.where(d == 0, (i + 1) * tt, (nt - 1 - i) * tt - 1)
    high_after = t_after % PACK_TILE >= PACK_TILE // 2

    @pl.when(i == 0)
    def _():
        st_scr[...] = jnp.zeros_like(st_scr)
        sa_scr[...] = jnp.zeros_like(sa_scr)
        prevr_scr[...] = jnp.zeros_like(prevr_scr)
        prevv_scr[...] = jnp.zeros_like(prevv_scr)

    mu_r, mu_v = mur_ref[0], muv_ref[0]

    def row(slot, q, k):
        return jnp.broadcast_to(rows_scr[slot, q, k // SUBLANES, pl.ds(k % SUBLANES, 1), :],
                                sa_scr.shape[1:])

    def step(j, t, a_next, sa, gam, pr_prev, pv_prev):
        slot = j % 2
        gam = gam * w_ref[0, t]
        inv = 1.0 / gam
        pr, pv = _unpack_rows(pr_ref[t], high), pv_ref[t]
        r = pr + (pr_prev - pr) * mu_r
        v = pv + (pv_prev - pv) * mu_v
        rows_scr[slot, 0] = _unpack_rows(b_ref[0, t], high) * inv
        rows_scr[slot, 1] = _unpack_rows(k_ref[0, t], high) * inv
        rows_scr[slot, 2] = r * gam
        rows_scr[slot, 3] = a_next * gam
        y = jnp.zeros_like(v)
        sa_next = [jnp.zeros_like(v), jnp.zeros_like(v)]
        for k in range(n):
            new = st_scr[k] + (sa * row(slot, 0, k) + v * row(slot, 1, k))
            st_scr[k] = new
            y = y + new * row(slot, 2, k)
            sa_next[k % 2] = sa_next[k % 2] + new * row(slot, 3, k)
        y_ref[0, t] = y
        return sa_next[0] + sa_next[1], gam, pr, pv

    def body(j, carry):
        t = j + d * (tt - 1 - 2 * j)
        return step(j, t, _unpack_rows(a_ref[0, t + 1 - 2 * d], high), *carry)

    carry = (sa_scr[...], jnp.ones(sa_scr.shape, F32), prevr_scr[...], prevv_scr[...])
    carry = lax.fori_loop(0, tt - 1, body, carry)
    sa, gam, pr, pv = step(tt - 1, (1 - d) * (tt - 1), _unpack_rows(an_ref[0, 0], high_after), *carry)
    sa_scr[...] = sa
    prevr_scr[...] = pr
    prevv_scr[...] = pv
    rows_scr[0, 0] = gam
    for k in range(n):
        st_scr[k] = st_scr[k] * row(0, 0, k)


def _rwkv_scan(pr, w, k, pv, a, b, mu_r, mu_v, *, tt=64):
    s, n, c = pv.shape
    nt = s // tt
    cpt = PACK_TILE // tt
    tile = (n // SUBLANES, SUBLANES, c)
    order = lambda d, i: i + d * (nt - 1 - 2 * i)
    packed = lambda ci: (ci // cpt) * (cpt // 2) + ci % (cpt // 2)
    dir_spec = pl.BlockSpec((1, tt) + tile, lambda d, i: (d, order(d, i), 0, 0, 0))
    shared_spec = pl.BlockSpec((tt,) + tile, lambda d, i: (order(d, i), 0, 0, 0))
    dir_packed = pl.BlockSpec((1, tt) + tile, lambda d, i: (d, packed(order(d, i)), 0, 0, 0))
    shared_packed = pl.BlockSpec((tt,) + tile, lambda d, i: (packed(order(d, i)), 0, 0, 0))
    per_dir = pl.BlockSpec((1,) + tile, lambda d, i: (d, 0, 0, 0))

    def token_after(d, i):
        t = jnp.where(d == 0, jnp.minimum((i + 1) * tt, s - 1), jnp.maximum((nt - 1 - i) * tt - 1, 0))
        return (d, (t // PACK_TILE) * (PACK_TILE // 2) + t % (PACK_TILE // 2), 0, 0, 0)

    k, a, b = (t.reshape((2, s // 2) + tile) for t in (k, a, b))
    y = pl.pallas_call(
        functools.partial(_rwkv_scan_kernel, tt=tt, seq=s),
        grid=(2, nt),
        in_specs=[shared_packed, dir_spec, dir_packed, shared_spec, dir_packed, dir_packed,
                  pl.BlockSpec((1, 1) + tile, token_after), per_dir, per_dir],
        out_specs=dir_spec,
        out_shape=jax.ShapeDtypeStruct((2, s) + tile, F32),
        scratch_shapes=[pltpu.VMEM((n,) + tile, F32), pltpu.VMEM(tile, F32),
                        pltpu.VMEM((2, 4) + tile, F32), pltpu.VMEM(tile, F32), pltpu.VMEM(tile, F32)],
        compiler_params=_cparams(("parallel", "arbitrary")),
        name="rwkv_scan",
    )(pr.reshape((s // 2,) + tile), w.reshape((2, s) + tile), k, pv.reshape((s,) + tile), a, b, a,
      mu_r.reshape((2,) + tile), mu_v.reshape((2,) + tile))
    return y.reshape(2, s, n, c)


def _to_chain_lanes(t):
    *lead, b, s, _ = t.shape
    k = len(lead)
    t = t.reshape(*lead, b, s, N_HEADS, HEAD_DIM)
    t = t.transpose(*range(k), k + 1, k + 3, k, k + 2)
    return t.reshape(*lead, s, HEAD_DIM, b * N_HEADS)


def _param_to_chain_lanes(p, b):
    lead = p.shape[:-1]
    p = jnp.swapaxes(p.reshape(*lead, N_HEADS, HEAD_DIM), -1, -2)
    return jnp.broadcast_to(p[..., None, :], lead + (HEAD_DIM, b, N_HEADS)).reshape(
        *lead, HEAD_DIM, b * N_HEADS)


def _from_chain_lanes(t, b):
    _, s, _, _ = t.shape
    t = t.reshape(2, s, HEAD_DIM, b, N_HEADS).transpose(0, 3, 1, 4, 2)
    return t.reshape(2, b, s, W_MIX)


def _merge_kernel(y_ref, bonus_ref, l_ref, gz0_ref, gz1_ref, gz2_ref, oa_ref, ob_ref, x_ref, gt_ref,
                  gnw_ref, gnb_ref, ones_ref, gc_ref, wb_ref, wo_ref, o_ref):
    ones_bd = ones_ref[...]
    inv_n = 1.0 / HEAD_DIM
    acc = bonus_ref[0]
    for d in range(2):
        y = y_ref[d, 0]
        mean = _seg_sum(y, ones_bd) * inv_n
        yc = y - mean
        var = _seg_sum(yc * yc, ones_bd) * inv_n
        acc = acc + (yc * lax.rsqrt(var + GN_EPS)) * gnw_ref[...] + gnb_ref[...]
    g = jnp.dot(_sigmoid(l_ref[0]).astype(BF16), gc_ref[...], preferred_element_type=F32)
    o_c = acc * g
    merged = (_sigmoid(gz0_ref[0])
              * jnp.dot(oa_ref[0], wb_ref[0], preferred_element_type=F32)
              + _sigmoid(gz1_ref[0])
              * jnp.dot(ob_ref[0], wb_ref[1], preferred_element_type=F32)
              + _sigmoid(gz2_ref[0])
              * jnp.dot(o_c.astype(BF16), wb_ref[2], preferred_element_type=F32))
    out = jnp.dot(merged.astype(BF16), wo_ref[...], preferred_element_type=F32)
    o_ref[0] = x_ref[0] + gt_ref[0] * out


def _merge(y, bonus, proj, o_a, o_b, x, gate, gn_w, gn_b, ones_bd, g_comb, w_branch, w_out, *, tm=512):
    b, s, d = x.shape
    gz_per = d // COLBLK
    tok = lambda width, cb: pl.BlockSpec((1, tm, width), lambda bi, i, cb=cb: (bi, i, cb))
    const = lambda shape: pl.BlockSpec(shape, lambda bi, i: (0,) * len(shape))
    return pl.pallas_call(
        _merge_kernel,
        grid=(b, s // tm),
        in_specs=[pl.BlockSpec((2, 1, tm, COLBLK), lambda bi, i: (0, bi, i, 0)),
                  tok(COLBLK, 0),
                  tok(COLBLK, CB_LORA),
                  tok(d, CB_GZ // gz_per), tok(d, CB_GZ // gz_per + 1), tok(d, CB_GZ // gz_per + 2),
                  tok(COLBLK, 0), tok(COLBLK, 0), tok(d, 0),
                  pl.BlockSpec((1, 1, d), lambda bi, i: (bi, 0, 0)),
                  const((1, COLBLK)), const((1, COLBLK)), const((COLBLK, COLBLK)),
                  const((COLBLK, COLBLK)), const((3, COLBLK, d)), const((d, d))],
        out_specs=pl.BlockSpec((1, tm, d), lambda bi, i: (bi, i, 0)),
        out_shape=jax.ShapeDtypeStruct((b, s, d), F32),
        compiler_params=_cparams(("parallel", "parallel")),
        name="merge_out",
    )(y, bonus, proj, proj, proj, proj, o_a, o_b, x, gate, gn_w, gn_b, ones_bd, g_comb, w_branch, w_out)


def _split_w_in(w_in):
    n_lora = 4 * DECAY_LORA + GATE_LORA
    n_rest = N_ATTN + CB_LORA * COLBLK + n_lora
    rest, gates = w_in[..., N_ATTN:n_rest], w_in[..., n_rest:]
    zpad = jnp.zeros(w_in.shape[:-1] + (COLBLK - n_lora,), w_in.dtype)
    return (w_in[..., :N_ATTN].astype(BF16),
            jnp.concatenate([rest, zpad, gates], axis=-1).astype(BF16))


def _lora_params(mu_w, mu_a, w_up, a_up, g_up):
    r = DECAY_LORA
    depth = mu_w.shape[0]
    mu_l = jnp.zeros((depth, 2, COLBLK), F32)
    w_comb = jnp.zeros((depth, 2, 2 * LANES, 2 * COLBLK), F32)
    for d in range(2):
        wcol, acol = d * r, 2 * r + d * r
        mu_l = mu_l.at[:, d, wcol:wcol + r].set(mu_w[:, d]).at[:, d, acol:acol + r].set(mu_a[:, d])
        w_comb = w_comb.at[:, d, wcol:wcol + r, :COLBLK].set(w_up[:, d])
        w_comb = w_comb.at[:, d, acol:acol + r, COLBLK:].set(a_up[:, d])
    g_comb = jnp.zeros((depth, COLBLK, COLBLK), F32).at[:, 4 * r:4 * r + GATE_LORA].set(g_up)
    return mu_l, w_comb.astype(BF16), g_comb.astype(BF16)


def _head_block_ones():
    seg = np.arange(COLBLK) // HEAD_DIM
    return jnp.asarray(seg[:, None] == seg[None, :], dtype=BF16)


def kernel(x, c, positions, ada_w, ada_b, norm_gains, ffn_wi, ffn_wo, w_in, rpb, mu_rkv, mu_w, mu_a,
           w0, w_up, a0, a_up, g_up, k_k, k_a, r_k, gn_w, gn_b, w_branch, w_out, final_norm):
    depth = ada_w.shape[0]
    b, s, d = x.shape
    assert d == 2 * COLBLK and s % 1024 == 0 and s % (GRID_W * NA_WIN_ROWS) == 0

    mod = _modulation(c, ada_w, ada_b).reshape(depth, b, N_MOD, 1, d)
    cos, s1, s2 = _rope_tables(positions)
    ones_bd = _head_block_ones()
    bias_tabs = _natten_bias_tables(rpb, s // GRID_W)
    dil_masks = _dilated_masks(s)
    ffn_wi_bf, ffn_wo_bf = ffn_wi.astype(BF16), ffn_wo.astype(BF16)
    w_branch_bf, w_out_bf = w_branch.astype(BF16), w_out.astype(BF16)
    w_attn, w_rest = _split_w_in(w_in)
    mu_l, w_comb, g_comb = _lora_params(mu_w, mu_a, w_up, a_up, g_up)
    mu_r, mu_v = _param_to_chain_lanes(mu_rkv[:, :, 0], b), _param_to_chain_lanes(mu_rkv[:, :, 2], b)
    row = lambda t: t.reshape(1, -1)

    for l in range(depth):
        sh1, sc1, gt1, sh2, sc2, gt2, sh3, sc3, gt3 = (mod[l, :, i] for i in range(N_MOD))

        x = _ffn(x, norm_gains[l, 0], sh1, sc1, gt1, ffn_wi_bf[l, 0], ffn_wo_bf[l, 0])

        attn, proj = _in_proj(x, norm_gains[l, 1], sh2, sc2, w_attn[l], w_rest[l])
        o_a = _dilated_attention(attn, cos, s1, s2, dil_masks)
        o_b = _neighborhood_attention(attn, bias_tabs[l])
        decay, k2, a_vec, b_vec, bonus, pr, pv = _rwkv_pre(
            proj, mu_rkv[l], mu_l[l], w_comb[l], w0[l], a0[l], row(k_k[l]), row(k_a[l]), row(r_k[l]),
            ones_bd)
        y = _rwkv_scan(_to_chain_lanes(pr), _to_chain_lanes(decay), _to_chain_lanes(k2),
                       _to_chain_lanes(pv), _to_chain_lanes(a_vec), _to_chain_lanes(b_vec),
                       mu_r[l], mu_v[l])
        y = _from_chain_lanes(y, b)
        x = _merge(y, bonus, proj, o_a, o_b, x, gt2, row(gn_w[l]), row(gn_b[l]), ones_bd, g_comb[l],
                   w_branch_bf[l], w_out_bf[l])

        x = _ffn(x, norm_gains[l, 2], sh3, sc3, gt3, ffn_wi_bf[l, 1], ffn_wo_bf[l, 1],
                 final_norm if l == depth - 1 else None)
    return x
```

```python
import functools

import numpy as np
import jax
import jax.numpy as jnp
from jax import lax
from jax.experimental import pallas as pl
from jax.experimental.pallas import tpu as pltpu

F32 = jnp.float32
BF16 = jnp.bfloat16

HEAD_DIM = 64
N_HEADS = 8
W_MIX = N_HEADS * HEAD_DIM
DILATED_PAIRS = ((128, 1), (512, 4), (2048, 16))
QBLK = 128
ROT_DIM = HEAD_DIM // 4
ROPE_THETA = 500000.0
GRID_W = 64
NA_WIN_ROWS = 8
NA_WIN_COLS = 16
DECAY_LORA = 64
ICLR_LORA = 64
GATE_LORA = 128
DECAY_SCALE = 0.6065306597126334
N_MOD = 9
RMS_EPS = 1e-6
GN_EPS = 64e-5
NEG = -1e30

LANES = 128
SUBLANES = 8
VMEM_LIMIT = 48 * 1024 * 1024
DILATED_GROUP = 4
NATTEN_GROUP = 16
PACK_TILE = 256

COLBLK = 512
CB_AQ, CB_AK, CB_AV = 0, 1, 2
CB_NQ, CB_NK, CB_NV = 3, 4, 5
N_ATTN = 6 * COLBLK
CB_PR, CB_PK, CB_PV = 0, 1, 2
CB_LORA = 3
CB_GZ = 4


def _cparams(sem):
    return pltpu.CompilerParams(dimension_semantics=sem, vmem_limit_bytes=VMEM_LIMIT)


def _sigmoid(x):
    return 0.5 * jnp.tanh(0.5 * x) + 0.5


def _mod_kernel(c_ref, w_ref, b_ref, o_ref):
    c = c_ref[...]
    cs = (c * _sigmoid(c)).astype(BF16)
    o_ref[0] = jnp.dot(cs, w_ref[0].astype(BF16), preferred_element_type=F32) + b_ref[0]


def _modulation(c, ada_w, ada_b):
    depth, d, nd = ada_w.shape
    b = c.shape[0]
    return pl.pallas_call(
        _mod_kernel,
        grid=(depth, nd // d),
        in_specs=[pl.BlockSpec((b, d), lambda l, j: (0, 0)),
                  pl.BlockSpec((1, d, d), lambda l, j: (l, 0, j)),
                  pl.BlockSpec((1, 1, d), lambda l, j: (l, 0, j))],
        out_specs=pl.BlockSpec((1, b, d), lambda l, j: (l, 0, j)),
        out_shape=jax.ShapeDtypeStruct((depth, b, nd), F32),
        compiler_params=_cparams(("parallel", "parallel")),
        name="adaln_mod",
    )(c, ada_w, ada_b.reshape(depth, 1, nd))


def _norm_mod(x, gain, shift, scale):
    ms = jnp.mean(x * x, axis=-1, keepdims=True)
    y = x * lax.rsqrt(ms + RMS_EPS) * gain
    return y * (1.0 + scale) + shift


def _ffn_kernel(*refs, final, tf):
    if final:
        x_ref, g_ref, sh_ref, sc_ref, gt_ref, wi_ref, wo_ref, fin_ref, o_ref = refs
    else:
        x_ref, g_ref, sh_ref, sc_ref, gt_ref, wi_ref, wo_ref, o_ref = refs
    x = x_ref[0]
    h = _norm_mod(x, g_ref[...], sh_ref[0], sc_ref[0]).astype(BF16)
    dff = wo_ref.shape[0]
    acc = None
    for lo in range(0, dff, tf):
        gate = jnp.dot(h, wi_ref[:, lo:lo + tf], preferred_element_type=F32)
        up = jnp.dot(h, wi_ref[:, dff + lo:dff + lo + tf], preferred_element_type=F32)
        act = ((gate * _sigmoid(gate)) * up).astype(BF16)
        part = jnp.dot(act, wo_ref[lo:lo + tf, :], preferred_element_type=F32)
        acc = part if acc is None else acc + part
    y = x + 0.5 * gt_ref[0] * acc
    if final:
        ms = jnp.mean(y * y, axis=-1, keepdims=True)
        y = y * lax.rsqrt(ms + RMS_EPS) * fin_ref[...]
    o_ref[0] = y


def _ffn(x, gain, shift, scale, gate, wi, wo, final_gain=None, *, tm=512, tf=256):
    b, s, d = x.shape
    dff = wo.shape[0]
    final = final_gain is not None
    vec = pl.BlockSpec((1, 1, d), lambda bi, i: (bi, 0, 0))
    resident = lambda shape: pl.BlockSpec(shape, lambda bi, i: (0, 0), pipeline_mode=pl.Buffered(1))
    in_specs = [pl.BlockSpec((1, tm, d), lambda bi, i: (bi, i, 0)),
                pl.BlockSpec((1, d), lambda bi, i: (0, 0)),
                vec, vec, vec,
                resident((d, 2 * dff)), resident((dff, d))]
    args = [x, gain.reshape(1, d), shift, scale, gate, wi, wo]
    if final:
        in_specs.append(pl.BlockSpec((1, d), lambda bi, i: (0, 0)))
        args.append(final_gain.reshape(1, d))
    return pl.pallas_call(
        functools.partial(_ffn_kernel, final=final, tf=tf),
        grid=(b, s // tm),
        in_specs=in_specs,
        out_specs=pl.BlockSpec((1, tm, d), lambda bi, i: (bi, i, 0)),
        out_shape=jax.ShapeDtypeStruct((b, s, d), F32),
        compiler_params=_cparams(("parallel", "parallel")),
        name="ffn_final" if final else "ffn",
    )(*args)


def _win_kernel(x_ref, g_ref, sh_ref, sc_ref, wa_ref, wr_ref, oa_ref, or_ref, h_scr):
    @pl.when(pl.program_id(2) == 0)
    def _():
        h = _norm_mod(x_ref[0], g_ref[...], sh_ref[0], sc_ref[0])
        h_scr[...] = h.astype(BF16)

    h = h_scr[...]
    oa_ref[0] = jnp.dot(h, wa_ref[...], preferred_element_type=F32).astype(BF16)
    or_ref[0] = jnp.dot(h, wr_ref[...], preferred_element_type=F32)


def _in_proj(x, gain, shift, scale, w_attn, w_rest, *, tm=1024, nj=4):
    b, s, d = x.shape
    ta, tr = w_attn.shape[1] // nj, w_rest.shape[1] // nj
    vec = pl.BlockSpec((1, 1, d), lambda bi, i, j: (bi, 0, 0))
    return pl.pallas_call(
        _win_kernel,
        grid=(b, s // tm, nj),
        in_specs=[pl.BlockSpec((1, tm, d), lambda bi, i, j: (bi, i, 0)),
                  pl.BlockSpec((1, d), lambda bi, i, j: (0, 0)),
                  vec, vec,
                  pl.BlockSpec((d, ta), lambda bi, i, j: (0, j)),
                  pl.BlockSpec((d, tr), lambda bi, i, j: (0, j))],
        out_specs=[pl.BlockSpec((1, tm, ta), lambda bi, i, j: (bi, i, j)),
                   pl.BlockSpec((1, tm, tr), lambda bi, i, j: (bi, i, j))],
        out_shape=[jax.ShapeDtypeStruct((b, s, w_attn.shape[1]), BF16),
                   jax.ShapeDtypeStruct((b, s, w_rest.shape[1]), F32)],
        scratch_shapes=[pltpu.VMEM((tm, d), BF16)],
        compiler_params=_cparams(("parallel", "parallel", "arbitrary")),
        name="in_proj",
    )(x, gain.reshape(1, d), shift, scale, w_attn, w_rest)


def _dilated_kernel(q_ref, k_ref, v_ref, cos_ref, s1_ref, s2_ref, *rest, seq, pad):
    n_pat = len(DILATED_PAIRS)
    mask_refs = rest[:n_pat]
    o_ref, q_scr, k_scr, v_scr, og_scr, lse_scr = rest[n_pat:]
    cos, s1, s2 = cos_ref[0], s1_ref[0], s2_ref[0]

    def rotary(t):
        return (t * cos + pltpu.roll(t, LANES - ROT_DIM // 2, axis=1) * s1
                + pltpu.roll(t, ROT_DIM // 2, axis=1) * s2)

    q_scr[...] = rotary(q_ref[0].astype(F32)) * (HEAD_DIM ** -0.5)
    zeros = jnp.zeros((pad, LANES), F32)
    k_scr[pl.ds(0, pad), :] = zeros
    k_scr[pl.ds(pad + seq, pad), :] = zeros
    v_scr[pl.ds(0, pad), :] = zeros
    v_scr[pl.ds(pad + seq, pad), :] = zeros
    k_scr[pl.ds(pad, seq), :] = rotary(k_ref[0].astype(F32))
    v_scr[pl.ds(pad, seq), :] = v_ref[0].astype(F32)

    lane = lax.broadcasted_iota(jnp.int32, (QBLK, LANES), 1)
    head0 = lane < HEAD_DIM

    for g, (window, dil) in enumerate(DILATED_PAIRS):
        nblk, lead, nkeys = _dilated_geometry(seq, window, dil)

        def group(gi, carry, g=g, dil=dil, lead=lead, nkeys=nkeys, nblk=nblk):
            units = []
            for u in range(DILATED_GROUP):
                idx = gi * DILATED_GROUP + u
                cls = idx // nblk
                n = idx % nblk
                q_start = cls + dil * QBLK * n
                k_start = pad + cls + dil * (QBLK * n - lead)
                if dil == 1:
                    q_rows = pl.ds(q_start, QBLK)
                    k_rows = pl.ds(k_start, nkeys)
                else:
                    q_rows = pl.ds(q_start, QBLK, stride=dil)
                    k_rows = pl.ds(k_start, nkeys, stride=dil)
                which = jnp.where(n == 0, 0, jnp.where(n == nblk - 1, 2, 1)) if nblk > 1 else 0
                units.append((q_rows, q_scr[q_rows, :], k_scr[k_rows, :].astype(BF16),
                              v_scr[k_rows, :].astype(BF16), which))
            scores = []
            for _, q, kw, _, which in units:
                for hmask in (head0, jnp.logical_not(head0)):
                    qh = jnp.where(hmask, q, 0.0).astype(BF16)
                    s = lax.dot_general(qh, kw, (((1,), (1,)), ((), ())),
                                        preferred_element_type=F32)
                    scores.append(s + mask_refs[g][which])
            probs, invs, lses = [], [], []
            for s in scores:
                m = jnp.max(s, axis=-1, keepdims=True)
                p = jnp.exp(s - m)
                den = jnp.sum(p, axis=-1, keepdims=True)
                probs.append(p.astype(BF16))
                invs.append(1.0 / den)
                lses.append(m + jnp.log(den))
            for u, (q_rows, _, _, vw, _) in enumerate(units):
                o0 = jnp.dot(probs[2 * u], vw, preferred_element_type=F32) * invs[2 * u]
                o1 = jnp.dot(probs[2 * u + 1], vw, preferred_element_type=F32) * invs[2 * u + 1]
                og_scr[g, q_rows, :] = jnp.where(head0, o0, o1)
                lse_scr[g, q_rows, :] = jnp.where(head0, lses[2 * u], lses[2 * u + 1])
            return carry

        lax.fori_loop(0, dil * nblk // DILATED_GROUP, group, 0)

    l0, l1, l2 = lse_scr[0], lse_scr[1], lse_scr[2]
    m = jnp.maximum(jnp.maximum(l0, l1), l2)
    e0, e1, e2 = jnp.exp(l0 - m), jnp.exp(l1 - m), jnp.exp(l2 - m)
    out = (e0 * og_scr[0] + e1 * og_scr[1] + e2 * og_scr[2]) * (1.0 / (e0 + e1 + e2))
    o_ref[0] = out.astype(o_ref.dtype)


def _dilated_geometry(seq, window, dil):
    radius = window // (2 * dil)
    nblk = seq // dil // QBLK
    lead = radius if nblk > 1 else 0
    return nblk, lead, QBLK + 2 * lead


def _dilated_masks(seq):
    out = []
    for window, dil in DILATED_PAIRS:
        radius = window // (2 * dil)
        nblk, lead, nkeys = _dilated_geometry(seq, window, dil)
        qi = np.arange(QBLK)[:, None]
        kj = np.arange(nkeys)[None, :]
        band = np.abs(kj - lead - qi) <= radius
        tabs = []
        for n in ((0, 1, nblk - 1) if nblk > 1 else (0,)):
            key_pos = QBLK * n + kj - lead
            tabs.append(band & (key_pos >= 0) & (key_pos < nblk * QBLK))
        out.append(jnp.asarray(np.where(np.stack(tabs), 0.0, NEG), F32))
    return out


def _dilated_attention(attn, cos, s1, s2, masks):
    b, s, _ = attn.shape
    pad = max((w // (2 * d)) * d for w, d in DILATED_PAIRS)
    hp = W_MIX // LANES
    per_cb = COLBLK // LANES
    col = lambda cb: pl.BlockSpec((1, s, LANES), lambda bi, h, cb=cb: (bi, 0, cb * per_cb + h))
    tab = pl.BlockSpec((1, s, LANES), lambda bi, h: (bi, 0, 0))
    const = lambda shape: pl.BlockSpec(shape, lambda bi, h: (0,) * len(shape))
    return pl.pallas_call(
        functools.partial(_dilated_kernel, seq=s, pad=pad),
        grid=(b, hp),
        in_specs=[col(CB_AQ), col(CB_AK), col(CB_AV), tab, tab, tab] + [const(m.shape) for m in masks],
        out_specs=pl.BlockSpec((1, s, LANES), lambda bi, h: (bi, 0, h)),
        out_shape=jax.ShapeDtypeStruct((b, s, W_MIX), BF16),
        scratch_shapes=[pltpu.VMEM((s, LANES), F32),
                        pltpu.VMEM((s + 2 * pad, LANES), F32),
                        pltpu.VMEM((s + 2 * pad, LANES), F32),
                        pltpu.VMEM((len(DILATED_PAIRS), s, LANES), F32),
                        pltpu.VMEM((len(DILATED_PAIRS), s, LANES), F32)],
        compiler_params=_cparams(("parallel", "parallel")),
        name="dilated_attn",
    )(attn, attn, attn, cos, s1, s2, *masks)


def _rope_tables(positions):
    half = ROT_DIM // 2
    lane = np.arange(LANES) % HEAD_DIM
    inv_freq = ROPE_THETA ** (-jnp.asarray(lane % half, F32) * 2.0 / ROT_DIM)
    ang = positions.astype(F32)[..., None] * inv_freq
    cos, sin = jnp.cos(ang), jnp.sin(ang)
    return (jnp.where(lane < ROT_DIM, cos, 1.0), jnp.where(lane < half, -sin, 0.0),
            jnp.where((lane >= half) & (lane < ROT_DIM), sin, 0.0))


def _natten_kernel(q_ref, k_ref, v_ref, bias_ref, o_ref, *, rows, win_rows):
    lane = lax.broadcasted_iota(jnp.int32, (GRID_W, LANES), 1)
    head0 = lane < HEAD_DIM
    scale = HEAD_DIM ** -0.5
    nkeys = win_rows * GRID_W

    def group(gi, carry):
        units = []
        for u in range(NATTEN_GROUP):
            r = gi * NATTEN_GROUP + u
            r_start = jnp.clip(r - win_rows // 2, 0, rows - win_rows)
            q_rows = pl.ds(pl.multiple_of(r * GRID_W, GRID_W), GRID_W)
            k_rows = pl.ds(pl.multiple_of(r_start * GRID_W, GRID_W), nkeys)
            units.append((q_rows, r - r_start, q_ref[0, q_rows, :],
                          k_ref[0, k_rows, :], v_ref[0, k_rows, :]))
        scores = []
        for _, delta, q, kw, _ in units:
            q = q * scale
            zero = jnp.zeros_like(q)
            q2 = jnp.concatenate([jnp.where(head0, q, zero), jnp.where(head0, zero, q)], axis=0)
            s2 = lax.dot_general(q2, kw, (((1,), (1,)), ((), ())), preferred_element_type=F32)
            scores += [s2[:GRID_W] + bias_ref[0, delta], s2[GRID_W:] + bias_ref[1, delta]]
        probs, invs = [], []
        for s in scores:
            m = jnp.max(s, axis=-1, keepdims=True)
            p = jnp.exp(s - m)
            probs.append(p.astype(BF16))
            invs.append(1.0 / jnp.sum(p, axis=-1, keepdims=True))
        for u, (q_rows, _, _, _, vw) in enumerate(units):
            o2 = jnp.dot(jnp.concatenate(probs[2 * u:2 * u + 2], axis=0), vw,
                         preferred_element_type=F32)
            o_ref[0, q_rows, :] = jnp.where(head0, o2[:GRID_W] * invs[2 * u],
                                            o2[GRID_W:] * invs[2 * u + 1]).astype(o_ref.dtype)
        return carry

    lax.fori_loop(0, rows // NATTEN_GROUP, group, 0)


def _natten_bias_tables(rpb, rows):
    wr = min(NA_WIN_ROWS, rows)
    wc = NA_WIN_COLS
    cols = np.arange(GRID_W)
    c_start = np.clip(cols - wc // 2, 0, GRID_W - wc)
    col_in = (cols[None, :] >= c_start[:, None]) & (cols[None, :] < c_start[:, None] + wc)
    coff = np.clip(cols[None, :] - cols[:, None], -(wc - 1), wc - 1) + wc - 1
    onehot = (coff[None] == np.arange(2 * wc - 1)[:, None, None]).astype(np.float32)
    by_col = jnp.einsum("lhrc,cqk->lhqrk", rpb.astype(F32), onehot, precision=lax.Precision.HIGHEST)
    by_col = jnp.where(col_in[:, None, :], by_col, NEG)
    top = NA_WIN_ROWS - 1
    tabs = jnp.stack([by_col[:, :, :, top - dl:top - dl + wr] for dl in range(wr)], axis=2)
    return tabs.reshape(rpb.shape[0], rpb.shape[1], wr, GRID_W, wr * GRID_W)


def _neighborhood_attention(attn, bias_tab):
    b, s, _ = attn.shape
    rows = s // GRID_W
    wr = min(NA_WIN_ROWS, rows)
    hp = W_MIX // LANES
    per_cb = COLBLK // LANES
    col = lambda cb: pl.BlockSpec((1, s, LANES), lambda bi, h, cb=cb: (bi, 0, cb * per_cb + h))
    return pl.pallas_call(
        functools.partial(_natten_kernel, rows=rows, win_rows=wr),
        grid=(b, hp),
        in_specs=[col(CB_NQ), col(CB_NK), col(CB_NV),
                  pl.BlockSpec((LANES // HEAD_DIM, wr, GRID_W, wr * GRID_W),
                               lambda bi, h: (h, 0, 0, 0))],
        out_specs=pl.BlockSpec((1, s, LANES), lambda bi, h: (bi, 0, h)),
        out_shape=jax.ShapeDtypeStruct((b, s, W_MIX), BF16),
        compiler_params=_cparams(("parallel", "parallel")),
        name="natten",
    )(attn, attn, attn, bias_tab)


def _seg_sum(x, ones_bd):
    return jnp.dot(x.astype(BF16), ones_bd, preferred_element_type=F32)


def _pack_rows(x):
    half = x.shape[0] // 2
    bits = lax.bitcast_convert_type(x.astype(BF16).astype(F32), jnp.uint32)
    return (bits[:half] >> 16) | (bits[half:] & jnp.uint32(0xFFFF0000))


def _unpack_rows(u, high):
    shift = jnp.where(high, 0, 16).astype(jnp.uint32)
    return lax.bitcast_convert_type((u << shift) & jnp.uint32(0xFFFF0000), F32)


def _rwkv_pre_kernel(r_ref, k_ref, v_ref, l_ref,
                     rp_ref, kp_ref, vp_ref, lp_ref,
                     rn_ref, kn_ref, vn_ref, ln_ref,
                     mu_ref, mul_ref, wc_ref, w0_ref, a0_ref, kk_ref, ka_ref, rk_ref, ones_ref,
                     w_o, k_o, a_o, b_o, bonus_o, pr_o, pv_o, *, tm):
    i = pl.program_id(1)
    first = i == 0
    last = i == pl.num_programs(1) - 1
    row = lax.broadcasted_iota(jnp.int32, (tm, COLBLK), 0)
    ones_bd = ones_ref[...]

    def neighbours(x_ref, p_ref, n_ref):
        x = x_ref[0]
        prev_row = jnp.where(first, 0.0, p_ref[0, SUBLANES - 1:SUBLANES, :])
        next_row = jnp.where(last, 0.0, n_ref[0, 0:1, :])
        prev = jnp.where(row == 0, prev_row, pltpu.roll(x, 1, axis=0))
        nxt = jnp.where(row == tm - 1, next_row, pltpu.roll(x, tm - 1, axis=0))
        return x, (prev, nxt)

    pr, nb_r = neighbours(r_ref, rp_ref, rn_ref)
    pk, nb_k = neighbours(k_ref, kp_ref, kn_ref)
    pv, nb_v = neighbours(v_ref, vp_ref, vn_ref)
    pl_, nb_l = neighbours(l_ref, lp_ref, ln_ref)

    bonus = jnp.zeros((tm, COLBLK), F32)
    for d in range(2):
        r = pr + (nb_r[d] - pr) * mu_ref[d, 0:1, :]
        k = pk + (nb_k[d] - pk) * mu_ref[d, 1:2, :]
        v = pv + (nb_v[d] - pv) * mu_ref[d, 2:3, :]
        xl = pl_ + (nb_l[d] - pl_) * mul_ref[d:d + 1, :]
        zw = jnp.dot(jnp.tanh(xl[:, :LANES]).astype(BF16), wc_ref[d, :LANES, :],
                     preferred_element_type=F32)
        za = jnp.dot(xl[:, LANES:2 * LANES].astype(BF16), wc_ref[d, LANES:2 * LANES, :],
                     preferred_element_type=F32)
        z = zw + za
        wz = w0_ref[d:d + 1, :] + z[:, :COLBLK]
        az = a0_ref[d:d + 1, :] + z[:, COLBLK:]
        decay = jnp.exp(-DECAY_SCALE * _sigmoid(wz))
        a = _sigmoid(az)
        kk = k * kk_ref[...]
        kk = kk * lax.rsqrt(jnp.maximum(_seg_sum(kk * kk, ones_bd), 1e-24))
        k2 = k * (1.0 + (a - 1.0) * ka_ref[...])
        bonus = bonus + _seg_sum(r * k2 * rk_ref[...], ones_bd) * v
        w_o[d, 0] = decay
        k_o[d, 0] = _pack_rows(k2)
        a_o[d, 0] = _pack_rows(-kk)
        b_o[d, 0] = _pack_rows(kk * a)
    bonus_o[0] = bonus
    pr_o[0] = _pack_rows(pr)
    pv_o[0] = pv


def _rwkv_pre(proj, mu_rkv, mu_lora, w_comb, w0, a0, k_k, k_a, r_k, ones_bd, *, tm=PACK_TILE):
    b, s, _ = proj.shape
    nsub = tm // SUBLANES
    last_sub = s // SUBLANES - 1
    main = lambda cb: pl.BlockSpec((1, tm, COLBLK), lambda bi, i, cb=cb: (bi, i, cb))
    prev = lambda cb: pl.BlockSpec(
        (1, SUBLANES, COLBLK), lambda bi, i, cb=cb: (bi, jnp.maximum(i * nsub - 1, 0), cb))
    nxt = lambda cb: pl.BlockSpec(
        (1, SUBLANES, COLBLK), lambda bi, i, cb=cb: (bi, jnp.minimum((i + 1) * nsub, last_sub), cb))
    cbs = (CB_PR, CB_PK, CB_PV, CB_LORA)
    const = lambda shape: pl.BlockSpec(shape, lambda bi, i: (0,) * len(shape))
    tok = lambda rows: pl.BlockSpec((1, rows, COLBLK), lambda bi, i: (bi, i, 0))
    dir_out = lambda rows: pl.BlockSpec((2, 1, rows, COLBLK), lambda bi, i: (0, bi, i, 0))
    u32 = jnp.uint32
    return pl.pallas_call(
        functools.partial(_rwkv_pre_kernel, tm=tm),
        grid=(b, s // tm),
        in_specs=([main(cb) for cb in cbs] + [prev(cb) for cb in cbs] + [nxt(cb) for cb in cbs]
                  + [const((2, 3, COLBLK)), const((2, COLBLK)), const((2, 2 * LANES, 2 * COLBLK)),
                     const((2, COLBLK)), const((2, COLBLK)), const((1, COLBLK)), const((1, COLBLK)),
                     const((1, COLBLK)), const((COLBLK, COLBLK))]),
        out_specs=[dir_out(tm), dir_out(tm // 2), dir_out(tm // 2), dir_out(tm // 2),
                   tok(tm), tok(tm // 2), tok(tm)],
        out_shape=[jax.ShapeDtypeStruct((2, b, s, COLBLK), F32)]
                  + [jax.ShapeDtypeStruct((2, b, s // 2, COLBLK), u32)] * 3
                  + [jax.ShapeDtypeStruct((b, s, COLBLK), F32),
                     jax.ShapeDtypeStruct((b, s // 2, COLBLK), u32),
                     jax.ShapeDtypeStruct((b, s, COLBLK), F32)],
        compiler_params=_cparams(("parallel", "parallel")),
        name="rwkv_pre",
    )(*([proj] * 12), mu_rkv, mu_lora, w_comb, w0, a0, k_k, k_a, r_k, ones_bd)


def _rwkv_scan_kernel(pr_ref, w_ref, k_ref, pv_ref, a_ref, b_ref, an_ref, mur_ref, muv_ref, y_ref,
                      st_scr, sa_scr, rows_scr, prevr_scr, prevv_scr, *, tt, seq):
    d = pl.program_id(0)
    i = pl.program_id(1)
    n = HEAD_DIM
    nt = seq // tt
    chunks_per_tile = PACK_TILE // tt
    chunk = i + d * (nt - 1 - 2 * i)
    high = chunk % chunks_per_tile >= chunks_per_tile // 2
    t_after = jnp.where(d == 0, (i + 1) * tt, (nt - 1 - i) * tt - 1)
    high_after = t_after % PACK_TILE >= PACK_TILE // 2

    @pl.when(i == 0)
    def _():
        st_scr[...] = jnp.zeros_like(st_scr)
        sa_scr[...] = jnp.zeros_like(sa_scr)
        prevr_scr[...] = jnp.zeros_like(prevr_scr)
        prevv_scr[...] = jnp.zeros_like(prevv_scr)

    mu_r, mu_v = mur_ref[0], muv_ref[0]

    def row(slot, q, k):
        return jnp.broadcast_to(rows_scr[slot, q, k // SUBLANES, pl.ds(k % SUBLANES, 1), :],
                                sa_scr.shape[1:])

    def step(j, t, a_next, sa, gam, pr_prev, pv_prev):
        slot = j % 2
        gam = gam * w_ref[0, t]
        inv = 1.0 / gam
        pr, pv = _unpack_rows(pr_ref[t], high), pv_ref[t]
        r = pr + (pr_prev - pr) * mu_r
        v = pv + (pv_prev - pv) * mu_v
        rows_scr[slot, 0] = _unpack_rows(b_ref[0, t], high) * inv
        rows_scr[slot, 1] = _unpack_rows(k_ref[0, t], high) * inv
        rows_scr[slot, 2] = r * gam
        rows_scr[slot, 3] = a_next * gam
        y = jnp.zeros_like(v)
        sa_next = [jnp.zeros_like(v), jnp.zeros_like(v)]
        for k in range(n):
            new = st_scr[k] + (sa * row(slot, 0, k) + v * row(slot, 1, k))
            st_scr[k] = new
            y = y + new * row(slot, 2, k)
            sa_next[k % 2] = sa_next[k % 2] + new * row(slot, 3, k)
        y_ref[0, t] = y
        return sa_next[0] + sa_next[1], gam, pr, pv

    def body(j, carry):
        t = j + d * (tt - 1 - 2 * j)
        return step(j, t, _unpack_rows(a_ref[0, t + 1 - 2 * d], high), *carry)

    carry = (sa_scr[...], jnp.ones(sa_scr.shape, F32), prevr_scr[...], prevv_scr[...])
    carry = lax.fori_loop(0, tt - 1, body, carry)
    sa, gam, pr, pv = step(tt - 1, (1 - d) * (tt - 1), _unpack_rows(an_ref[0, 0], high_after), *carry)
    sa_scr[...] = sa
    prevr_scr[...] = pr
    prevv_scr[...] = pv
    rows_scr[0, 0] = gam
    for k in range(n):
        st_scr[k] = st_scr[k] * row(0, 0, k)


def _rwkv_scan(pr, w, k, pv, a, b, mu_r, mu_v, *, tt=64):
    s, n, c = pv.shape
    nt = s // tt
    cpt = PACK_TILE // tt
    tile = (n // SUBLANES, SUBLANES, c)
    order = lambda d, i: i + d * (nt - 1 - 2 * i)
    packed = lambda ci: (ci // cpt) * (cpt // 2) + ci % (cpt // 2)
    dir_spec = pl.BlockSpec((1, tt) + tile, lambda d, i: (d, order(d, i), 0, 0, 0))
    shared_spec = pl.BlockSpec((tt,) + tile, lambda d, i: (order(d, i), 0, 0, 0))
    dir_packed = pl.BlockSpec((1, tt) + tile, lambda d, i: (d, packed(order(d, i)), 0, 0, 0))
    shared_packed = pl.BlockSpec((tt,) + tile, lambda d, i: (packed(order(d, i)), 0, 0, 0))
    per_dir = pl.BlockSpec((1,) + tile, lambda d, i: (d, 0, 0, 0))

    def token_after(d, i):
        t = jnp.where(d == 0, jnp.minimum((i + 1) * tt, s - 1), jnp.maximum((nt - 1 - i) * tt - 1, 0))
        return (d, (t // PACK_TILE) * (PACK_TILE // 2) + t % (PACK_TILE // 2), 0, 0, 0)

    k, a, b = (t.reshape((2, s // 2) + tile) for t in (k, a, b))
    y = pl.pallas_call(
        functools.partial(_rwkv_scan_kernel, tt=tt, seq=s),
        grid=(2, nt),
        in_specs=[shared_packed, dir_spec, dir_packed, shared_spec, dir_packed, dir_packed,
                  pl.BlockSpec((1, 1) + tile, token_after), per_dir, per_dir],
        out_specs=dir_spec,
        out_shape=jax.ShapeDtypeStruct((2, s) + tile, F32),
        scratch_shapes=[pltpu.VMEM((n,) + tile, F32), pltpu.VMEM(tile, F32),
                        pltpu.VMEM((2, 4) + tile, F32), pltpu.VMEM(tile, F32), pltpu.VMEM(tile, F32)],
        compiler_params=_cparams(("parallel", "arbitrary")),
        name="rwkv_scan",
    )(pr.reshape((s // 2,) + tile), w.reshape((2, s) + tile), k, pv.reshape((s,) + tile), a, b, a,
      mu_r.reshape((2,) + tile), mu_v.reshape((2,) + tile))
    return y.reshape(2, s, n, c)


def _to_chain_lanes(t):
    *lead, b, s, _ = t.shape
    k = len(lead)
    t = t.reshape(*lead, b, s, N_HEADS, HEAD_DIM)
    t = t.transpose(*range(k), k + 1, k + 3, k, k + 2)
    return t.reshape(*lead, s, HEAD_DIM, b * N_HEADS)


def _param_to_chain_lanes(p, b):
    lead = p.shape[:-1]
    p = jnp.swapaxes(p.reshape(*lead, N_HEADS, HEAD_DIM), -1, -2)
    return jnp.broadcast_to(p[..., None, :], lead + (HEAD_DIM, b, N_HEADS)).reshape(
        *lead, HEAD_DIM, b * N_HEADS)


def _from_chain_lanes(t, b):
    _, s, _, _ = t.shape
    t = t.reshape(2, s, HEAD_DIM, b, N_HEADS).transpose(0, 3, 1, 4, 2)
    return t.reshape(2, b, s, W_MIX)


def _merge_kernel(y_ref, bonus_ref, l_ref, gz0_ref, gz1_ref, gz2_ref, oa_ref, ob_ref, x_ref, gt_ref,
                  gnw_ref, gnb_ref, ones_ref, gc_ref, wb_ref, wo_ref, o_ref):
    ones_bd = ones_ref[...]
    inv_n = 1.0 / HEAD_DIM
    acc = bonus_ref[0]
    for d in range(2):
        y = y_ref[d, 0]
        mean = _seg_sum(y, ones_bd) * inv_n
        yc = y - mean
        var = _seg_sum(yc * yc, ones_bd) * inv_n
        acc = acc + (yc * lax.rsqrt(var + GN_EPS)) * gnw_ref[...] + gnb_ref[...]
    g = jnp.dot(_sigmoid(l_ref[0]).astype(BF16), gc_ref[...], preferred_element_type=F32)
    o_c = acc * g
    merged = (_sigmoid(gz0_ref[0])
              * jnp.dot(oa_ref[0], wb_ref[0], preferred_element_type=F32)
              + _sigmoid(gz1_ref[0])
              * jnp.dot(ob_ref[0], wb_ref[1], preferred_element_type=F32)
              + _sigmoid(gz2_ref[0])
              * jnp.dot(o_c.astype(BF16), wb_ref[2], preferred_element_type=F32))
    out = jnp.dot(merged.astype(BF16), wo_ref[...], preferred_element_type=F32)
    o_ref[0] = x_ref[0] + gt_ref[0] * out


def _merge(y, bonus, proj, o_a, o_b, x, gate, gn_w, gn_b, ones_bd, g_comb, w_branch, w_out, *, tm=512):
    b, s, d = x.shape
    gz_per = d // COLBLK
    tok = lambda width, cb: pl.BlockSpec((1, tm, width), lambda bi, i, cb=cb: (bi, i, cb))
    const = lambda shape: pl.BlockSpec(shape, lambda bi, i: (0,) * len(shape))
    return pl.pallas_call(
        _merge_kernel,
        grid=(b, s // tm),
        in_specs=[pl.BlockSpec((2, 1, tm, COLBLK), lambda bi, i: (0, bi, i, 0)),
                  tok(COLBLK, 0),
                  tok(COLBLK, CB_LORA),
                  tok(d, CB_GZ // gz_per), tok(d, CB_GZ // gz_per + 1), tok(d, CB_GZ // gz_per + 2),
                  tok(COLBLK, 0), tok(COLBLK, 0), tok(d, 0),
                  pl.BlockSpec((1, 1, d), lambda bi, i: (bi, 0, 0)),
                  const((1, COLBLK)), const((1, COLBLK)), const((COLBLK, COLBLK)),
                  const((COLBLK, COLBLK)), const((3, COLBLK, d)), const((d, d))],
        out_specs=pl.BlockSpec((1, tm, d), lambda bi, i: (bi, i, 0)),
        out_shape=jax.ShapeDtypeStruct((b, s, d), F32),
        compiler_params=_cparams(("parallel", "parallel")),
        name="merge_out",
    )(y, bonus, proj, proj, proj, proj, o_a, o_b, x, gate, gn_w, gn_b, ones_bd, g_comb, w_branch, w_out)


def _split_w_in(w_in):
    n_lora = 4 * DECAY_LORA + GATE_LORA
    n_rest = N_ATTN + CB_LORA * COLBLK + n_lora
    rest, gates = w_in[..., N_ATTN:n_rest], w_in[..., n_rest:]
    zpad = jnp.zeros(w_in.shape[:-1] + (COLBLK - n_lora,), w_in.dtype)
    return (w_in[..., :N_ATTN].astype(BF16),
            jnp.concatenate([rest, zpad, gates], axis=-1).astype(BF16))


def _lora_params(mu_w, mu_a, w_up, a_up, g_up):
    r = DECAY_LORA
    depth = mu_w.shape[0]
    mu_l = jnp.zeros((depth, 2, COLBLK), F32)
    w_comb = jnp.zeros((depth, 2, 2 * LANES, 2 * COLBLK), F32)
    for d in range(2):
        wcol, acol = d * r, 2 * r + d * r
        mu_l = mu_l.at[:, d, wcol:wcol + r].set(mu_w[:, d]).at[:, d, acol:acol + r].set(mu_a[:, d])
        w_comb = w_comb.at[:, d, wcol:wcol + r, :COLBLK].set(w_up[:, d])
        w_comb = w_comb.at[:, d, acol:acol + r, COLBLK:].set(a_up[:, d])
    g_comb = jnp.zeros((depth, COLBLK, COLBLK), F32).at[:, 4 * r:4 * r + GATE_LORA].set(g_up)
    return mu_l, w_comb.astype(BF16), g_comb.astype(BF16)


def _head_block_ones():
    seg = np.arange(COLBLK) // HEAD_DIM
    return jnp.asarray(seg[:, None] == seg[None, :], dtype=BF16)


def kernel(x, c, positions, ada_w, ada_b, norm_gains, ffn_wi, ffn_wo, w_in, rpb, mu_rkv, mu_w, mu_a,
           w0, w_up, a0, a_up, g_up, k_k, k_a, r_k, gn_w, gn_b, w_branch, w_out, final_norm):
    depth = ada_w.shape[0]
    b, s, d = x.shape
    assert d == 2 * COLBLK and s % 1024 == 0 and s % (GRID_W * NA_WIN_ROWS) == 0

    mod = _modulation(c, ada_w, ada_b).reshape(depth, b, N_MOD, 1, d)
    cos, s1, s2 = _rope_tables(positions)
    ones_bd = _head_block_ones()
    bias_tabs = _natten_bias_tables(rpb, s // GRID_W)
    dil_masks = _dilated_masks(s)
    ffn_wi_bf, ffn_wo_bf = ffn_wi.astype(BF16), ffn_wo.astype(BF16)
    w_branch_bf, w_out_bf = w_branch.astype(BF16), w_out.astype(BF16)
    w_attn, w_rest = _split_w_in(w_in)
    mu_l, w_comb, g_comb = _lora_params(mu_w, mu_a, w_up, a_up, g_up)
    mu_r, mu_v = _param_to_chain_lanes(mu_rkv[:, :, 0], b), _param_to_chain_lanes(mu_rkv[:, :, 2], b)
    row = lambda t: t.reshape(1, -1)

    for l in range(depth):
        sh1, sc1, gt1, sh2, sc2, gt2, sh3, sc3, gt3 = (mod[l, :, i] for i in range(N_MOD))

        x = _ffn(x, norm_gains[l, 0], sh1, sc1, gt1, ffn_wi_bf[l, 0], ffn_wo_bf[l, 0])

        attn, proj = _in_proj(x, norm_gains[l, 1], sh2, sc2, w_attn[l], w_rest[l])
        o_a = _dilated_attention(attn, cos, s1, s2, dil_masks)
        o_b = _neighborhood_attention(attn, bias_tabs[l])
        decay, k2, a_vec, b_vec, bonus, pr, pv = _rwkv_pre(
            proj, mu_rkv[l], mu_l[l], w_comb[l], w0[l], a0[l], row(k_k[l]), row(k_a[l]), row(r_k[l]),
            ones_bd)
        y = _rwkv_scan(_to_chain_lanes(pr), _to_chain_lanes(decay), _to_chain_lanes(k2),
                       _to_chain_lanes(pv), _to_chain_lanes(a_vec), _to_chain_lanes(b_vec),
                       mu_r[l], mu_v[l])
        y = _from_chain_lanes(y, b)
        x = _merge(y, bonus, proj, o_a, o_b, x, gt2, row(gn_w[l]), row(gn_b[l]), ones_bd, g_comb[l],
                   w_branch_bf[l], w_out_bf[l])

        x = _ffn(x, norm_gains[l, 2], sh3, sc3, gt3, ffn_wi_bf[l, 1], ffn_wo_bf[l, 1],
                 final_norm if l == depth - 1 else None)
    return x
```

```python
import functools

import numpy as np
import jax
import jax.numpy as jnp
from jax import lax
from jax.experimental import pallas as pl
from jax.experimental.pallas import tpu as pltpu

F32 = jnp.float32
BF16 = jnp.bfloat16

HEAD_DIM = 64
N_HEADS = 8
W_MIX = N_HEADS * HEAD_DIM
DILATED_PAIRS = ((128, 1), (512, 4), (2048, 16))
QBLK = 128
ROT_DIM = HEAD_DIM // 4
ROPE_THETA = 500000.0
GRID_W = 64
NA_WIN_ROWS = 8
NA_WIN_COLS = 16
DECAY_LORA = 64
ICLR_LORA = 64
GATE_LORA = 128
DECAY_SCALE = 0.6065306597126334
N_MOD = 9
RMS_EPS = 1e-6
GN_EPS = 64e-5
NEG = -1e30

LANES = 128
SUBLANES = 8
VMEM_LIMIT = 48 * 1024 * 1024
DILATED_GROUP = 4
NATTEN_GROUP = 16
PACK_TILE = 256

COLBLK = 512
CB_AQ, CB_AK, CB_AV = 0, 1, 2
CB_NQ, CB_NK, CB_NV = 3, 4, 5
N_ATTN = 6 * COLBLK
CB_PR, CB_PK, CB_PV = 0, 1, 2
CB_LORA = 3
CB_GZ = 4


def _cparams(sem):
    return pltpu.CompilerParams(dimension_semantics=sem, vmem_limit_bytes=VMEM_LIMIT)


def _sigmoid(x):
    return 0.5 * jnp.tanh(0.5 * x) + 0.5


def _mod_kernel(c_ref, w_ref, b_ref, o_ref):
    c = c_ref[...]
    cs = (c * _sigmoid(c)).astype(BF16)
    o_ref[0] = jnp.dot(cs, w_ref[0].astype(BF16), preferred_element_type=F32) + b_ref[0]


def _modulation(c, ada_w, ada_b):
    depth, d, nd = ada_w.shape
    b = c.shape[0]
    return pl.pallas_call(
        _mod_kernel,
        grid=(depth, nd // d),
        in_specs=[pl.BlockSpec((b, d), lambda l, j: (0, 0)),
                  pl.BlockSpec((1, d, d), lambda l, j: (l, 0, j)),
                  pl.BlockSpec((1, 1, d), lambda l, j: (l, 0, j))],
        out_specs=pl.BlockSpec((1, b, d), lambda l, j: (l, 0, j)),
        out_shape=jax.ShapeDtypeStruct((depth, b, nd), F32),
        compiler_params=_cparams(("parallel", "parallel")),
        name="adaln_mod",
    )(c, ada_w, ada_b.reshape(depth, 1, nd))


def _norm_mod(x, gain, shift, scale):
    ms = jnp.mean(x * x, axis=-1, keepdims=True)
    y = x * lax.rsqrt(ms + RMS_EPS) * gain
    return y * (1.0 + scale) + shift


def _ffn_kernel(*refs, final, tf):
    if final:
        x_ref, g_ref, sh_ref, sc_ref, gt_ref, wi_ref, wo_ref, fin_ref, o_ref = refs
    else:
        x_ref, g_ref, sh_ref, sc_ref, gt_ref, wi_ref, wo_ref, o_ref = refs
    x = x_ref[0]
    h = _norm_mod(x, g_ref[...], sh_ref[0], sc_ref[0]).astype(BF16)
    dff = wo_ref.shape[0]
    acc = None
    for lo in range(0, dff, tf):
        gate = jnp.dot(h, wi_ref[:, lo:lo + tf], preferred_element_type=F32)
        up = jnp.dot(h, wi_ref[:, dff + lo:dff + lo + tf], preferred_element_type=F32)
        act = ((gate * _sigmoid(gate)) * up).astype(BF16)
        part = jnp.dot(act, wo_ref[lo:lo + tf, :], preferred_element_type=F32)
        acc = part if acc is None else acc + part
    y = x + 0.5 * gt_ref[0] * acc
    if final:
        ms = jnp.mean(y * y, axis=-1, keepdims=True)
        y = y * lax.rsqrt(ms + RMS_EPS) * fin_ref[...]
    o_ref[0] = y


def _ffn(x, gain, shift, scale, gate, wi, wo, final_gain=None, *, tm=512, tf=256):
    b, s, d = x.shape
    dff = wo.shape[0]
    final = final_gain is not None
    vec = pl.BlockSpec((1, 1, d), lambda bi, i: (bi, 0, 0))
    resident = lambda shape: pl.BlockSpec(shape, lambda bi, i: (0, 0), pipeline_mode=pl.Buffered(1))
    in_specs = [pl.BlockSpec((1, tm, d), lambda bi, i: (bi, i, 0)),
                pl.BlockSpec((1, d), lambda bi, i: (0, 0)),
                vec, vec, vec,
                resident((d, 2 * dff)), resident((dff, d))]
    args = [x, gain.reshape(1, d), shift, scale, gate, wi, wo]
    if final:
        in_specs.append(pl.BlockSpec((1, d), lambda bi, i: (0, 0)))
        args.append(final_gain.reshape(1, d))
    return pl.pallas_call(
        functools.partial(_ffn_kernel, final=final, tf=tf),
        grid=(b, s // tm),
        in_specs=in_specs,
        out_specs=pl.BlockSpec((1, tm, d), lambda bi, i: (bi, i, 0)),
        out_shape=jax.ShapeDtypeStruct((b, s, d), F32),
        compiler_params=_cparams(("parallel", "parallel")),
        name="ffn_final" if final else "ffn",
    )(*args)


def _win_kernel(x_ref, g_ref, sh_ref, sc_ref, wa_ref, wr_ref, oa_ref, or_ref, h_scr):
    @pl.when(pl.program_id(2) == 0)
    def _():
        h = _norm_mod(x_ref[0], g_ref[...], sh_ref[0], sc_ref[0])
        h_scr[...] = h.astype(BF16)

    h = h_scr[...]
    oa_ref[0] = jnp.dot(h, wa_ref[...], preferred_element_type=F32).astype(BF16)
    or_ref[0] = jnp.dot(h, wr_ref[...], preferred_element_type=F32)


def _in_proj(x, gain, shift, scale, w_attn, w_rest, *, tm=1024, nj=4):
    b, s, d = x.shape
    ta, tr = w_attn.shape[1] // nj, w_rest.shape[1] // nj
    vec = pl.BlockSpec((1, 1, d), lambda bi, i, j: (bi, 0, 0))
    return pl.pallas_call(
        _win_kernel,
        grid=(b, s // tm, nj),
        in_specs=[pl.BlockSpec((1, tm, d), lambda bi, i, j: (bi, i, 0)),
                  pl.BlockSpec((1, d), lambda bi, i, j: (0, 0)),
                  vec, vec,
                  pl.BlockSpec((d, ta), lambda bi, i, j: (0, j)),
                  pl.BlockSpec((d, tr), lambda bi, i, j: (0, j))],
        out_specs=[pl.BlockSpec((1, tm, ta), lambda bi, i, j: (bi, i, j)),
                   pl.BlockSpec((1, tm, tr), lambda bi, i, j: (bi, i, j))],
        out_shape=[jax.ShapeDtypeStruct((b, s, w_attn.shape[1]), BF16),
                   jax.ShapeDtypeStruct((b, s, w_rest.shape[1]), F32)],
        scratch_shapes=[pltpu.VMEM((tm, d), BF16)],
        compiler_params=_cparams(("parallel", "parallel", "arbitrary")),
        name="in_proj",
    )(x, gain.reshape(1, d), shift, scale, w_attn, w_rest)


def _dilated_kernel(q_ref, k_ref, v_ref, cos_ref, s1_ref, s2_ref, *rest, seq, pad):
    n_pat = len(DILATED_PAIRS)
    mask_refs = rest[:n_pat]
    o_ref, q_scr, k_scr, v_scr, og_scr, lse_scr = rest[n_pat:]
    cos, s1, s2 = cos_ref[0], s1_ref[0], s2_ref[0]

    def rotary(t):
        return (t * cos + pltpu.roll(t, LANES - ROT_DIM // 2, axis=1) * s1
                + pltpu.roll(t, ROT_DIM // 2, axis=1) * s2)

    q_scr[...] = rotary(q_ref[0].astype(F32)) * (HEAD_DIM ** -0.5)
    zeros = jnp.zeros((pad, LANES), F32)
    k_scr[pl.ds(0, pad), :] = zeros
    k_scr[pl.ds(pad + seq, pad), :] = zeros
    v_scr[pl.ds(0, pad), :] = zeros
    v_scr[pl.ds(pad + seq, pad), :] = zeros
    k_scr[pl.ds(pad, seq), :] = rotary(k_ref[0].astype(F32))
    v_scr[pl.ds(pad, seq), :] = v_ref[0].astype(F32)

    lane = lax.broadcasted_iota(jnp.int32, (QBLK, LANES), 1)
    head0 = lane < HEAD_DIM

    for g, (window, dil) in enumerate(DILATED_PAIRS):
        nblk, lead, nkeys = _dilated_geometry(seq, window, dil)

        def group(gi, carry, g=g, dil=dil, lead=lead, nkeys=nkeys, nblk=nblk):
            units = []
            for u in range(DILATED_GROUP):
                idx = gi * DILATED_GROUP + u
                cls = idx // nblk
                n = idx % nblk
                q_start = cls + dil * QBLK * n
                k_start = pad + cls + dil * (QBLK * n - lead)
                if dil == 1:
                    q_rows = pl.ds(q_start, QBLK)
                    k_rows = pl.ds(k_start, nkeys)
                else:
                    q_rows = pl.ds(q_start, QBLK, stride=dil)
                    k_rows = pl.ds(k_start, nkeys, stride=dil)
                which = jnp.where(n == 0, 0, jnp.where(n == nblk - 1, 2, 1)) if nblk > 1 else 0
                units.append((q_rows, q_scr[q_rows, :], k_scr[k_rows, :].astype(BF16),
                              v_scr[k_rows, :].astype(BF16), which))
            scores = []
            for _, q, kw, _, which in units:
                for hmask in (head0, jnp.logical_not(head0)):
                    qh = jnp.where(hmask, q, 0.0).astype(BF16)
                    s = lax.dot_general(qh, kw, (((1,), (1,)), ((), ())),
                                        preferred_element_type=F32)
                    scores.append(s + mask_refs[g][which])
            probs, invs, lses = [], [], []
            for s in scores:
                m = jnp.max(s, axis=-1, keepdims=True)
                p = jnp.exp(s - m)
                den = jnp.sum(p, axis=-1, keepdims=True)
                probs.append(p.astype(BF16))
                invs.append(1.0 / den)
                lses.append(m + jnp.log(den))
            for u, (q_rows, _, _, vw, _) in enumerate(units):
                o0 = jnp.dot(probs[2 * u], vw, preferred_element_type=F32) * invs[2 * u]
                o1 = jnp.dot(probs[2 * u + 1], vw, preferred_element_type=F32) * invs[2 * u + 1]
                og_scr[g, q_rows, :] = jnp.where(head0, o0, o1)
                lse_scr[g, q_rows, :] = jnp.where(head0, lses[2 * u], lses[2 * u + 1])
            return carry

        lax.fori_loop(0, dil * nblk // DILATED_GROUP, group, 0)

    l0, l1, l2 = lse_scr[0], lse_scr[1], lse_scr[2]
    m = jnp.maximum(jnp.maximum(l0, l1), l2)
    e0, e1, e2 = jnp.exp(l0 - m), jnp.exp(l1 - m), jnp.exp(l2 - m)
    out = (e0 * og_scr[0] + e1 * og_scr[1] + e2 * og_scr[2]) * (1.0 / (e0 + e1 + e2))
    o_ref[0] = out.astype(o_ref.dtype)


def _dilated_geometry(seq, window, dil):
    radius = window // (2 * dil)
    nblk = seq // dil // QBLK
    lead = radius if nblk > 1 else 0
    return nblk, lead, QBLK + 2 * lead


def _dilated_masks(seq):
    out = []
    for window, dil in DILATED_PAIRS:
        radius = window // (2 * dil)
        nblk, lead, nkeys = _dilated_geometry(seq, window, dil)
        qi = np.arange(QBLK)[:, None]
        kj = np.arange(nkeys)[None, :]
        band = np.abs(kj - lead - qi) <= radius
        tabs = []
        for n in ((0, 1, nblk - 1) if nblk > 1 else (0,)):
            key_pos = QBLK * n + kj - lead
            tabs.append(band & (key_pos >= 0) & (key_pos < nblk * QBLK))
        out.append(jnp.asarray(np.where(np.stack(tabs), 0.0, NEG), F32))
    return out


def _dilated_attention(attn, cos, s1, s2, masks):
    b, s, _ = attn.shape
    pad = max((w // (2 * d)) * d for w, d in DILATED_PAIRS)
    hp = W_MIX // LANES
    per_cb = COLBLK // LANES
    col = lambda cb: pl.BlockSpec((1, s, LANES), lambda bi, h, cb=cb: (bi, 0, cb * per_cb + h))
    tab = pl.BlockSpec((1, s, LANES), lambda bi, h: (bi, 0, 0))
    const = lambda shape: pl.BlockSpec(shape, lambda bi, h: (0,) * len(shape))
    return pl.pallas_call(
        functools.partial(_dilated_kernel, seq=s, pad=pad),
        grid=(b, hp),
        in_specs=[col(CB_AQ), col(CB_AK), col(CB_AV), tab, tab, tab] + [const(m.shape) for m in masks],
        out_specs=pl.BlockSpec((1, s, LANES), lambda bi, h: (bi, 0, h)),
        out_shape=jax.ShapeDtypeStruct((b, s, W_MIX), BF16),
        scratch_shapes=[pltpu.VMEM((s, LANES), F32),
                        pltpu.VMEM((s + 2 * pad, LANES), F32),
                        pltpu.VMEM((s + 2 * pad, LANES), F32),
                        pltpu.VMEM((len(DILATED_PAIRS), s, LANES), F32),
                        pltpu.VMEM((len(DILATED_PAIRS), s, LANES), F32)],
        compiler_params=_cparams(("parallel", "parallel")),
        name="dilated_attn",
    )(attn, attn, attn, cos, s1, s2, *masks)


def _rope_tables(positions):
    half = ROT_DIM // 2
    lane = np.arange(LANES) % HEAD_DIM
    inv_freq = ROPE_THETA ** (-jnp.asarray(lane % half, F32) * 2.0 / ROT_DIM)
    ang = positions.astype(F32)[..., None] * inv_freq
    cos, sin = jnp.cos(ang), jnp.sin(ang)
    return (jnp.where(lane < ROT_DIM, cos, 1.0), jnp.where(lane < half, -sin, 0.0),
            jnp.where((lane >= half) & (lane < ROT_DIM), sin, 0.0))


def _natten_kernel(q_ref, k_ref, v_ref, bias_ref, o_ref, *, rows, win_rows):
    lane = lax.broadcasted_iota(jnp.int32, (GRID_W, LANES), 1)
    head0 = lane < HEAD_DIM
    scale = HEAD_DIM ** -0.5
    nkeys = win_rows * GRID_W

    def group(gi, carry):
        units = []
        for u in range(NATTEN_GROUP):
            r = gi * NATTEN_GROUP + u
            r_start = jnp.clip(r - win_rows // 2, 0, rows - win_rows)
            q_rows = pl.ds(pl.multiple_of(r * GRID_W, GRID_W), GRID_W)
            k_rows = pl.ds(pl.multiple_of(r_start * GRID_W, GRID_W), nkeys)
            units.append((q_rows, r - r_start, q_ref[0, q_rows, :],
                          k_ref[0, k_rows, :], v_ref[0, k_rows, :]))
        scores = []
        for _, delta, q, kw, _ in units:
            q = q * scale
            zero = jnp.zeros_like(q)
            q2 = jnp.concatenate([jnp.where(head0, q, zero), jnp.where(head0, zero, q)], axis=0)
            s2 = lax.dot_general(q2, kw, (((1,), (1,)), ((), ())), preferred_element_type=F32)
            scores += [s2[:GRID_W] + bias_ref[0, delta], s2[GRID_W:] + bias_ref[1, delta]]
        probs, invs = [], []
        for s in scores:
            m = jnp.max(s, axis=-1, keepdims=True)
            p = jnp.exp(s - m)
            probs.append(p.astype(BF16))
            invs.append(1.0 / jnp.sum(p, axis=-1, keepdims=True))
        for u, (q_rows, _, _, _, vw) in enumerate(units):
            o2 = jnp.dot(jnp.concatenate(probs[2 * u:2 * u + 2], axis=0), vw,
                         preferred_element_type=F32)
            o_ref[0, q_rows, :] = jnp.where(head0, o2[:GRID_W] * invs[2 * u],
                                            o2[GRID_W:] * invs[2 * u + 1]).astype(o_ref.dtype)
        return carry

    lax.fori_loop(0, rows // NATTEN_GROUP, group, 0)


def _natten_bias_tables(rpb, rows):
    wr = min(NA_WIN_ROWS, rows)
    wc = NA_WIN_COLS
    cols = np.arange(GRID_W)
    c_start = np.clip(cols - wc // 2, 0, GRID_W - wc)
    col_in = (cols[None, :] >= c_start[:, None]) & (cols[None, :] < c_start[:, None] + wc)
    coff = np.clip(cols[None, :] - cols[:, None], -(wc - 1), wc - 1) + wc - 1
    onehot = (coff[None] == np.arange(2 * wc - 1)[:, None, None]).astype(np.float32)
    by_col = jnp.einsum("lhrc,cqk->lhqrk", rpb.astype(F32), onehot, precision=lax.Precision.HIGHEST)
    by_col = jnp.where(col_in[:, None, :], by_col, NEG)
    top = NA_WIN_ROWS - 1
    tabs = jnp.stack([by_col[:, :, :, top - dl:top - dl + wr] for dl in range(wr)], axis=2)
    return tabs.reshape(rpb.shape[0], rpb.shape[1], wr, GRID_W, wr * GRID_W)


def _neighborhood_attention(attn, bias_tab):
    b, s, _ = attn.shape
    rows = s // GRID_W
    wr = min(NA_WIN_ROWS, rows)
    hp = W_MIX // LANES
    per_cb = COLBLK // LANES
    col = lambda cb: pl.BlockSpec((1, s, LANES), lambda bi, h, cb=cb: (bi, 0, cb * per_cb + h))
    return pl.pallas_call(
        functools.partial(_natten_kernel, rows=rows, win_rows=wr),
        grid=(b, hp),
        in_specs=[col(CB_NQ), col(CB_NK), col(CB_NV),
                  pl.BlockSpec((LANES // HEAD_DIM, wr, GRID_W, wr * GRID_W),
                               lambda bi, h: (h, 0, 0, 0))],
        out_specs=pl.BlockSpec((1, s, LANES), lambda bi, h: (bi, 0, h)),
        out_shape=jax.ShapeDtypeStruct((b, s, W_MIX), BF16),
        compiler_params=_cparams(("parallel", "parallel")),
        name="natten",
    )(attn, attn, attn, bias_tab)


def _seg_sum(x, ones_bd):
    return jnp.dot(x.astype(BF16), ones_bd, preferred_element_type=F32)


def _pack_rows(x):
    half = x.shape[0] // 2
    bits = lax.bitcast_convert_type(x.astype(BF16).astype(F32), jnp.uint32)
    return (bits[:half] >> 16) | (bits[half:] & jnp.uint32(0xFFFF0000))


def _unpack_rows(u, high):
    shift = jnp.where(high, 0, 16).astype(jnp.uint32)
    return lax.bitcast_convert_type((u << shift) & jnp.uint32(0xFFFF0000), F32)


def _rwkv_pre_kernel(r_ref, k_ref, v_ref, l_ref,
                     rp_ref, kp_ref, vp_ref, lp_ref,
                     rn_ref, kn_ref, vn_ref, ln_ref,
                     mu_ref, mul_ref, wc_ref, w0_ref, a0_ref, kk_ref, ka_ref, rk_ref, ones_ref,
                     w_o, k_o, a_o, b_o, bonus_o, pr_o, pv_o, *, tm):
    i = pl.program_id(1)
    first = i == 0
    last = i == pl.num_programs(1) - 1
    half = COLBLK // 2
    row = lax.broadcasted_iota(jnp.int32, (tm, half), 0)
    ones_bd = ones_ref[:half, :half]

    def neighbour(x, p_ref, n_ref, cols, d):
        if d == 0:
            edge = jnp.where(first, 0.0, p_ref[0, SUBLANES - 1:SUBLANES, cols])
            return jnp.where(row == 0, edge, pltpu.roll(x, 1, axis=0))
        edge = jnp.where(last, 0.0, n_ref[0, 0:1, cols])
        return jnp.where(row == tm - 1, edge, pltpu.roll(x, tm - 1, axis=0))

    lora = slice(0, half)
    pl_ = l_ref[0, :, lora]
    tw, ta = [], []
    for d in range(2):
        xl = pl_ + (neighbour(pl_, lp_ref, ln_ref, lora, d) - pl_) * mul_ref[d:d + 1, lora]
        tw.append(jnp.tanh(xl[:, :LANES]).astype(BF16))
        ta.append(xl[:, LANES:].astype(BF16))

    for c in range(COLBLK // half):
        cols = slice(c * half, (c + 1) * half)
        pr, pk, pv = r_ref[0, :, cols], k_ref[0, :, cols], v_ref[0, :, cols]
        bonus = jnp.zeros((tm, half), F32)
        for d in range(2):
            r = pr + (neighbour(pr, rp_ref, rn_ref, cols, d) - pr) * mu_ref[d, 0:1, cols]
            k = pk + (neighbour(pk, kp_ref, kn_ref, cols, d) - pk) * mu_ref[d, 1:2, cols]
            v = pv + (neighbour(pv, vp_ref, vn_ref, cols, d) - pv) * mu_ref[d, 2:3, cols]
            wz = w0_ref[d:d + 1, cols] + jnp.dot(tw[d], wc_ref[d, :LANES, cols],
                                                 preferred_element_type=F32)
            az = a0_ref[d:d + 1, cols] + jnp.dot(
                ta[d], wc_ref[d, LANES:2 * LANES, COLBLK + c * half:COLBLK + (c + 1) * half],
                preferred_element_type=F32)
            decay = jnp.exp(-DECAY_SCALE * _sigmoid(wz))
            a = _sigmoid(az)
            kk = k * kk_ref[:, cols]
            kk = kk * lax.rsqrt(jnp.maximum(_seg_sum(kk * kk, ones_bd), 1e-24))
            k2 = k * (1.0 + (a - 1.0) * ka_ref[:, cols])
            bonus = bonus + _seg_sum(r * k2 * rk_ref[:, cols], ones_bd) * v
            w_o[d, 0, :, cols] = decay
            k_o[d, 0, :, cols] = _pack_rows(k2)
            a_o[d, 0, :, cols] = _pack_rows(-kk)
            b_o[d, 0, :, cols] = _pack_rows(kk * a)
        bonus_o[0, :, cols] = bonus
        pr_o[0, :, cols] = _pack_rows(pr)
        pv_o[0, :, cols] = pv


def _rwkv_pre(proj, mu_rkv, mu_lora, w_comb, w0, a0, k_k, k_a, r_k, ones_bd, *, tm=PACK_TILE):
    b, s, _ = proj.shape
    nsub = tm // SUBLANES
    last_sub = s // SUBLANES - 1
    main = lambda cb: pl.BlockSpec((1, tm, COLBLK), lambda bi, i, cb=cb: (bi, i, cb))
    prev = lambda cb: pl.BlockSpec(
        (1, SUBLANES, COLBLK), lambda bi, i, cb=cb: (bi, jnp.maximum(i * nsub - 1, 0), cb))
    nxt = lambda cb: pl.BlockSpec(
        (1, SUBLANES, COLBLK), lambda bi, i, cb=cb: (bi, jnp.minimum((i + 1) * nsub, last_sub), cb))
    cbs = (CB_PR, CB_PK, CB_PV, CB_LORA)
    const = lambda shape: pl.BlockSpec(shape, lambda bi, i: (0,) * len(shape))
    tok = lambda rows: pl.BlockSpec((1, rows, COLBLK), lambda bi, i: (bi, i, 0))
    dir_out = lambda rows: pl.BlockSpec((2, 1, rows, COLBLK), lambda bi, i: (0, bi, i, 0))
    u32 = jnp.uint32
    return pl.pallas_call(
        functools.partial(_rwkv_pre_kernel, tm=tm),
        grid=(b, s // tm),
        in_specs=([main(cb) for cb in cbs] + [prev(cb) for cb in cbs] + [nxt(cb) for cb in cbs]
                  + [const((2, 3, COLBLK)), const((2, COLBLK)), const((2, 2 * LANES, 2 * COLBLK)),
                     const((2, COLBLK)), const((2, COLBLK)), const((1, COLBLK)), const((1, COLBLK)),
                     const((1, COLBLK)), const((COLBLK, COLBLK))]),
        out_specs=[dir_out(tm), dir_out(tm // 2), dir_out(tm // 2), dir_out(tm // 2),
                   tok(tm), tok(tm // 2), tok(tm)],
        out_shape=[jax.ShapeDtypeStruct((2, b, s, COLBLK), F32)]
                  + [jax.ShapeDtypeStruct((2, b, s // 2, COLBLK), u32)] * 3
                  + [jax.ShapeDtypeStruct((b, s, COLBLK), F32),
                     jax.ShapeDtypeStruct((b, s // 2, COLBLK), u32),
                     jax.ShapeDtypeStruct((b, s, COLBLK), F32)],
        compiler_params=_cparams(("parallel", "parallel")),
        name="rwkv_pre",
    )(*([proj] * 12), mu_rkv, mu_lora, w_comb, w0, a0, k_k, k_a, r_k, ones_bd)


def _rwkv_scan_kernel(pr_ref, w_ref, k_ref, pv_ref, a_ref, b_ref, an_ref, mur_ref, muv_ref, y_ref,
                      st_scr, sa_scr, rows_scr, prevr_scr, prevv_scr, *, tt, seq):
    d = pl.program_id(0)
    i = pl.program_id(1)
    n = HEAD_DIM
    nt = seq // tt
    chunks_per_tile = PACK_TILE // tt
    chunk = i + d * (nt - 1 - 2 * i)
    high = chunk % chunks_per_tile >= chunks_per_tile // 2
    t_after = jnp.where(d == 0, (i + 1) * tt, (nt - 1 - i) * tt - 1)
    high_after = t_after % PACK_TILE >= PACK_TILE // 2

    @pl.when(i == 0)
    def _():
        st_scr[...] = jnp.zeros_like(st_scr)
        sa_scr[...] = jnp.zeros_like(sa_scr)
        prevr_scr[...] = jnp.zeros_like(prevr_scr)
        prevv_scr[...] = jnp.zeros_like(prevv_scr)

    mu_r, mu_v = mur_ref[0], muv_ref[0]

    def row(slot, q, k):
        return jnp.broadcast_to(rows_scr[slot, q, k // SUBLANES, pl.ds(k % SUBLANES, 1), :],
                                sa_scr.shape[1:])

    def step(j, t, a_next, sa, gam, pr_prev, pv_prev):
        slot = j % 2
        gam = gam * w_ref[0, t]
        inv = 1.0 / gam
        pr, pv = _unpack_rows(pr_ref[t], high), pv_ref[t]
        r = pr + (pr_prev - pr) * mu_r
        v = pv + (pv_prev - pv) * mu_v
        rows_scr[slot, 0] = _unpack_rows(b_ref[0, t], high) * inv
        rows_scr[slot, 1] = _unpack_rows(k_ref[0, t], high) * inv
        rows_scr[slot, 2] = r * gam
        rows_scr[slot, 3] = a_next * gam
        y = jnp.zeros_like(v)
        sa_next = [jnp.zeros_like(v), jnp.zeros_like(v)]
        for k in range(n):
            new = st_scr[k] + (sa * row(slot, 0, k) + v * row(slot, 1, k))
            st_scr[k] = new
            y = y + new * row(slot, 2, k)
            sa_next[k % 2] = sa_next[k % 2] + new * row(slot, 3, k)
        y_ref[0, t] = y
        return sa_next[0] + sa_next[1], gam, pr, pv

    def body(j, carry):
        t = j + d * (tt - 1 - 2 * j)
        return step(j, t, _unpack_rows(a_ref[0, t + 1 - 2 * d], high), *carry)

    carry = (sa_scr[...], jnp.ones(sa_scr.shape, F32), prevr_scr[...], prevv_scr[...])
    carry = lax.fori_loop(0, tt - 1, body, carry)
    sa, gam, pr, pv = step(tt - 1, (1 - d) * (tt - 1), _unpack_rows(an_ref[0, 0], high_after), *carry)
    sa_scr[...] = sa
    prevr_scr[...] = pr
    prevv_scr[...] = pv
    rows_scr[0, 0] = gam
    for k in range(n):
        st_scr[k] = st_scr[k] * row(0, 0, k)


def _rwkv_scan(pr, w, k, pv, a, b, mu_r, mu_v, *, tt=64):
    s, n, c = pv.shape
    nt = s // tt
    cpt = PACK_TILE // tt
    tile = (n // SUBLANES, SUBLANES, c)
    order = lambda d, i: i + d * (nt - 1 - 2 * i)
    packed = lambda ci: (ci // cpt) * (cpt // 2) + ci % (cpt // 2)
    dir_spec = pl.BlockSpec((1, tt) + tile, lambda d, i: (d, order(d, i), 0, 0, 0))
    shared_spec = pl.BlockSpec((tt,) + tile, lambda d, i: (order(d, i), 0, 0, 0))
    dir_packed = pl.BlockSpec((1, tt) + tile, lambda d, i: (d, packed(order(d, i)), 0, 0, 0))
    shared_packed = pl.BlockSpec((tt,) + tile, lambda d, i: (packed(order(d, i)), 0, 0, 0))
    per_dir = pl.BlockSpec((1,) + tile, lambda d, i: (d, 0, 0, 0))

    def token_after(d, i):
        t = jnp.where(d == 0, jnp.minimum((i + 1) * tt, s - 1), jnp.maximum((nt - 1 - i) * tt - 1, 0))
        return (d, (t // PACK_TILE) * (PACK_TILE // 2) + t % (PACK_TILE // 2), 0, 0, 0)

    k, a, b = (t.reshape((2, s // 2) + tile) for t in (k, a, b))
    y = pl.pallas_call(
        functools.partial(_rwkv_scan_kernel, tt=tt, seq=s),
        grid=(2, nt),
        in_specs=[shared_packed, dir_spec, dir_packed, shared_spec, dir_packed, dir_packed,
                  pl.BlockSpec((1, 1) + tile, token_after), per_dir, per_dir],
        out_specs=dir_spec,
        out_shape=jax.ShapeDtypeStruct((2, s) + tile, F32),
        scratch_shapes=[pltpu.VMEM((n,) + tile, F32), pltpu.VMEM(tile, F32),
                        pltpu.VMEM((2, 4) + tile, F32), pltpu.VMEM(tile, F32), pltpu.VMEM(tile, F32)],
        compiler_params=_cparams(("parallel", "arbitrary")),
        name="rwkv_scan",
    )(pr.reshape((s // 2,) + tile), w.reshape((2, s) + tile), k, pv.reshape((s,) + tile), a, b, a,
      mu_r.reshape((2,) + tile), mu_v.reshape((2,) + tile))
    return y.reshape(2, s, n, c)


def _to_chain_lanes(t):
    *lead, b, s, _ = t.shape
    k = len(lead)
    t = t.reshape(*lead, b, s, N_HEADS, HEAD_DIM)
    t = t.transpose(*range(k), k + 1, k + 3, k, k + 2)
    return t.reshape(*lead, s, HEAD_DIM, b * N_HEADS)


def _param_to_chain_lanes(p, b):
    lead = p.shape[:-1]
    p = jnp.swapaxes(p.reshape(*lead, N_HEADS, HEAD_DIM), -1, -2)
    return jnp.broadcast_to(p[..., None, :], lead + (HEAD_DIM, b, N_HEADS)).reshape(
        *lead, HEAD_DIM, b * N_HEADS)


def _from_chain_lanes(t, b):
    _, s, _, _ = t.shape
    t = t.reshape(2, s, HEAD_DIM, b, N_HEADS).transpose(0, 3, 1, 4, 2)
    return t.reshape(2, b, s, W_MIX)


def _merge_kernel(y_ref, bonus_ref, l_ref, gz0_ref, gz1_ref, gz2_ref, oa_ref, ob_ref, x_ref, gt_ref,
                  gnw_ref, gnb_ref, ones_ref, gc_ref, wb_ref, wo_ref, o_ref):
    ones_bd = ones_ref[...]
    inv_n = 1.0 / HEAD_DIM
    acc = bonus_ref[0]
    for d in range(2):
        y = y_ref[d, 0]
        mean = _seg_sum(y, ones_bd) * inv_n
        yc = y - mean
        var = _seg_sum(yc * yc, ones_bd) * inv_n
        acc = acc + (yc * lax.rsqrt(var + GN_EPS)) * gnw_ref[...] + gnb_ref[...]
    g = jnp.dot(_sigmoid(l_ref[0]).astype(BF16), gc_ref[...], preferred_element_type=F32)
    o_c = acc * g
    merged = (_sigmoid(gz0_ref[0])
              * jnp.dot(oa_ref[0], wb_ref[0], preferred_element_type=F32)
              + _sigmoid(gz1_ref[0])
              * jnp.dot(ob_ref[0], wb_ref[1], preferred_element_type=F32)
              + _sigmoid(gz2_ref[0])
              * jnp.dot(o_c.astype(BF16), wb_ref[2], preferred_element_type=F32))
    out = jnp.dot(merged.astype(BF16), wo_ref[...], preferred_element_type=F32)
    o_ref[0] = x_ref[0] + gt_ref[0] * out


def _merge(y, bonus, proj, o_a, o_b, x, gate, gn_w, gn_b, ones_bd, g_comb, w_branch, w_out, *, tm=512):
    b, s, d = x.shape
    gz_per = d // COLBLK
    tok = lambda width, cb: pl.BlockSpec((1, tm, width), lambda bi, i, cb=cb: (bi, i, cb))
    const = lambda shape: pl.BlockSpec(shape, lambda bi, i: (0,) * len(shape))
    return pl.pallas_call(
        _merge_kernel,
        grid=(b, s // tm),
        in_specs=[pl.BlockSpec((2, 1, tm, COLBLK), lambda bi, i: (0, bi, i, 0)),
                  tok(COLBLK, 0),
                  tok(COLBLK, CB_LORA),
                  tok(d, CB_GZ // gz_per), tok(d, CB_GZ // gz_per + 1), tok(d, CB_GZ // gz_per + 2),
                  tok(COLBLK, 0), tok(COLBLK, 0), tok(d, 0),
                  pl.BlockSpec((1, 1, d), lambda bi, i: (bi, 0, 0)),
                  const((1, COLBLK)), const((1, COLBLK)), const((COLBLK, COLBLK)),
                  const((COLBLK, COLBLK)), const((3, COLBLK, d)), const((d, d))],
        out_specs=pl.BlockSpec((1, tm, d), lambda bi, i: (bi, i, 0)),
        out_shape=jax.ShapeDtypeStruct((b, s, d), F32),
        compiler_params=_cparams(("parallel", "parallel")),
        name="merge_out",
    )(y, bonus, proj, proj, proj, proj, o_a, o_b, x, gate, gn_w, gn_b, ones_bd, g_comb, w_branch, w_out)


def _split_w_in(w_in):
    n_lora = 4 * DECAY_LORA + GATE_LORA
    n_rest = N_ATTN + CB_LORA * COLBLK + n_lora
    rest, gates = w_in[..., N_ATTN:n_rest], w_in[..., n_rest:]
    zpad = jnp.zeros(w_in.shape[:-1] + (COLBLK - n_lora,), w_in.dtype)
    return (w_in[..., :N_ATTN].astype(BF16),
            jnp.concatenate([rest, zpad, gates], axis=-1).astype(BF16))


def _lora_params(mu_w, mu_a, w_up, a_up, g_up):
    r = DECAY_LORA
    depth = mu_w.shape[0]
    mu_l = jnp.zeros((depth, 2, COLBLK), F32)
    w_comb = jnp.zeros((depth, 2, 2 * LANES, 2 * COLBLK), F32)
    for d in range(2):
        wcol, acol = d * r, 2 * r + d * r
        mu_l = mu_l.at[:, d, wcol:wcol + r].set(mu_w[:, d]).at[:, d, acol:acol + r].set(mu_a[:, d])
        w_comb = w_comb.at[:, d, wcol:wcol + r, :COLBLK].set(w_up[:, d])
        w_comb = w_comb.at[:, d, acol:acol + r, COLBLK:].set(a_up[:, d])
    g_comb = jnp.zeros((depth, COLBLK, COLBLK), F32).at[:, 4 * r:4 * r + GATE_LORA].set(g_up)
    return mu_l, w_comb.astype(BF16), g_comb.astype(BF16)


def _head_block_ones():
    seg = np.arange(COLBLK) // HEAD_DIM
    return jnp.asarray(seg[:, None] == seg[None, :], dtype=BF16)


def kernel(x, c, positions, ada_w, ada_b, norm_gains, ffn_wi, ffn_wo, w_in, rpb, mu_rkv, mu_w, mu_a,
           w0, w_up, a0, a_up, g_up, k_k, k_a, r_k, gn_w, gn_b, w_branch, w_out, final_norm):
    depth = ada_w.shape[0]
    b, s, d = x.shape
    assert d == 2 * COLBLK and s % 1024 == 0 and s % (GRID_W * NA_WIN_ROWS) == 0

    mod = _modulation(c, ada_w, ada_b).reshape(depth, b, N_MOD, 1, d)
    cos, s1, s2 = _rope_tables(positions)
    ones_bd = _head_block_ones()
    bias_tabs = _natten_bias_tables(rpb, s // GRID_W)
    dil_masks = _dilated_masks(s)
    ffn_wi_bf, ffn_wo_bf = ffn_wi.astype(BF16), ffn_wo.astype(BF16)
    w_branch_bf, w_out_bf = w_branch.astype(BF16), w_out.astype(BF16)
    w_attn, w_rest = _split_w_in(w_in)
    mu_l, w_comb, g_comb = _lora_params(mu_w, mu_a, w_up, a_up, g_up)
    mu_r, mu_v = _param_to_chain_lanes(mu_rkv[:, :, 0], b), _param_to_chain_lanes(mu_rkv[:, :, 2], b)
    row = lambda t: t.reshape(1, -1)

    for l in range(depth):
        sh1, sc1, gt1, sh2, sc2, gt2, sh3, sc3, gt3 = (mod[l, :, i] for i in range(N_MOD))

        x = _ffn(x, norm_gains[l, 0], sh1, sc1, gt1, ffn_wi_bf[l, 0], ffn_wo_bf[l, 0])

        attn, proj = _in_proj(x, norm_gains[l, 1], sh2, sc2, w_attn[l], w_rest[l])
        o_a = _dilated_attention(attn, cos, s1, s2, dil_masks)
        o_b = _neighborhood_attention(attn, bias_tabs[l])
        decay, k2, a_vec, b_vec, bonus, pr, pv = _rwkv_pre(
            proj, mu_rkv[l], mu_l[l], w_comb[l], w0[l], a0[l], row(k_k[l]), row(k_a[l]), row(r_k[l]),
            ones_bd)
        y = _rwkv_scan(_to_chain_lanes(pr), _to_chain_lanes(decay), _to_chain_lanes(k2),
                       _to_chain_lanes(pv), _to_chain_lanes(a_vec), _to_chain_lanes(b_vec),
                       mu_r[l], mu_v[l])
        y = _from_chain_lanes(y, b)
        x = _merge(y, bonus, proj, o_a, o_b, x, gt2, row(gn_w[l]), row(gn_b[l]), ones_bd, g_comb[l],
                   w_branch_bf[l], w_out_bf[l])

        x = _ffn(x, norm_gains[l, 2], sh3, sc3, gt3, ffn_wi_bf[l, 1], ffn_wo_bf[l, 1],
                 final_norm if l == depth - 1 else None)
    return x
```

```python
import functools

import numpy as np
import jax
import jax.numpy as jnp
from jax import lax
from jax.experimental import pallas as pl
from jax.experimental.pallas import tpu as pltpu

F32 = jnp.float32
BF16 = jnp.bfloat16

HEAD_DIM = 64
N_HEADS = 8
W_MIX = N_HEADS * HEAD_DIM
DILATED_PAIRS = ((128, 1), (512, 4), (2048, 16))
QBLK = 128
ROT_DIM = HEAD_DIM // 4
ROPE_THETA = 500000.0
GRID_W = 64
NA_WIN_ROWS = 8
NA_WIN_COLS = 16
DECAY_LORA = 64
ICLR_LORA = 64
GATE_LORA = 128
DECAY_SCALE = 0.6065306597126334
N_MOD = 9
RMS_EPS = 1e-6
GN_EPS = 64e-5
NEG = -1e30

LANES = 128
SUBLANES = 8
VMEM_LIMIT = 48 * 1024 * 1024
DILATED_GROUP = 4
NATTEN_GROUP = 16
PACK_TILE = 256

COLBLK = 512
CB_AQ, CB_AK, CB_AV = 0, 1, 2
CB_NQ, CB_NK, CB_NV = 3, 4, 5
N_ATTN = 6 * COLBLK
CB_PR, CB_PK, CB_PV = 0, 1, 2
CB_LORA = 3
CB_GZ = 4


def _cparams(sem):
    return pltpu.CompilerParams(dimension_semantics=sem, vmem_limit_bytes=VMEM_LIMIT)


def _sigmoid(x):
    return 0.5 * jnp.tanh(0.5 * x) + 0.5


def _mod_kernel(c_ref, w_ref, b_ref, o_ref):
    c = c_ref[...]
    cs = (c * _sigmoid(c)).astype(BF16)
    o_ref[0] = jnp.dot(cs, w_ref[0].astype(BF16), preferred_element_type=F32) + b_ref[0]


def _modulation(c, ada_w, ada_b):
    depth, d, nd = ada_w.shape
    b = c.shape[0]
    return pl.pallas_call(
        _mod_kernel,
        grid=(depth, nd // d),
        in_specs=[pl.BlockSpec((b, d), lambda l, j: (0, 0)),
                  pl.BlockSpec((1, d, d), lambda l, j: (l, 0, j)),
                  pl.BlockSpec((1, 1, d), lambda l, j: (l, 0, j))],
        out_specs=pl.BlockSpec((1, b, d), lambda l, j: (l, 0, j)),
        out_shape=jax.ShapeDtypeStruct((depth, b, nd), F32),
        compiler_params=_cparams(("parallel", "parallel")),
        name="adaln_mod",
    )(c, ada_w, ada_b.reshape(depth, 1, nd))


def _norm_mod(x, gain, shift, scale):
    ms = jnp.mean(x * x, axis=-1, keepdims=True)
    y = x * lax.rsqrt(ms + RMS_EPS) * gain
    return y * (1.0 + scale) + shift


def _ffn_kernel(*refs, final, tf):
    if final:
        x_ref, g_ref, sh_ref, sc_ref, gt_ref, wi_ref, wo_ref, fin_ref, o_ref = refs
    else:
        x_ref, g_ref, sh_ref, sc_ref, gt_ref, wi_ref, wo_ref, o_ref = refs
    x = x_ref[0]
    h = _norm_mod(x, g_ref[...], sh_ref[0], sc_ref[0]).astype(BF16)
    dff = wo_ref.shape[0]
    acc = None
    for lo in range(0, dff, tf):
        gate = jnp.dot(h, wi_ref[:, lo:lo + tf], preferred_element_type=F32)
        up = jnp.dot(h, wi_ref[:, dff + lo:dff + lo + tf], preferred_element_type=F32)
        act = ((gate * _sigmoid(gate)) * up).astype(BF16)
        part = jnp.dot(act, wo_ref[lo:lo + tf, :], preferred_element_type=F32)
        acc = part if acc is None else acc + part
    y = x + 0.5 * gt_ref[0] * acc
    if final:
        ms = jnp.mean(y * y, axis=-1, keepdims=True)
        y = y * lax.rsqrt(ms + RMS_EPS) * fin_ref[...]
    o_ref[0] = y


def _ffn(x, gain, shift, scale, gate, wi, wo, final_gain=None, *, tm=512, tf=256):
    b, s, d = x.shape
    dff = wo.shape[0]
    final = final_gain is not None
    vec = pl.BlockSpec((1, 1, d), lambda bi, i: (bi, 0, 0))
    resident = lambda shape: pl.BlockSpec(shape, lambda bi, i: (0, 0), pipeline_mode=pl.Buffered(1))
    in_specs = [pl.BlockSpec((1, tm, d), lambda bi, i: (bi, i, 0)),
                pl.BlockSpec((1, d), lambda bi, i: (0, 0)),
                vec, vec, vec,
                resident((d, 2 * dff)), resident((dff, d))]
    args = [x, gain.reshape(1, d), shift, scale, gate, wi, wo]
    if final:
        in_specs.append(pl.BlockSpec((1, d), lambda bi, i: (0, 0)))
        args.append(final_gain.reshape(1, d))
    return pl.pallas_call(
        functools.partial(_ffn_kernel, final=final, tf=tf),
        grid=(b, s // tm),
        in_specs=in_specs,
        out_specs=pl.BlockSpec((1, tm, d), lambda bi, i: (bi, i, 0)),
        out_shape=jax.ShapeDtypeStruct((b, s, d), F32),
        compiler_params=_cparams(("parallel", "parallel")),
        name="ffn_final" if final else "ffn",
    )(*args)


def _win_kernel(x_ref, g_ref, sh_ref, sc_ref, wa_ref, wr_ref, oa_ref, or_ref, h_scr):
    @pl.when(pl.program_id(2) == 0)
    def _():
        h = _norm_mod(x_ref[0], g_ref[...], sh_ref[0], sc_ref[0])
        h_scr[...] = h.astype(BF16)

    h = h_scr[...]
    oa_ref[0] = jnp.dot(h, wa_ref[...], preferred_element_type=F32).astype(BF16)
    or_ref[0] = jnp.dot(h, wr_ref[...], preferred_element_type=F32)


def _in_proj(x, gain, shift, scale, w_attn, w_rest, *, tm=1024, nj=4):
    b, s, d = x.shape
    ta, tr = w_attn.shape[1] // nj, w_rest.shape[1] // nj
    vec = pl.BlockSpec((1, 1, d), lambda bi, i, j: (bi, 0, 0))
    return pl.pallas_call(
        _win_kernel,
        grid=(b, s // tm, nj),
        in_specs=[pl.BlockSpec((1, tm, d), lambda bi, i, j: (bi, i, 0)),
                  pl.BlockSpec((1, d), lambda bi, i, j: (0, 0)),
                  vec, vec,
                  pl.BlockSpec((d, ta), lambda bi, i, j: (0, j)),
                  pl.BlockSpec((d, tr), lambda bi, i, j: (0, j))],
        out_specs=[pl.BlockSpec((1, tm, ta), lambda bi, i, j: (bi, i, j)),
                   pl.BlockSpec((1, tm, tr), lambda bi, i, j: (bi, i, j))],
        out_shape=[jax.ShapeDtypeStruct((b, s, w_attn.shape[1]), BF16),
                   jax.ShapeDtypeStruct((b, s, w_rest.shape[1]), F32)],
        scratch_shapes=[pltpu.VMEM((tm, d), BF16)],
        compiler_params=_cparams(("parallel", "parallel", "arbitrary")),
        name="in_proj",
    )(x, gain.reshape(1, d), shift, scale, w_attn, w_rest)


def _dilated_kernel(q_ref, k_ref, v_ref, cos_ref, s1_ref, s2_ref, *rest, seq, pad):
    n_pat = len(DILATED_PAIRS)
    mask_refs = rest[:n_pat]
    o_ref, q_scr, k_scr, v_scr, og_scr, lse_scr = rest[n_pat:]
    cos, s1, s2 = cos_ref[0], s1_ref[0], s2_ref[0]

    def rotary(t):
        return (t * cos + pltpu.roll(t, LANES - ROT_DIM // 2, axis=1) * s1
                + pltpu.roll(t, ROT_DIM // 2, axis=1) * s2)

    q_scr[...] = rotary(q_ref[0].astype(F32)) * (HEAD_DIM ** -0.5)
    zeros = jnp.zeros((pad, LANES), F32)
    k_scr[pl.ds(0, pad), :] = zeros
    k_scr[pl.ds(pad + seq, pad), :] = zeros
    v_scr[pl.ds(0, pad), :] = zeros
    v_scr[pl.ds(pad + seq, pad), :] = zeros
    k_scr[pl.ds(pad, seq), :] = rotary(k_ref[0].astype(F32))
    v_scr[pl.ds(pad, seq), :] = v_ref[0].astype(F32)

    lane = lax.broadcasted_iota(jnp.int32, (QBLK, LANES), 1)
    head0 = lane < HEAD_DIM

    for g, (window, dil) in enumerate(DILATED_PAIRS):
        nblk, lead, nkeys = _dilated_geometry(seq, window, dil)

        def group(gi, carry, g=g, dil=dil, lead=lead, nkeys=nkeys, nblk=nblk):
            units = []
            for u in range(DILATED_GROUP):
                idx = gi * DILATED_GROUP + u
                cls = idx // nblk
                n = idx % nblk
                q_start = cls + dil * QBLK * n
                k_start = pad + cls + dil * (QBLK * n - lead)
                if dil == 1:
                    q_rows = pl.ds(q_start, QBLK)
                    k_rows = pl.ds(k_start, nkeys)
                else:
                    q_rows = pl.ds(q_start, QBLK, stride=dil)
                    k_rows = pl.ds(k_start, nkeys, stride=dil)
                which = jnp.where(n == 0, 0, jnp.where(n == nblk - 1, 2, 1)) if nblk > 1 else 0
                units.append((q_rows, q_scr[q_rows, :], k_scr[k_rows, :].astype(BF16),
                              v_scr[k_rows, :].astype(BF16), which))
            scores = []
            for _, q, kw, _, which in units:
                for hmask in (head0, jnp.logical_not(head0)):
                    qh = jnp.where(hmask, q, 0.0).astype(BF16)
                    s = lax.dot_general(qh, kw, (((1,), (1,)), ((), ())),
                                        preferred_element_type=F32)
                    scores.append(s + mask_refs[g][which])
            probs, invs, lses = [], [], []
            for s in scores:
                m = jnp.max(s, axis=-1, keepdims=True)
                p = jnp.exp(s - m)
                den = jnp.sum(p, axis=-1, keepdims=True)
                probs.append(p.astype(BF16))
                invs.append(1.0 / den)
                lses.append(m + jnp.log(den))
            for u, (q_rows, _, _, vw, _) in enumerate(units):
                o0 = jnp.dot(probs[2 * u], vw, preferred_element_type=F32) * invs[2 * u]
                o1 = jnp.dot(probs[2 * u + 1], vw, preferred_element_type=F32) * invs[2 * u + 1]
                og_scr[g, q_rows, :] = jnp.where(head0, o0, o1)
                lse_scr[g, q_rows, :] = jnp.where(head0, lses[2 * u], lses[2 * u + 1])
            return carry

        lax.fori_loop(0, dil * nblk // DILATED_GROUP, group, 0)

    l0, l1, l2 = lse_scr[0], lse_scr[1], lse_scr[2]
    m = jnp.maximum(jnp.maximum(l0, l1), l2)
    e0, e1, e2 = jnp.exp(l0 - m), jnp.exp(l1 - m), jnp.exp(l2 - m)
    out = (e0 * og_scr[0] + e1 * og_scr[1] + e2 * og_scr[2]) * (1.0 / (e0 + e1 + e2))
    o_ref[0] = out.astype(o_ref.dtype)


def _dilated_geometry(seq, window, dil):
    radius = window // (2 * dil)
    nblk = seq // dil // QBLK
    lead = radius if nblk > 1 else 0
    return nblk, lead, QBLK + 2 * lead


def _dilated_masks(seq):
    out = []
    for window, dil in DILATED_PAIRS:
        radius = window // (2 * dil)
        nblk, lead, nkeys = _dilated_geometry(seq, window, dil)
        qi = np.arange(QBLK)[:, None]
        kj = np.arange(nkeys)[None, :]
        band = np.abs(kj - lead - qi) <= radius
        tabs = []
        for n in ((0, 1, nblk - 1) if nblk > 1 else (0,)):
            key_pos = QBLK * n + kj - lead
            tabs.append(band & (key_pos >= 0) & (key_pos < nblk * QBLK))
        out.append(jnp.asarray(np.where(np.stack(tabs), 0.0, NEG), F32))
    return out


def _dilated_attention(attn, cos, s1, s2, masks):
    b, s, _ = attn.shape
    pad = max((w // (2 * d)) * d for w, d in DILATED_PAIRS)
    hp = W_MIX // LANES
    per_cb = COLBLK // LANES
    col = lambda cb: pl.BlockSpec((1, s, LANES), lambda bi, h, cb=cb: (bi, 0, cb * per_cb + h))
    tab = pl.BlockSpec((1, s, LANES), lambda bi, h: (bi, 0, 0))
    const = lambda shape: pl.BlockSpec(shape, lambda bi, h: (0,) * len(shape))
    return pl.pallas_call(
        functools.partial(_dilated_kernel, seq=s, pad=pad),
        grid=(b, hp),
        in_specs=[col(CB_AQ), col(CB_AK), col(CB_AV), tab, tab, tab] + [const(m.shape) for m in masks],
        out_specs=pl.BlockSpec((1, s, LANES), lambda bi, h: (bi, 0, h)),
        out_shape=jax.ShapeDtypeStruct((b, s, W_MIX), BF16),
        scratch_shapes=[pltpu.VMEM((s, LANES), F32),
                        pltpu.VMEM((s + 2 * pad, LANES), F32),
                        pltpu.VMEM((s + 2 * pad, LANES), F32),
                        pltpu.VMEM((len(DILATED_PAIRS), s, LANES), F32),
                        pltpu.VMEM((len(DILATED_PAIRS), s, LANES), F32)],
        compiler_params=_cparams(("parallel", "parallel")),
        name="dilated_attn",
    )(attn, attn, attn, cos, s1, s2, *masks)


def _rope_tables(positions):
    half = ROT_DIM // 2
    lane = np.arange(LANES) % HEAD_DIM
    inv_freq = ROPE_THETA ** (-jnp.asarray(lane % half, F32) * 2.0 / ROT_DIM)
    ang = positions.astype(F32)[..., None] * inv_freq
    cos, sin = jnp.cos(ang), jnp.sin(ang)
    return (jnp.where(lane < ROT_DIM, cos, 1.0), jnp.where(lane < half, -sin, 0.0),
            jnp.where((lane >= half) & (lane < ROT_DIM), sin, 0.0))


def _natten_kernel(q_ref, k_ref, v_ref, bias_ref, o_ref, *, rows, win_rows):
    lane = lax.broadcasted_iota(jnp.int32, (GRID_W, LANES), 1)
    head0 = lane < HEAD_DIM
    scale = HEAD_DIM ** -0.5
    nkeys = win_rows * GRID_W

    def group(gi, carry):
        units = []
        for u in range(NATTEN_GROUP):
            r = gi * NATTEN_GROUP + u
            r_start = jnp.clip(r - win_rows // 2, 0, rows - win_rows)
            q_rows = pl.ds(pl.multiple_of(r * GRID_W, GRID_W), GRID_W)
            k_rows = pl.ds(pl.multiple_of(r_start * GRID_W, GRID_W), nkeys)
            units.append((q_rows, r - r_start, q_ref[0, q_rows, :],
                          k_ref[0, k_rows, :], v_ref[0, k_rows, :]))
        scores = []
        for _, delta, q, kw, _ in units:
            q = q * scale
            zero = jnp.zeros_like(q)
            q2 = jnp.concatenate([jnp.where(head0, q, zero), jnp.where(head0, zero, q)], axis=0)
            s2 = lax.dot_general(q2, kw, (((1,), (1,)), ((), ())), preferred_element_type=F32)
            scores += [s2[:GRID_W] + bias_ref[0, delta], s2[GRID_W:] + bias_ref[1, delta]]
        probs, invs = [], []
        for s in scores:
            m = jnp.max(s, axis=-1, keepdims=True)
            p = jnp.exp(s - m)
            probs.append(p.astype(BF16))
            invs.append(1.0 / jnp.sum(p, axis=-1, keepdims=True))
        for u, (q_rows, _, _, _, vw) in enumerate(units):
            o2 = jnp.dot(jnp.concatenate(probs[2 * u:2 * u + 2], axis=0), vw,
                         preferred_element_type=F32)
            o_ref[0, q_rows, :] = jnp.where(head0, o2[:GRID_W] * invs[2 * u],
                                            o2[GRID_W:] * invs[2 * u + 1]).astype(o_ref.dtype)
        return carry

    lax.fori_loop(0, rows // NATTEN_GROUP, group, 0)


def _natten_bias_tables(rpb, rows):
    wr = min(NA_WIN_ROWS, rows)
    wc = NA_WIN_COLS
    cols = np.arange(GRID_W)
    c_start = np.clip(cols - wc // 2, 0, GRID_W - wc)
    col_in = (cols[None, :] >= c_start[:, None]) & (cols[None, :] < c_start[:, None] + wc)
    coff = np.clip(cols[None, :] - cols[:, None], -(wc - 1), wc - 1) + wc - 1
    onehot = (coff[None] == np.arange(2 * wc - 1)[:, None, None]).astype(np.float32)
    by_col = jnp.einsum("lhrc,cqk->lhqrk", rpb.astype(F32), onehot, precision=lax.Precision.HIGHEST)
    by_col = jnp.where(col_in[:, None, :], by_col, NEG)
    top = NA_WIN_ROWS - 1
    tabs = jnp.stack([by_col[:, :, :, top - dl:top - dl + wr] for dl in range(wr)], axis=2)
    return tabs.reshape(rpb.shape[0], rpb.shape[1], wr, GRID_W, wr * GRID_W)


def _neighborhood_attention(attn, bias_tab):
    b, s, _ = attn.shape
    rows = s // GRID_W
    wr = min(NA_WIN_ROWS, rows)
    hp = W_MIX // LANES
    per_cb = COLBLK // LANES
    col = lambda cb: pl.BlockSpec((1, s, LANES), lambda bi, h, cb=cb: (bi, 0, cb * per_cb + h))
    return pl.pallas_call(
        functools.partial(_natten_kernel, rows=rows, win_rows=wr),
        grid=(b, hp),
        in_specs=[col(CB_NQ), col(CB_NK), col(CB_NV),
                  pl.BlockSpec((LANES // HEAD_DIM, wr, GRID_W, wr * GRID_W),
                               lambda bi, h: (h, 0, 0, 0))],
        out_specs=pl.BlockSpec((1, s, LANES), lambda bi, h: (bi, 0, h)),
        out_shape=jax.ShapeDtypeStruct((b, s, W_MIX), BF16),
        compiler_params=_cparams(("parallel", "parallel")),
        name="natten",
    )(attn, attn, attn, bias_tab)


def _seg_sum(x, ones_bd):
    return jnp.dot(x.astype(BF16), ones_bd, preferred_element_type=F32)


def _pack_rows(x):
    half = x.shape[0] // 2
    bits = lax.bitcast_convert_type(x.astype(BF16).astype(F32), jnp.uint32)
    return (bits[:half] >> 16) | (bits[half:] & jnp.uint32(0xFFFF0000))


def _unpack_rows(u, high):
    shift = jnp.where(high, 0, 16).astype(jnp.uint32)
    return lax.bitcast_convert_type((u << shift) & jnp.uint32(0xFFFF0000), F32)


def _rwkv_pre_kernel(r_ref, k_ref, v_ref, l_ref,
                     rp_ref, kp_ref, vp_ref, lp_ref,
                     rn_ref, kn_ref, vn_ref, ln_ref,
                     mu_ref, mul_ref, wc_ref, w0_ref, a0_ref, kk_ref, ka_ref, rk_ref, ones_ref,
                     w_o, k_o, a_o, b_o, bonus_o, pr_o, pv_o, *, tm):
    i = pl.program_id(1)
    first = i == 0
    last = i == pl.num_programs(1) - 1
    half = COLBLK // 2
    row = lax.broadcasted_iota(jnp.int32, (tm, half), 0)
    ones_bd = ones_ref[:half, :half]

    def neighbour(x, p_ref, n_ref, cols, d):
        if d == 0:
            edge = jnp.where(first, 0.0, p_ref[0, SUBLANES - 1:SUBLANES, cols])
            return jnp.where(row == 0, edge, pltpu.roll(x, 1, axis=0))
        edge = jnp.where(last, 0.0, n_ref[0, 0:1, cols])
        return jnp.where(row == tm - 1, edge, pltpu.roll(x, tm - 1, axis=0))

    lora = slice(0, half)
    pl_ = l_ref[0, :, lora]
    tw, ta = [], []
    for d in range(2):
        xl = pl_ + (neighbour(pl_, lp_ref, ln_ref, lora, d) - pl_) * mul_ref[d:d + 1, lora]
        tw.append(jnp.tanh(xl[:, :LANES]).astype(BF16))
        ta.append(xl[:, LANES:].astype(BF16))

    for c in range(COLBLK // half):
        cols = slice(c * half, (c + 1) * half)
        pr, pk, pv = r_ref[0, :, cols], k_ref[0, :, cols], v_ref[0, :, cols]
        bonus = jnp.zeros((tm, half), F32)
        for d in range(2):
            r = pr + (neighbour(pr, rp_ref, rn_ref, cols, d) - pr) * mu_ref[d, 0:1, cols]
            k = pk + (neighbour(pk, kp_ref, kn_ref, cols, d) - pk) * mu_ref[d, 1:2, cols]
            v = pv + (neighbour(pv, vp_ref, vn_ref, cols, d) - pv) * mu_ref[d, 2:3, cols]
            wz = w0_ref[d:d + 1, cols] + jnp.dot(tw[d], wc_ref[d, :LANES, cols],
                                                 preferred_element_type=F32)
            az = a0_ref[d:d + 1, cols] + jnp.dot(
                ta[d], wc_ref[d, LANES:2 * LANES, COLBLK + c * half:COLBLK + (c + 1) * half],
                preferred_element_type=F32)
            decay = jnp.exp(-DECAY_SCALE * _sigmoid(wz))
            a = _sigmoid(az)
            kk = k * kk_ref[:, cols]
            kk = kk * lax.rsqrt(jnp.maximum(_seg_sum(kk * kk, ones_bd), 1e-24))
            k2 = k * (1.0 + (a - 1.0) * ka_ref[:, cols])
            bonus = bonus + _seg_sum(r * k2 * rk_ref[:, cols], ones_bd) * v
            w_o[d, 0, :, cols] = decay
            k_o[d, 0, :, cols] = _pack_rows(k2)
            a_o[d, 0, :, cols] = _pack_rows(-kk)
            b_o[d, 0, :, cols] = _pack_rows(kk * a)
        bonus_o[0, :, cols] = bonus
        pr_o[0, :, cols] = _pack_rows(pr)
        pv_o[0, :, cols] = pv


def _rwkv_pre(proj, mu_rkv, mu_lora, w_comb, w0, a0, k_k, k_a, r_k, ones_bd, *, tm=PACK_TILE):
    b, s, _ = proj.shape
    nsub = tm // SUBLANES
    last_sub = s // SUBLANES - 1
    main = lambda cb: pl.BlockSpec((1, tm, COLBLK), lambda bi, i, cb=cb: (bi, i, cb))
    prev = lambda cb: pl.BlockSpec(
        (1, SUBLANES, COLBLK), lambda bi, i, cb=cb: (bi, jnp.maximum(i * nsub - 1, 0), cb))
    nxt = lambda cb: pl.BlockSpec(
        (1, SUBLANES, COLBLK), lambda bi, i, cb=cb: (bi, jnp.minimum((i + 1) * nsub, last_sub), cb))
    cbs = (CB_PR, CB_PK, CB_PV, CB_LORA)
    const = lambda shape: pl.BlockSpec(shape, lambda bi, i: (0,) * len(shape))
    tok = lambda rows: pl.BlockSpec((1, rows, COLBLK), lambda bi, i: (bi, i, 0))
    dir_out = lambda rows: pl.BlockSpec((2, 1, rows, COLBLK), lambda bi, i: (0, bi, i, 0))
    u32 = jnp.uint32
    return pl.pallas_call(
        functools.partial(_rwkv_pre_kernel, tm=tm),
        grid=(b, s // tm),
        in_specs=([main(cb) for cb in cbs] + [prev(cb) for cb in cbs] + [nxt(cb) for cb in cbs]
                  + [const((2, 3, COLBLK)), const((2, COLBLK)), const((2, 2 * LANES, 2 * COLBLK)),
                     const((2, COLBLK)), const((2, COLBLK)), const((1, COLBLK)), const((1, COLBLK)),
                     const((1, COLBLK)), const((COLBLK, COLBLK))]),
        out_specs=[dir_out(tm), dir_out(tm // 2), dir_out(tm // 2), dir_out(tm // 2),
                   tok(tm), tok(tm // 2), tok(tm)],
        out_shape=[jax.ShapeDtypeStruct((2, b, s, COLBLK), F32)]
                  + [jax.ShapeDtypeStruct((2, b, s // 2, COLBLK), u32)] * 3
                  + [jax.ShapeDtypeStruct((b, s, COLBLK), F32),
                     jax.ShapeDtypeStruct((b, s // 2, COLBLK), u32),
                     jax.ShapeDtypeStruct((b, s, COLBLK), F32)],
        compiler_params=_cparams(("parallel", "parallel")),
        name="rwkv_pre",
    )(*([proj] * 12), mu_rkv, mu_lora, w_comb, w0, a0, k_k, k_a, r_k, ones_bd)


def _rwkv_scan_kernel(pr_ref, w_ref, k_ref, pv_ref, a_ref, b_ref, an_ref, mur_ref, muv_ref, y_ref,
                      st_scr, sa_scr, rows_scr, prevr_scr, prevv_scr, *, tt, seq):
    d = pl.program_id(0)
    i = pl.program_id(1)
    n = HEAD_DIM
    nt = seq // tt
    chunks_per_tile = PACK_TILE // tt
    chunk = i + d * (nt - 1 - 2 * i)
    high = chunk % chunks_per_tile >= chunks_per_tile // 2
    t_after = jnp.where(d == 0, (i + 1) * tt, (nt - 1 - i) * tt - 1)
    high_after = t_after % PACK_TILE >= PACK_TILE // 2

    @pl.when(i == 0)
    def _():
        st_scr[...] = jnp.zeros_like(st_scr)
        sa_scr[...] = jnp.zeros_like(sa_scr)
        prevr_scr[...] = jnp.zeros_like(prevr_scr)
        prevv_scr[...] = jnp.zeros_like(prevv_scr)

    mu_r, mu_v = mur_ref[0], muv_ref[0]

    def row(slot, q, k):
        return jnp.broadcast_to(rows_scr[slot, q, k // SUBLANES, pl.ds(k % SUBLANES, 1), :],
                                sa_scr.shape[1:])

    def step(j, t, a_next, sa, gam, pr_prev, pv_prev):
        slot = j % 2
        gam = gam * w_ref[0, t]
        inv = 1.0 / gam
        pr, pv = _unpack_rows(pr_ref[t], high), pv_ref[t]
        r = pr + (pr_prev - pr) * mu_r
        v = pv + (pv_prev - pv) * mu_v
        rows_scr[slot, 0] = _unpack_rows(b_ref[0, t], high) * inv
        rows_scr[slot, 1] = _unpack_rows(k_ref[0, t], high) * inv
        rows_scr[slot, 2] = r * gam
        rows_scr[slot, 3] = a_next * gam
        y = jnp.zeros_like(v)
        sa_next = [jnp.zeros_like(v), jnp.zeros_like(v)]
        for k in range(n):
            new = st_scr[k] + (sa * row(slot, 0, k) + v * row(slot, 1, k))
            st_scr[k] = new
            y = y + new * row(slot, 2, k)
            sa_next[k % 2] = sa_next[k % 2] + new * row(slot, 3, k)
        y_ref[0, t] = y
        return sa_next[0] + sa_next[1], gam, pr, pv

    def body(j, carry):
        t = j + d * (tt - 1 - 2 * j)
        return step(j, t, _unpack_rows(a_ref[0, t + 1 - 2 * d], high), *carry)

    carry = (sa_scr[...], jnp.ones(sa_scr.shape, F32), prevr_scr[...], prevv_scr[...])
    carry = lax.fori_loop(0, tt - 1, body, carry)
    sa, gam, pr, pv = step(tt - 1, (1 - d) * (tt - 1), _unpack_rows(an_ref[0, 0], high_after), *carry)
    sa_scr[...] = sa
    prevr_scr[...] = pr
    prevv_scr[...] = pv
    rows_scr[0, 0] = gam
    for k in range(n):
        st_scr[k] = st_scr[k] * row(0, 0, k)


def _rwkv_scan(pr, w, k, pv, a, b, mu_r, mu_v, *, tt=64):
    s, n, c = pv.shape
    nt = s // tt
    cpt = PACK_TILE // tt
    tile = (n // SUBLANES, SUBLANES, c)
    order = lambda d, i: i + d * (nt - 1 - 2 * i)
    packed = lambda ci: (ci // cpt) * (cpt // 2) + ci % (cpt // 2)
    dir_spec = pl.BlockSpec((1, tt) + tile, lambda d, i: (d, order(d, i), 0, 0, 0))
    shared_spec = pl.BlockSpec((tt,) + tile, lambda d, i: (order(d, i), 0, 0, 0))
    dir_packed = pl.BlockSpec((1, tt) + tile, lambda d, i: (d, packed(order(d, i)), 0, 0, 0))
    shared_packed = pl.BlockSpec((tt,) + tile, lambda d, i: (packed(order(d, i)), 0, 0, 0))
    per_dir = pl.BlockSpec((1,) + tile, lambda d, i: (d, 0, 0, 0))

    def token_after(d, i):
        t = jnp.where(d == 0, jnp.minimum((i + 1) * tt, s - 1), jnp.maximum((nt - 1 - i) * tt - 1, 0))
        return (d, (t // PACK_TILE) * (PACK_TILE // 2) + t % (PACK_TILE // 2), 0, 0, 0)

    k, a, b = (t.reshape((2, s // 2) + tile) for t in (k, a, b))
    y = pl.pallas_call(
        functools.partial(_rwkv_scan_kernel, tt=tt, seq=s),
        grid=(2, nt),
        in_specs=[shared_packed, dir_spec, dir_packed, shared_spec, dir_packed, dir_packed,
                  pl.BlockSpec((1, 1) + tile, token_after), per_dir, per_dir],
        out_specs=dir_spec,
        out_shape=jax.ShapeDtypeStruct((2, s) + tile, F32),
        scratch_shapes=[pltpu.VMEM((n,) + tile, F32), pltpu.VMEM(tile, F32),
                        pltpu.VMEM((2, 4) + tile, F32), pltpu.VMEM(tile, F32), pltpu.VMEM(tile, F32)],
        compiler_params=_cparams(("parallel", "arbitrary")),
        name="rwkv_scan",
    )(pr.reshape((s // 2,) + tile), w.reshape((2, s) + tile), k, pv.reshape((s,) + tile), a, b, a,
      mu_r.reshape((2,) + tile), mu_v.reshape((2,) + tile))
    return y.reshape(2, s, n, c)


def _to_chain_lanes(t):
    *lead, b, s, _ = t.shape
    k = len(lead)
    t = t.reshape(*lead, b, s, N_HEADS, HEAD_DIM)
    t = t.transpose(*range(k), k + 1, k + 3, k, k + 2)
    return t.reshape(*lead, s, HEAD_DIM, b * N_HEADS)


def _param_to_chain_lanes(p, b):
    lead = p.shape[:-1]
    p = jnp.swapaxes(p.reshape(*lead, N_HEADS, HEAD_DIM), -1, -2)
    return jnp.broadcast_to(p[..., None, :], lead + (HEAD_DIM, b, N_HEADS)).reshape(
        *lead, HEAD_DIM, b * N_HEADS)


def _from_chain_lanes(t, b):
    _, s, _, _ = t.shape
    t = t.reshape(2, s, HEAD_DIM, b, N_HEADS).transpose(0, 3, 1, 4, 2)
    return t.reshape(2, b, s, W_MIX)


def _merge_kernel(y_ref, bonus_ref, l_ref, gz0_ref, gz1_ref, gz2_ref, oa_ref, ob_ref, x_ref, gt_ref,
                  gnw_ref, gnb_ref, ones_ref, gc_ref, wb_ref, wo_ref, o_ref):
    half = COLBLK // 2
    ones_bd = ones_ref[:half, :half]
    inv_n = 1.0 / HEAD_DIM
    halves = []
    for c in range(COLBLK // half):
        cols = slice(c * half, (c + 1) * half)
        part = bonus_ref[0, :, cols]
        for d in range(2):
            y = y_ref[d, 0, :, cols]
            mean = _seg_sum(y, ones_bd) * inv_n
            yc = y - mean
            var = _seg_sum(yc * yc, ones_bd) * inv_n
            part = part + (yc * lax.rsqrt(var + GN_EPS)) * gnw_ref[:, cols] + gnb_ref[:, cols]
        halves.append(part)
    acc = jnp.concatenate(halves, axis=1)
    g = jnp.dot(_sigmoid(l_ref[0]).astype(BF16), gc_ref[...], preferred_element_type=F32)
    o_c = acc * g
    merged = (_sigmoid(gz0_ref[0])
              * jnp.dot(oa_ref[0], wb_ref[0], preferred_element_type=F32)
              + _sigmoid(gz1_ref[0])
              * jnp.dot(ob_ref[0], wb_ref[1], preferred_element_type=F32)
              + _sigmoid(gz2_ref[0])
              * jnp.dot(o_c.astype(BF16), wb_ref[2], preferred_element_type=F32))
    out = jnp.dot(merged.astype(BF16), wo_ref[...], preferred_element_type=F32)
    o_ref[0] = x_ref[0] + gt_ref[0] * out


def _merge(y, bonus, proj, o_a, o_b, x, gate, gn_w, gn_b, ones_bd, g_comb, w_branch, w_out, *, tm=512):
    b, s, d = x.shape
    gz_per = d // COLBLK
    tok = lambda width, cb: pl.BlockSpec((1, tm, width), lambda bi, i, cb=cb: (bi, i, cb))
    const = lambda shape: pl.BlockSpec(shape, lambda bi, i: (0,) * len(shape))
    return pl.pallas_call(
        _merge_kernel,
        grid=(b, s // tm),
        in_specs=[pl.BlockSpec((2, 1, tm, COLBLK), lambda bi, i: (0, bi, i, 0)),
                  tok(COLBLK, 0),
                  tok(COLBLK, CB_LORA),
                  tok(d, CB_GZ // gz_per), tok(d, CB_GZ // gz_per + 1), tok(d, CB_GZ // gz_per + 2),
                  tok(COLBLK, 0), tok(COLBLK, 0), tok(d, 0),
                  pl.BlockSpec((1, 1, d), lambda bi, i: (bi, 0, 0)),
                  const((1, COLBLK)), const((1, COLBLK)), const((COLBLK, COLBLK)),
                  const((COLBLK, COLBLK)), const((3, COLBLK, d)), const((d, d))],
        out_specs=pl.BlockSpec((1, tm, d), lambda bi, i: (bi, i, 0)),
        out_shape=jax.ShapeDtypeStruct((b, s, d), F32),
        compiler_params=_cparams(("parallel", "parallel")),
        name="merge_out",
    )(y, bonus, proj, proj, proj, proj, o_a, o_b, x, gate, gn_w, gn_b, ones_bd, g_comb, w_branch, w_out)


def _split_w_in(w_in):
    n_lora = 4 * DECAY_LORA + GATE_LORA
    n_rest = N_ATTN + CB_LORA * COLBLK + n_lora
    rest, gates = w_in[..., N_ATTN:n_rest], w_in[..., n_rest:]
    zpad = jnp.zeros(w_in.shape[:-1] + (COLBLK - n_lora,), w_in.dtype)
    return (w_in[..., :N_ATTN].astype(BF16),
            jnp.concatenate([rest, zpad, gates], axis=-1).astype(BF16))


def _lora_params(mu_w, mu_a, w_up, a_up, g_up):
    r = DECAY_LORA
    depth = mu_w.shape[0]
    mu_l = jnp.zeros((depth, 2, COLBLK), F32)
    w_comb = jnp.zeros((depth, 2, 2 * LANES, 2 * COLBLK), F32)
    for d in range(2):
        wcol, acol = d * r, 2 * r + d * r
        mu_l = mu_l.at[:, d, wcol:wcol + r].set(mu_w[:, d]).at[:, d, acol:acol + r].set(mu_a[:, d])
        w_comb = w_comb.at[:, d, wcol:wcol + r, :COLBLK].set(w_up[:, d])
        w_comb = w_comb.at[:, d, acol:acol + r, COLBLK:].set(a_up[:, d])
    g_comb = jnp.zeros((depth, COLBLK, COLBLK), F32).at[:, 4 * r:4 * r + GATE_LORA].set(g_up)
    return mu_l, w_comb.astype(BF16), g_comb.astype(BF16)


def _head_block_ones():
    seg = np.arange(COLBLK) // HEAD_DIM
    return jnp.asarray(seg[:, None] == seg[None, :], dtype=BF16)


def kernel(x, c, positions, ada_w, ada_b, norm_gains, ffn_wi, ffn_wo, w_in, rpb, mu_rkv, mu_w, mu_a,
           w0, w_up, a0, a_up, g_up, k_k, k_a, r_k, gn_w, gn_b, w_branch, w_out, final_norm):
    depth = ada_w.shape[0]
    b, s, d = x.shape
    assert d == 2 * COLBLK and s % 1024 == 0 and s % (GRID_W * NA_WIN_ROWS) == 0

    mod = _modulation(c, ada_w, ada_b).reshape(depth, b, N_MOD, 1, d)
    cos, s1, s2 = _rope_tables(positions)
    ones_bd = _head_block_ones()
    bias_tabs = _natten_bias_tables(rpb, s // GRID_W)
    dil_masks = _dilated_masks(s)
    ffn_wi_bf, ffn_wo_bf = ffn_wi.astype(BF16), ffn_wo.astype(BF16)
    w_branch_bf, w_out_bf = w_branch.astype(BF16), w_out.astype(BF16)
    w_attn, w_rest = _split_w_in(w_in)
    mu_l, w_comb, g_comb = _lora_params(mu_w, mu_a, w_up, a_up, g_up)
    mu_r, mu_v = _param_to_chain_lanes(mu_rkv[:, :, 0], b), _param_to_chain_lanes(mu_rkv[:, :, 2], b)
    row = lambda t: t.reshape(1, -1)

    for l in range(depth):
        sh1, sc1, gt1, sh2, sc2, gt2, sh3, sc3, gt3 = (mod[l, :, i] for i in range(N_MOD))

        x = _ffn(x, norm_gains[l, 0], sh1, sc1, gt1, ffn_wi_bf[l, 0], ffn_wo_bf[l, 0])

        attn, proj = _in_proj(x, norm_gains[l, 1], sh2, sc2, w_attn[l], w_rest[l])
        o_a = _dilated_attention(attn, cos, s1, s2, dil_masks)
        o_b = _neighborhood_attention(attn, bias_tabs[l])
        decay, k2, a_vec, b_vec, bonus, pr, pv = _rwkv_pre(
            proj, mu_rkv[l], mu_l[l], w_comb[l], w0[l], a0[l], row(k_k[l]), row(k_a[l]), row(r_k[l]),
            ones_bd)
        y = _rwkv_scan(_to_chain_lanes(pr), _to_chain_lanes(decay), _to_chain_lanes(k2),
                       _to_chain_lanes(pv), _to_chain_lanes(a_vec), _to_chain_lanes(b_vec),
                       mu_r[l], mu_v[l])
        y = _from_chain_lanes(y, b)
        x = _merge(y, bonus, proj, o_a, o_b, x, gt2, row(gn_w[l]), row(gn_b[l]), ones_bd, g_comb[l],
                   w_branch_bf[l], w_out_bf[l])

        x = _ffn(x, norm_gains[l, 2], sh3, sc3, gt3, ffn_wi_bf[l, 1], ffn_wo_bf[l, 1],
                 final_norm if l == depth - 1 else None)
    return x
```
